```python
import jax, jax.numpy as jnp
from jax import lax
import numpy as np

D_MODEL = 2048
BATCH = 8
SEQ = 8192
DEPTH = 4

N_MIXERS = 2
N_ATTN_LAYERS = (DEPTH + N_MIXERS - 1) // N_MIXERS
N_POOL_LAYERS = DEPTH // N_MIXERS
HEAD_DIM = 64
N_HEADS = D_MODEL // HEAD_DIM
N_KV_HEADS = N_HEADS // 8
GQA_GROUP = N_HEADS // N_KV_HEADS
WINDOW = 128
BLOCK = WINDOW
N_BUCKETS = 32
MAX_DISTANCE = 128
POOL_WINDOWS = (2, 4, 8, 16)
N_POOL_GROUPS = len(POOL_WINDOWS)
POOL_GROUP_DIM = D_MODEL // N_POOL_GROUPS
D_FF = 5632
CONV_WIDTH = 3
EPS = 1e-6
NEG_INF = -1e30

kernel_name = "hybrid_swa_sink_pool_convffn"


def rmsnorm(x, gain):
    xf = x.astype(jnp.float32)
    y = xf * lax.rsqrt(jnp.mean(xf * xf, axis=-1, keepdims=True) + EPS)
    return (y * gain.astype(jnp.float32)).astype(x.dtype)


def _t5_band_buckets():
    i = np.arange(BLOCK)[:, None]
    j = np.arange(2 * BLOCK)[None, :]
    n = np.maximum(BLOCK + i - j, 0)
    max_exact = N_BUCKETS // 2
    nf = np.maximum(n, 1).astype(np.float32)
    large = max_exact + (np.log(nf / max_exact) / np.log(MAX_DISTANCE / max_exact)
                         * (N_BUCKETS - max_exact)).astype(np.int32)
    large = np.minimum(large, N_BUCKETS - 1)
    return np.where(n < max_exact, n, large).astype(np.int32)


def _band_mask(n_blocks):
    i = np.arange(BLOCK)[:, None]
    j = np.arange(2 * BLOCK)[None, :]
    dist = BLOCK + i - j
    in_win = (dist >= 0) & (dist < WINDOW)
    key_pos = np.arange(n_blocks)[:, None, None] * BLOCK - BLOCK + j[None]
    return in_win[None] & (key_pos >= 0)


def sliding_window_attention(h, w_qkv, q_gain, k_gain, sinks, rel_bias, w_o):
    B, S, _ = h.shape
    nb = S // BLOCK
    qkv = h @ w_qkv
    q, k, v = jnp.split(qkv, [N_HEADS * HEAD_DIM, (N_HEADS + N_KV_HEADS) * HEAD_DIM], axis=-1)
    q = rmsnorm(q.reshape(B, S, N_HEADS, HEAD_DIM), q_gain)
    k = rmsnorm(k.reshape(B, S, N_KV_HEADS, HEAD_DIM), k_gain)
    v = v.reshape(B, S, N_KV_HEADS, HEAD_DIM)
    q = q.reshape(B, nb, BLOCK, N_KV_HEADS, GQA_GROUP, HEAD_DIM)

    def band(t):
        t = t.reshape(B, nb, BLOCK, N_KV_HEADS, HEAD_DIM)
        prev = jnp.pad(t[:, :-1], ((0, 0), (1, 0), (0, 0), (0, 0), (0, 0)))
        return jnp.concatenate([prev, t], axis=2)

    kb, vb = band(k), band(v)
    s = jnp.einsum('bnqkgd,bnskd->bnkgqs', q, kb).astype(jnp.float32) * (HEAD_DIM ** -0.5)
    bias = rel_bias[:, _t5_band_buckets()].astype(jnp.float32)
    bias = bias.reshape(N_KV_HEADS, GQA_GROUP, BLOCK, 2 * BLOCK)
    mask = jnp.asarray(_band_mask(nb))[None, :, None, None]
    s = jnp.where(mask, s + bias, NEG_INF)
    sink = sinks.astype(jnp.float32).reshape(N_KV_HEADS, GQA_GROUP)[:, :, None, None]
    m = jnp.maximum(jnp.max(s, axis=-1, keepdims=True), sink)
    p = jnp.exp(s - m)
    p = p / (jnp.sum(p, axis=-1, keepdims=True) + jnp.exp(sink - m))
    o = jnp.einsum('bnkgqs,bnskd->bnqkgd', p.astype(vb.dtype), vb)
    return o.reshape(B, S, N_HEADS * HEAD_DIM) @ w_o


def multiscale_pool_mixer(h, w_pool, scale):
    B, S, _ = h.shape
    hg = h.reshape(B, S, N_POOL_GROUPS, POOL_GROUP_DIM)
    hf = hg.astype(jnp.float32)
    c = jnp.cumsum(hf, axis=1)
    t = jnp.arange(S)
    means = []
    for g, w in enumerate(POOL_WINDOWS):
        cg = c[:, :, g]
        lag = jnp.pad(cg, ((0, 0), (w, 0), (0, 0)))[:, :S]
        cnt = jnp.minimum(t + 1, w).astype(jnp.float32)[None, :, None]
        means.append((cg - lag) / cnt)
    d = (jnp.stack(means, axis=2) - hf).astype(h.dtype)
    y = jnp.einsum('bsgc,gce->bsge', d, w_pool).reshape(B, S, D_MODEL)
    return y * scale


def conv_gated_mlp(h, w_up, conv_w, conv_b, w_down):
    S = h.shape[1]
    u = h @ w_up
    up = jnp.pad(u, ((0, 0), (CONV_WIDTH - 1, 0), (0, 0)))
    u = sum(conv_w[k] * up[:, k:k + S] for k in range(CONV_WIDTH)) + conv_b
    gate, val = jnp.split(u, 2, axis=-1)
    return (jax.nn.silu(gate) * val) @ w_down


def _fwd_setup_inputs(seed: int = 0) -> dict:
    key = jax.random.key(seed)
    ks = jax.random.split(key, 16)
    f32 = jnp.float32
    qkv_out = (N_HEADS + 2 * N_KV_HEADS) * HEAD_DIM
    return {
        "x": jax.random.normal(ks[0], (BATCH, SEQ, D_MODEL), f32),
        "norm_mix": 1.0 + 0.05 * jax.random.normal(ks[1], (DEPTH, D_MODEL), f32),
        "norm_ffn": 1.0 + 0.05 * jax.random.normal(ks[2], (DEPTH, D_MODEL), f32),
        "rel_bias": 0.5 * jax.random.normal(ks[3], (N_HEADS, N_BUCKETS), f32),
        "attn_w_qkv": jax.random.normal(ks[4], (N_ATTN_LAYERS, D_MODEL, qkv_out), f32) * D_MODEL ** -0.5,
        "attn_q_gain": 1.0 + 0.05 * jax.random.normal(ks[5], (N_ATTN_LAYERS, HEAD_DIM), f32),
        "attn_k_gain": 1.0 + 0.05 * jax.random.normal(ks[6], (N_ATTN_LAYERS, HEAD_DIM), f32),
        "attn_sinks": 0.5 * jax.random.normal(ks[7], (N_ATTN_LAYERS, N_HEADS), f32),
        "attn_w_o": jax.random.normal(ks[8], (N_ATTN_LAYERS, N_HEADS * HEAD_DIM, D_MODEL), f32) * (N_HEADS * HEAD_DIM) ** -0.5,
        "pool_w": jax.random.normal(ks[9], (N_POOL_LAYERS, N_POOL_GROUPS, POOL_GROUP_DIM, POOL_GROUP_DIM), f32) * POOL_GROUP_DIM ** -0.5,
        "pool_scale": 1.0 + 0.05 * jax.random.normal(ks[10], (N_POOL_LAYERS, D_MODEL), f32),
        "ffn_w_up": jax.random.normal(ks[11], (DEPTH, D_MODEL, 2 * D_FF), f32) * D_MODEL ** -0.5,
        "ffn_conv_w": jax.random.normal(ks[12], (DEPTH, CONV_WIDTH, 2 * D_FF), f32) * CONV_WIDTH ** -0.5,
        "ffn_conv_b": 0.02 * jax.random.normal(ks[13], (DEPTH, 2 * D_FF), f32),
        "ffn_w_down": jax.random.normal(ks[14], (DEPTH, D_FF, D_MODEL), f32) * D_FF ** -0.5,
    }


def _fwd_reference(x, norm_mix, norm_ffn, rel_bias, attn_w_qkv, attn_q_gain, attn_k_gain,
              attn_sinks, attn_w_o, pool_w, pool_scale, ffn_w_up, ffn_conv_w, ffn_conv_b,
              ffn_w_down):
    for i in range(DEPTH):
        h = rmsnorm(x, norm_mix[i])
        j = i // N_MIXERS
        if i % N_MIXERS == 0:
            mix = sliding_window_attention(h, attn_w_qkv[j], attn_q_gain[j], attn_k_gain[j],
                                           attn_sinks[j], rel_bias, attn_w_o[j])
        else:
            mix = multiscale_pool_mixer(h, pool_w[j], pool_scale[j])
        x = x + mix
        h = rmsnorm(x, norm_ffn[i])
        x = x + conv_gated_mlp(h, ffn_w_up[i], ffn_conv_w[i], ffn_conv_b[i], ffn_w_down[i])
    return x


import jax as _jax
import jax.numpy as _jnp

TWIN_FORMAT = 'train_step'
FWD_PARAMS = ['x', 'norm_mix', 'norm_ffn', 'rel_bias', 'attn_w_qkv', 'attn_q_gain', 'attn_k_gain', 'attn_sinks', 'attn_w_o', 'pool_w', 'pool_scale', 'ffn_w_up', 'ffn_conv_w', 'ffn_conv_b', 'ffn_w_down']
TWIN_WEIGHTS = ['norm_mix', 'norm_ffn', 'rel_bias', 'attn_w_qkv', 'attn_q_gain', 'attn_k_gain', 'attn_sinks', 'attn_w_o', 'pool_w', 'pool_scale', 'ffn_w_up', 'ffn_conv_w', 'ffn_conv_b', 'ffn_w_down']
TWIN_DIFF_INPUT = 'x'
TWIN_INPUTS = ['x', 'norm_mix', 'norm_ffn', 'rel_bias', 'attn_w_qkv', 'attn_q_gain', 'attn_k_gain', 'attn_sinks', 'attn_w_o', 'pool_w', 'pool_scale', 'ffn_w_up', 'ffn_conv_w', 'ffn_conv_b', 'ffn_w_down', 'loss_target', 'm_norm_mix', 'm_norm_ffn', 'm_rel_bias', 'm_attn_w_qkv', 'm_attn_q_gain', 'm_attn_k_gain', 'm_attn_sinks', 'm_attn_w_o', 'm_pool_w', 'm_pool_scale', 'm_ffn_w_up', 'm_ffn_conv_w', 'm_ffn_conv_b', 'm_ffn_w_down', 'v_norm_mix', 'v_norm_ffn', 'v_rel_bias', 'v_attn_w_qkv', 'v_attn_q_gain', 'v_attn_k_gain', 'v_attn_sinks', 'v_attn_w_o', 'v_pool_w', 'v_pool_scale', 'v_ffn_w_up', 'v_ffn_conv_w', 'v_ffn_conv_b', 'v_ffn_w_down']
TWIN_OUTPUTS = ['loss', 'grad_x', 'grad_norm_mix', 'grad_norm_ffn', 'grad_rel_bias', 'grad_attn_w_qkv', 'grad_attn_q_gain', 'grad_attn_k_gain', 'grad_attn_sinks', 'grad_attn_w_o', 'grad_pool_w', 'grad_pool_scale', 'grad_ffn_w_up', 'grad_ffn_conv_w', 'grad_ffn_conv_b', 'grad_ffn_w_down', 'delta_norm_mix', 'delta_norm_ffn', 'delta_rel_bias', 'delta_attn_w_qkv', 'delta_attn_q_gain', 'delta_attn_k_gain', 'delta_attn_sinks', 'delta_attn_w_o', 'delta_pool_w', 'delta_pool_scale', 'delta_ffn_w_up', 'delta_ffn_conv_w', 'delta_ffn_conv_b', 'delta_ffn_w_down', 'new_m_norm_mix', 'new_m_norm_ffn', 'new_m_rel_bias', 'new_m_attn_w_qkv', 'new_m_attn_q_gain', 'new_m_attn_k_gain', 'new_m_attn_sinks', 'new_m_attn_w_o', 'new_m_pool_w', 'new_m_pool_scale', 'new_m_ffn_w_up', 'new_m_ffn_conv_w', 'new_m_ffn_conv_b', 'new_m_ffn_w_down', 'new_v_norm_mix', 'new_v_norm_ffn', 'new_v_rel_bias', 'new_v_attn_w_qkv', 'new_v_attn_q_gain', 'new_v_attn_k_gain', 'new_v_attn_sinks', 'new_v_attn_w_o', 'new_v_pool_w', 'new_v_pool_scale', 'new_v_ffn_w_up', 'new_v_ffn_conv_w', 'new_v_ffn_conv_b', 'new_v_ffn_w_down']
TWIN_LEAF_KINDS = {'loss': 'loss', 'grad_x': 'grad_x', 'grad_norm_mix': 'grad_w', 'grad_norm_ffn': 'grad_w', 'grad_rel_bias': 'grad_w', 'grad_attn_w_qkv': 'grad_w', 'grad_attn_q_gain': 'grad_w', 'grad_attn_k_gain': 'grad_w', 'grad_attn_sinks': 'grad_w', 'grad_attn_w_o': 'grad_w', 'grad_pool_w': 'grad_w', 'grad_pool_scale': 'grad_w', 'grad_ffn_w_up': 'grad_w', 'grad_ffn_conv_w': 'grad_w', 'grad_ffn_conv_b': 'grad_w', 'grad_ffn_w_down': 'grad_w', 'delta_norm_mix': 'delta_w', 'delta_norm_ffn': 'delta_w', 'delta_rel_bias': 'delta_w', 'delta_attn_w_qkv': 'delta_w', 'delta_attn_q_gain': 'delta_w', 'delta_attn_k_gain': 'delta_w', 'delta_attn_sinks': 'delta_w', 'delta_attn_w_o': 'delta_w', 'delta_pool_w': 'delta_w', 'delta_pool_scale': 'delta_w', 'delta_ffn_w_up': 'delta_w', 'delta_ffn_conv_w': 'delta_w', 'delta_ffn_conv_b': 'delta_w', 'delta_ffn_w_down': 'delta_w', 'new_m_norm_mix': 'new_m', 'new_m_norm_ffn': 'new_m', 'new_m_rel_bias': 'new_m', 'new_m_attn_w_qkv': 'new_m', 'new_m_attn_q_gain': 'new_m', 'new_m_attn_k_gain': 'new_m', 'new_m_attn_sinks': 'new_m', 'new_m_attn_w_o': 'new_m', 'new_m_pool_w': 'new_m', 'new_m_pool_scale': 'new_m', 'new_m_ffn_w_up': 'new_m', 'new_m_ffn_conv_w': 'new_m', 'new_m_ffn_conv_b': 'new_m', 'new_m_ffn_w_down': 'new_m', 'new_v_norm_mix': 'new_v', 'new_v_norm_ffn': 'new_v', 'new_v_rel_bias': 'new_v', 'new_v_attn_w_qkv': 'new_v', 'new_v_attn_q_gain': 'new_v', 'new_v_attn_k_gain': 'new_v', 'new_v_attn_sinks': 'new_v', 'new_v_attn_w_o': 'new_v', 'new_v_pool_w': 'new_v', 'new_v_pool_scale': 'new_v', 'new_v_ffn_w_up': 'new_v', 'new_v_ffn_conv_w': 'new_v', 'new_v_ffn_conv_b': 'new_v', 'new_v_ffn_w_down': 'new_v'}


def _forward(args):
    return _fwd_reference(*[args[k] for k in FWD_PARAMS])


def _output_shape():
    def fwd():
        inp = _fwd_setup_inputs(0)
        return _fwd_reference(*[inp[k] for k in FWD_PARAMS])
    out = _jax.eval_shape(fwd)
    return out.shape, out.dtype

N_MICROBATCH = 1
ADAM_LR = 0.001
ADAM_B1 = 0.9
ADAM_B2 = 0.999
ADAM_EPS = 1e-08
ADAM_WD = 0.01
ADAM_STEP = 10
PER_EXAMPLE_BATCH_AXIS = {'x': 0, 'loss_target': 0}
SHARED_INPUTS = []
_WEIGHT_DTYPES = {'norm_mix': _jnp.float32, 'norm_ffn': _jnp.float32, 'rel_bias': _jnp.float32, 'attn_w_qkv': _jnp.float32, 'attn_q_gain': _jnp.float32, 'attn_k_gain': _jnp.float32, 'attn_sinks': _jnp.float32, 'attn_w_o': _jnp.float32, 'pool_w': _jnp.float32, 'pool_scale': _jnp.float32, 'ffn_w_up': _jnp.float32, 'ffn_conv_w': _jnp.float32, 'ffn_conv_b': _jnp.float32, 'ffn_w_down': _jnp.float32}
MOMENT_SCALE = {'norm_mix': 1.856668e+01, 'norm_ffn': 2.594839e+01, 'rel_bias': 1.117679e+00, 'attn_w_qkv': 2.412801e-01, 'attn_q_gain': 1.341134e+01, 'attn_k_gain': 1.348153e+01, 'attn_sinks': 7.199837e-01, 'attn_w_o': 1.777965e-01, 'pool_w': 2.075722e+00, 'pool_scale': 2.621574e+01, 'ffn_w_up': 2.393513e-01, 'ffn_conv_w': 3.510046e+00, 'ffn_conv_b': 3.203062e+00, 'ffn_w_down': 3.973817e-01}


def _to_microbatches(a, axis):
    t = _jnp.moveaxis(a, axis, 0)
    t = t.reshape((N_MICROBATCH, t.shape[0] // N_MICROBATCH) + t.shape[1:])
    return _jnp.moveaxis(t, 1, axis + 1)


def setup_inputs(seed: int = 0) -> dict:
    inp = _fwd_setup_inputs(seed)
    key = _jax.random.fold_in(_jax.random.key(seed), 7919)
    shape, _ = _output_shape()
    out = dict(inp)
    out["loss_target"] = _jax.random.normal(_jax.random.fold_in(key, 0), shape, _jnp.float32)
    for i, name in enumerate(TWIN_WEIGHTS):
        w = inp[name].astype(_jnp.float32)
        if MOMENT_SCALE is None:
            s = _jnp.sqrt(_jnp.mean(_jnp.square(w)) + 1e-30)
        else:
            s = MOMENT_SCALE[name]
        km, kv = _jax.random.split(_jax.random.fold_in(key, i + 1))
        out[name] = w
        out["m_" + name] = s * _jax.random.normal(km, w.shape, _jnp.float32)
        out["v_" + name] = (s * s) * _jax.random.uniform(kv, w.shape, _jnp.float32, 0.5, 1.5)
    if N_MICROBATCH > 1:
        for name, axis in PER_EXAMPLE_BATCH_AXIS.items():
            out[name] = _to_microbatches(out[name], axis)
    return {'x': out['x'], 'norm_mix': out['norm_mix'], 'norm_ffn': out['norm_ffn'], 'rel_bias': out['rel_bias'], 'attn_w_qkv': out['attn_w_qkv'], 'attn_q_gain': out['attn_q_gain'], 'attn_k_gain': out['attn_k_gain'], 'attn_sinks': out['attn_sinks'], 'attn_w_o': out['attn_w_o'], 'pool_w': out['pool_w'], 'pool_scale': out['pool_scale'], 'ffn_w_up': out['ffn_w_up'], 'ffn_conv_w': out['ffn_conv_w'], 'ffn_conv_b': out['ffn_conv_b'], 'ffn_w_down': out['ffn_w_down'], 'loss_target': out['loss_target'], 'm_norm_mix': out['m_norm_mix'], 'm_norm_ffn': out['m_norm_ffn'], 'm_rel_bias': out['m_rel_bias'], 'm_attn_w_qkv': out['m_attn_w_qkv'], 'm_attn_q_gain': out['m_attn_q_gain'], 'm_attn_k_gain': out['m_attn_k_gain'], 'm_attn_sinks': out['m_attn_sinks'], 'm_attn_w_o': out['m_attn_w_o'], 'm_pool_w': out['m_pool_w'], 'm_pool_scale': out['m_pool_scale'], 'm_ffn_w_up': out['m_ffn_w_up'], 'm_ffn_conv_w': out['m_ffn_conv_w'], 'm_ffn_conv_b': out['m_ffn_conv_b'], 'm_ffn_w_down': out['m_ffn_w_down'], 'v_norm_mix': out['v_norm_mix'], 'v_norm_ffn': out['v_norm_ffn'], 'v_rel_bias': out['v_rel_bias'], 'v_attn_w_qkv': out['v_attn_w_qkv'], 'v_attn_q_gain': out['v_attn_q_gain'], 'v_attn_k_gain': out['v_attn_k_gain'], 'v_attn_sinks': out['v_attn_sinks'], 'v_attn_w_o': out['v_attn_w_o'], 'v_pool_w': out['v_pool_w'], 'v_pool_scale': out['v_pool_scale'], 'v_ffn_w_up': out['v_ffn_w_up'], 'v_ffn_conv_w': out['v_ffn_conv_w'], 'v_ffn_conv_b': out['v_ffn_conv_b'], 'v_ffn_w_down': out['v_ffn_w_down']}


def _loss(weights, diff, rest, loss_target):
    with _jax.named_scope("forward"):
        args = {**rest, TWIN_DIFF_INPUT: diff, **{k: w.astype(_WEIGHT_DTYPES[k]) for k, w in weights.items()}}
        y = _forward(args)
    with _jax.named_scope("loss_head"):
        err = _jnp.square(y.astype(_jnp.float32) - loss_target)
        return 0.5 * _jnp.sum(_jnp.mean(err, axis=-1)) if err.ndim else 0.5 * err


def _adamw(w, g, m, v):
    m = ADAM_B1 * m + (1.0 - ADAM_B1) * g
    v = ADAM_B2 * v + (1.0 - ADAM_B2) * _jnp.square(g)
    m_hat = m / (1.0 - ADAM_B1 ** ADAM_STEP)
    v_hat = v / (1.0 - ADAM_B2 ** ADAM_STEP)
    delta = -ADAM_LR * (m_hat / (_jnp.sqrt(v_hat) + ADAM_EPS) + ADAM_WD * w)
    return delta, m, v


def reference(x, norm_mix, norm_ffn, rel_bias, attn_w_qkv, attn_q_gain, attn_k_gain, attn_sinks, attn_w_o, pool_w, pool_scale, ffn_w_up, ffn_conv_w, ffn_conv_b, ffn_w_down, loss_target, m_norm_mix, m_norm_ffn, m_rel_bias, m_attn_w_qkv, m_attn_q_gain, m_attn_k_gain, m_attn_sinks, m_attn_w_o, m_pool_w, m_pool_scale, m_ffn_w_up, m_ffn_conv_w, m_ffn_conv_b, m_ffn_w_down, v_norm_mix, v_norm_ffn, v_rel_bias, v_attn_w_qkv, v_attn_q_gain, v_attn_k_gain, v_attn_sinks, v_attn_w_o, v_pool_w, v_pool_scale, v_ffn_w_up, v_ffn_conv_w, v_ffn_conv_b, v_ffn_w_down):
    given = dict(x=x, norm_mix=norm_mix, norm_ffn=norm_ffn, rel_bias=rel_bias, attn_w_qkv=attn_w_qkv, attn_q_gain=attn_q_gain, attn_k_gain=attn_k_gain, attn_sinks=attn_sinks, attn_w_o=attn_w_o, pool_w=pool_w, pool_scale=pool_scale, ffn_w_up=ffn_w_up, ffn_conv_w=ffn_conv_w, ffn_conv_b=ffn_conv_b, ffn_w_down=ffn_w_down, loss_target=loss_target, m_norm_mix=m_norm_mix, m_norm_ffn=m_norm_ffn, m_rel_bias=m_rel_bias, m_attn_w_qkv=m_attn_w_qkv, m_attn_q_gain=m_attn_q_gain, m_attn_k_gain=m_attn_k_gain, m_attn_sinks=m_attn_sinks, m_attn_w_o=m_attn_w_o, m_pool_w=m_pool_w, m_pool_scale=m_pool_scale, m_ffn_w_up=m_ffn_w_up, m_ffn_conv_w=m_ffn_conv_w, m_ffn_conv_b=m_ffn_conv_b, m_ffn_w_down=m_ffn_w_down, v_norm_mix=v_norm_mix, v_norm_ffn=v_norm_ffn, v_rel_bias=v_rel_bias, v_attn_w_qkv=v_attn_w_qkv, v_attn_q_gain=v_attn_q_gain, v_attn_k_gain=v_attn_k_gain, v_attn_sinks=v_attn_sinks, v_attn_w_o=v_attn_w_o, v_pool_w=v_pool_w, v_pool_scale=v_pool_scale, v_ffn_w_up=v_ffn_w_up, v_ffn_conv_w=v_ffn_conv_w, v_ffn_conv_b=v_ffn_conv_b, v_ffn_w_down=v_ffn_w_down)
    weights = {n: given[n] for n in TWIN_WEIGHTS}
    shared = {n: given[n] for n in SHARED_INPUTS}
    per_example = {n: given[n] for n in ['x']}
    grad_fn = _jax.value_and_grad(_loss, argnums=(0, 1))

    def one_microbatch(ex, loss_target):
        ex = dict(ex)
        diff = ex.pop(TWIN_DIFF_INPUT)
        return grad_fn(weights, diff, {**shared, **ex}, loss_target)

    if N_MICROBATCH == 1:
        loss, (grad_w, grad_x) = one_microbatch(per_example, given["loss_target"])
    else:
        def body(carry, xs):
            loss_sum, grad_sum = carry
            l_k, (gw_k, gx_k) = one_microbatch(xs[0], xs[1])
            with _jax.named_scope("update"):
                return (loss_sum + l_k, _jax.tree.map(_jnp.add, grad_sum, gw_k)), gx_k

        init = (_jnp.zeros((), _jnp.float32), _jax.tree.map(_jnp.zeros_like, weights))
        (loss, grad_w), grad_x = _jax.lax.scan(body, init, (per_example, given["loss_target"]))
    with _jax.named_scope("update"):
        delta_w, new_m, new_v = {}, {}, {}
        for n in TWIN_WEIGHTS:
            delta_w[n], new_m[n], new_v[n] = _adamw(weights[n], grad_w[n], given["m_" + n], given["v_" + n])
    return (loss, grad_x, *[grad_w[n] for n in TWIN_WEIGHTS], *[delta_w[n] for n in TWIN_WEIGHTS],
            *[new_m[n] for n in TWIN_WEIGHTS], *[new_v[n] for n in TWIN_WEIGHTS])
```

```python
import numpy as np
import jax
import jax.numpy as jnp
from jax import lax
from jax.experimental import pallas as pl
from jax.experimental.pallas import tpu as pltpu

F32 = jnp.float32
BF16 = jnp.bfloat16
MESH = pl.DeviceIdType.MESH

NDEV = 8
HEAD_DIM = 64
GQA_GROUP = 8
WINDOW = 128
N_BUCKETS = 32
MAX_DISTANCE = 128
POOL_WINDOWS = (2, 4, 8, 16)
POOL_HALO = 32
EPS = 1e-6
NEG_INF = -1e30
ADAM_LR = 0.001
ADAM_B1 = 0.9
ADAM_B2 = 0.999
ADAM_EPS = 1e-08
ADAM_WD = 0.01
ADAM_STEP = 10

V7X_VMEM_BYTES = 64 * 1024 * 1024
VMEM_LIMIT = V7X_VMEM_BYTES - 8 * 1024 * 1024
LANES = 128
SUBLANES = 8
PACK_ALIGN = SUBLANES * LANES

NN = (((1,), (0,)), ((), ()))
NT = (((1,), (1,)), ((), ()))
TN = (((0,), (0,)), ((), ()))


def _tile(dim, pref, align):
    t = min(pref, dim)
    t -= t % align
    while t >= align:
        if dim % t == 0:
            return t
        t -= align
    return dim


def _cparams(*sem):
    return pltpu.CompilerParams(dimension_semantics=sem, vmem_limit_bytes=VMEM_LIMIT)


def _bf(v):
    return v if v.dtype == BF16 else v.astype(BF16)


def _mm(name, a, b, *, grid, a_spec, b_spec, o_spec, out_shape, contract, acc_shape, res=None):
    nk = grid[2]

    def body(*refs):
        if res is not None:
            a_ref, b_ref, r_ref, o_ref = refs[:4]
            scr = refs[4:]
        else:
            a_ref, b_ref, o_ref = refs[:3]
            r_ref = None
            scr = refs[3:]
        part = lax.dot_general(_bf(a_ref[...]), _bf(b_ref[...]), contract, preferred_element_type=F32)

        def finish(acc):
            if r_ref is not None:
                acc = acc + r_ref[...]
            o_ref[...] = acc.astype(o_ref.dtype)

        if nk == 1:
            finish(part)
        else:
            acc_ref = scr[0]
            k = pl.program_id(2)

            @pl.when(k == 0)
            def _():
                acc_ref[...] = part

            @pl.when(k > 0)
            def _():
                acc_ref[...] += part

            @pl.when(k == nk - 1)
            def _():
                finish(acc_ref[...])

    in_specs = [a_spec, b_spec] + ([o_spec] if res is not None else [])
    args = (a, b) + ((res,) if res is not None else ())
    return pl.pallas_call(
        body, name=name, grid=grid, in_specs=in_specs, out_specs=o_spec, out_shape=out_shape,
        scratch_shapes=[pltpu.VMEM(acc_shape, F32)] if nk > 1 else [],
        compiler_params=_cparams("parallel", "parallel", "arbitrary"),
    )(*args)


def _mm_rows(name, a, b, contract, out_dtype, res=None, tm_pref=512):
    S, K = a.shape
    N = b.shape[1] if contract == NN else b.shape[0]
    tm = _tile(S, tm_pref, 16)
    return _mm(name, a, b, grid=(1, S // tm, 1),
               a_spec=pl.BlockSpec((tm, K), lambda p, q, k: (q, 0)),
               b_spec=pl.BlockSpec(b.shape, lambda p, q, k: (0, 0)),
               o_spec=pl.BlockSpec((tm, N), lambda p, q, k: (q, 0)),
               out_shape=jax.ShapeDtypeStruct((S, N), out_dtype), contract=contract,
               acc_shape=(tm, N), res=res)


def _mm_tn(name, a, b, tm_pref=1024, tn_pref=1024, tk_pref=512):
    S, M = a.shape
    N = b.shape[1]
    tm, tn, tk = _tile(M, tm_pref, LANES), _tile(N, tn_pref, LANES), _tile(S, tk_pref, 16)
    return _mm(name, a, b, grid=(M // tm, N // tn, S // tk),
               a_spec=pl.BlockSpec((tk, tm), lambda p, q, k: (k, p)),
               b_spec=pl.BlockSpec((tk, tn), lambda p, q, k: (k, q)),
               o_spec=pl.BlockSpec((tm, tn), lambda p, q, k: (p, q)),
               out_shape=jax.ShapeDtypeStruct((M, N), F32), contract=TN, acc_shape=(tm, tn))


def _rmsnorm_fwd(name, x, gain):
    S, D = x.shape
    tm = _tile(S, 512, 16)

    def body(x_ref, g_ref, o_ref):
        xf = x_ref[...]
        r = lax.rsqrt(jnp.mean(xf * xf, axis=-1, keepdims=True) + EPS)
        o_ref[...] = (xf * r * g_ref[...]).astype(o_ref.dtype)

    return pl.pallas_call(
        body, name=name, grid=(S // tm,),
        in_specs=[pl.BlockSpec((tm, D), lambda i: (i, 0)), pl.BlockSpec((1, D), lambda i: (0, 0))],
        out_specs=pl.BlockSpec((tm, D), lambda i: (i, 0)),
        out_shape=jax.ShapeDtypeStruct((S, D), BF16), compiler_params=_cparams("parallel"),
    )(x, gain)


def _rms_bwd_math(dh, xf, gain):
    r = lax.rsqrt(jnp.mean(xf * xf, axis=-1, keepdims=True) + EPS)
    xhat = xf * r
    dxh = dh * gain
    dx = r * (dxh - xhat * jnp.mean(dxh * xhat, axis=-1, keepdims=True))
    return dx, jnp.sum(dh * xhat, axis=0, keepdims=True)


def _rmsnorm_bwd(name, dh, x, gain, dres):
    S, D = x.shape
    tm = _tile(S, 256, 8)

    def body(dh_ref, x_ref, g_ref, dr_ref, dx_ref, dg_ref):
        dx, dg = _rms_bwd_math(dh_ref[...], x_ref[...], g_ref[...])
        dx_ref[...] = dr_ref[...] + dx

        @pl.when(pl.program_id(0) == 0)
        def _():
            dg_ref[...] = dg

        @pl.when(pl.program_id(0) > 0)
        def _():
            dg_ref[...] += dg

    row = pl.BlockSpec((tm, D), lambda i: (i, 0))
    vec = pl.BlockSpec((1, D), lambda i: (0, 0))
    return pl.pallas_call(
        body, name=name, grid=(S // tm,), in_specs=[row, row, vec, row], out_specs=[row, vec],
        out_shape=[jax.ShapeDtypeStruct((S, D), F32), jax.ShapeDtypeStruct((1, D), F32)],
        compiler_params=_cparams("arbitrary"),
    )(dh, x, gain, dres)


def _loss_fwd_bwd(y, target):
    S, D = y.shape
    tm = _tile(S, 512, 8)

    def body(y_ref, t_ref, l_ref, dy_ref):
        e = y_ref[...] - t_ref[...]
        dy_ref[...] = e * (1.0 / D)
        part = 0.5 * jnp.sum(jnp.mean(e * e, axis=-1, keepdims=True), axis=0, keepdims=True)
        part = jnp.broadcast_to(part, (SUBLANES, LANES))

        @pl.when(pl.program_id(0) == 0)
        def _():
            l_ref[...] = part

        @pl.when(pl.program_id(0) > 0)
        def _():
            l_ref[...] += part

    row = pl.BlockSpec((tm, D), lambda i: (i, 0))
    return pl.pallas_call(
        body, name="loss", grid=(S // tm,), in_specs=[row, row],
        out_specs=[pl.BlockSpec((SUBLANES, LANES), lambda i: (0, 0)), row],
        out_shape=[jax.ShapeDtypeStruct((SUBLANES, LANES), F32), jax.ShapeDtypeStruct((S, D), F32)],
        compiler_params=_cparams("arbitrary"),
    )(y, target)


def _sigmoid(v):
    return 1.0 / (1.0 + jnp.exp(-v))


def _ffn_up_fwd(name, h2, wup4, layer, cw, cb):
    S, D = h2.shape
    CB = wup4.shape[3]
    NJ = NDEV // 2
    FF = NJ * CB
    tm = _tile(S, 512, 16)
    nI = S // tm

    def body(h_ref, wg_ref, wv_ref, cwg_ref, cwv_ref, cbg_ref, cbv_ref, u_ref, a_ref, ext_g, ext_v):
        @pl.when(pl.program_id(1) == 0)
        def _():
            ext_g[pl.ds(0, 8), :] = jnp.zeros((8, CB), F32)
            ext_v[pl.ds(0, 8), :] = jnp.zeros((8, CB), F32)

        h = h_ref[...]

        def conv(w_ref, cw_ref, cb_ref, ext, slot):
            u = jnp.dot(h, w_ref[...], preferred_element_type=F32)
            u_ref[slot] = u.astype(BF16)
            ext[pl.ds(8, tm), :] = u
            uc = (cw_ref[0:1, :] * ext[pl.ds(6, tm), :] + cw_ref[1:2, :] * ext[pl.ds(7, tm), :]
                  + cw_ref[2:3, :] * u + cb_ref[...])
            ext[pl.ds(0, 8), :] = ext[pl.ds(tm, 8), :]
            return uc

        gc = conv(wg_ref, cwg_ref, cbg_ref, ext_g, 0)
        vc = conv(wv_ref, cwv_ref, cbv_ref, ext_v, 1)
        a_ref[...] = (gc * _sigmoid(gc) * vc).astype(BF16)

    def wspec(off):
        return pl.BlockSpec((None, None, D, CB), lambda j, i: (j + off, layer, 0, 0))

    def cspec(rows, off):
        return pl.BlockSpec((rows, CB), lambda j, i: (0, j + off))

    return pl.pallas_call(
        body, name=name, grid=(NJ, nI),
        in_specs=[pl.BlockSpec((tm, D), lambda j, i: (i, 0)), wspec(0), wspec(NJ),
                  cspec(3, 0), cspec(3, NJ), cspec(1, 0), cspec(1, NJ)],
        out_specs=[pl.BlockSpec((2, tm, CB), lambda j, i: (0, i, j)),
                   pl.BlockSpec((tm, CB), lambda j, i: (i, j))],
        out_shape=[jax.ShapeDtypeStruct((2, S, FF), BF16), jax.ShapeDtypeStruct((S, FF), BF16)],
        scratch_shapes=[pltpu.VMEM((tm + 8, CB), F32), pltpu.VMEM((tm + 8, CB), F32)],
        compiler_params=_cparams("parallel", "arbitrary"),
    )(h2, wup4, wup4, cw, cw, cb, cb)


def _ffn_da_bwd(name, dy, wd3, layer, u3, cw, cb):
    S, D = dy.shape
    FF = wd3.shape[1]
    NJ = NDEV // 2
    CB = FF // NJ
    tm = _tile(S, 256, 16)
    nI = S // tm
    hb = tm // 16

    def body(dy_ref, wd_ref, u_ref, uh_ref, cwg_ref, cwv_ref, cbg_ref, cbv_ref, du_ref, dc_ref,
             ext_ug, ext_uv, ext_dg, ext_dv):
        i = pl.program_id(1)
        first_tile = i == nI - 1

        @pl.when(i == 0)
        def _():
            ext_dg[pl.ds(tm, 8), :] = jnp.zeros((8, CB), F32)
            ext_dv[pl.ds(tm, 8), :] = jnp.zeros((8, CB), F32)
            dc_ref[...] = jnp.zeros(dc_ref.shape, F32)

        da = lax.dot_general(_bf(dy_ref[...]), wd_ref[...], NT, preferred_element_type=F32)

        def conv(slot, ext, cw_ref, cb_ref):
            u = u_ref[slot].astype(F32)
            halo = uh_ref[slot].astype(F32)[8:16, :]
            ext[pl.ds(0, 8), :] = jnp.where(first_tile, 0.0, halo)
            ext[pl.ds(8, tm), :] = u
            um2 = ext[pl.ds(6, tm), :]
            um1 = ext[pl.ds(7, tm), :]
            uc = cw_ref[0:1, :] * um2 + cw_ref[1:2, :] * um1 + cw_ref[2:3, :] * u + cb_ref[...]
            return u, um1, um2, uc

        ug, ug1, ug2, gc = conv(0, ext_ug, cwg_ref, cbg_ref)
        uv, uv1, uv2, vc = conv(1, ext_uv, cwv_ref, cbv_ref)
        sig = _sigmoid(gc)
        d_vc = da * (gc * sig)
        d_gc = da * vc * (sig * (1.0 + gc * (1.0 - sig)))

        def back(slot, d_uc, ext, cw_ref, u, um1, um2):
            ext[pl.ds(0, tm), :] = d_uc
            du = (cw_ref[2:3, :] * d_uc + cw_ref[1:2, :] * ext[pl.ds(1, tm), :]
                  + cw_ref[0:1, :] * ext[pl.ds(2, tm), :])
            ext[pl.ds(tm, 8), :] = d_uc[0:8, :]
            du_ref[slot] = du.astype(BF16)
            dc_ref[slot, 0:1, :] += jnp.sum(d_uc * um2, axis=0, keepdims=True)
            dc_ref[slot, 1:2, :] += jnp.sum(d_uc * um1, axis=0, keepdims=True)
            dc_ref[slot, 2:3, :] += jnp.sum(d_uc * u, axis=0, keepdims=True)
            dc_ref[slot, 3:4, :] += jnp.sum(d_uc, axis=0, keepdims=True)

        back(0, d_gc, ext_dg, cwg_ref, ug, ug1, ug2)
        back(1, d_vc, ext_dv, cwv_ref, uv, uv1, uv2)

    def rev(i):
        return nI - 1 - i

    def cspec(rows, off):
        return pl.BlockSpec((rows, CB), lambda j, i: (0, j + off))

    return pl.pallas_call(
        body, name=name, grid=(NJ, nI),
        in_specs=[pl.BlockSpec((tm, D), lambda j, i: (rev(i), 0)),
                  pl.BlockSpec((None, CB, D), lambda j, i: (layer, j, 0)),
                  pl.BlockSpec((2, tm, CB), lambda j, i: (0, rev(i), j)),
                  pl.BlockSpec((2, 16, CB), lambda j, i: (0, jnp.maximum(rev(i) * hb - 1, 0), j)),
                  cspec(3, 0), cspec(3, NJ), cspec(1, 0), cspec(1, NJ)],
        out_specs=[pl.BlockSpec((2, tm, CB), lambda j, i: (0, rev(i), j)),
                   pl.BlockSpec((2, 8, CB), lambda j, i: (0, 0, j))],
        out_shape=[jax.ShapeDtypeStruct((2, S, FF), BF16), jax.ShapeDtypeStruct((2, 8, FF), F32)],
        scratch_shapes=[pltpu.VMEM((tm + 8, CB), F32) for _ in range(4)],
        compiler_params=_cparams("parallel", "arbitrary"),
    )(dy, wd3, u3, u3, cw, cw, cb, cb)


def _ffn_down_fwd(name, a, wd3, layer, res):
    S, FF = a.shape
    D = wd3.shape[2]
    tm = _tile(S, 512, 16)
    tk = FF // (NDEV // 2)
    return _mm(name, a, wd3, grid=(1, S // tm, FF // tk),
               a_spec=pl.BlockSpec((tm, tk), lambda p, q, k: (q, k)),
               b_spec=pl.BlockSpec((None, tk, D), lambda p, q, k: (layer, k, 0)),
               o_spec=pl.BlockSpec((tm, D), lambda p, q, k: (q, 0)),
               out_shape=jax.ShapeDtypeStruct((S, D), F32), contract=NN, acc_shape=(tm, D), res=res)


def _ffn_dh_bwd(name, du3, wup4, layer):
    _, S, FF = du3.shape
    D, CB = wup4.shape[2], wup4.shape[3]
    NJ = NDEV // 2
    tm = _tile(S, 512, 16)
    return _mm(name, du3, wup4, grid=(1, S // tm, NDEV),
               a_spec=pl.BlockSpec((None, tm, CB), lambda p, q, k: (k // NJ, q, k % NJ)),
               b_spec=pl.BlockSpec((None, None, D, CB), lambda p, q, k: (k, layer, 0, 0)),
               o_spec=pl.BlockSpec((tm, D), lambda p, q, k: (q, 0)),
               out_shape=jax.ShapeDtypeStruct((S, D), F32), contract=NT, acc_shape=(tm, D))


def _ffn_dwup_bwd(name, h2, du3):
    S, D = h2.shape
    FF = du3.shape[2]
    NJ = NDEV // 2
    CB = FF // NJ
    tm, tk = _tile(D, 1024, LANES), _tile(S, 512, 16)
    return _mm(name, h2, du3, grid=(NDEV, D // tm, S // tk),
               a_spec=pl.BlockSpec((tk, tm), lambda p, q, k: (k, q)),
               b_spec=pl.BlockSpec((None, tk, CB), lambda p, q, k: (p // NJ, k, p % NJ)),
               o_spec=pl.BlockSpec((None, tm, CB), lambda p, q, k: (p, q, 0)),
               out_shape=jax.ShapeDtypeStruct((NDEV, D, CB), F32), contract=TN, acc_shape=(tm, CB))


def _t5_onehot():
    i = np.arange(WINDOW)[:, None]
    j = np.arange(2 * WINDOW)[None, :]
    n = np.maximum(WINDOW + i - j, 0)
    max_exact = N_BUCKETS // 2
    nf = np.maximum(n, 1).astype(np.float32)
    large = max_exact + (np.log(nf / max_exact) / np.log(MAX_DISTANCE / max_exact)
                         * (N_BUCKETS - max_exact)).astype(np.int32)
    large = np.minimum(large, N_BUCKETS - 1)
    bucket = np.where(n < max_exact, n, large).astype(np.int32).reshape(-1)
    return (np.arange(N_BUCKETS)[:, None] == bucket[None, :]).astype(np.float32)


def _bias_band(rel_bias, onehot):
    H = rel_bias.shape[0]
    n = onehot.shape[1]

    def body(r_ref, oh_ref, o_ref):
        o_ref[...] = jnp.dot(r_ref[...], oh_ref[...], preferred_element_type=F32,
                             precision=lax.Precision.HIGHEST)

    return pl.pallas_call(body, name="bias_band", out_shape=jax.ShapeDtypeStruct((H, n), F32),
                          compiler_params=pltpu.CompilerParams(vmem_limit_bytes=VMEM_LIMIT))(rel_bias, onehot)


def _bias_band_bwd(dband, onehot):
    H = dband.shape[0]

    def body(d_ref, oh_ref, o_ref):
        o_ref[...] = lax.dot_general(d_ref[...], oh_ref[...], NT, preferred_element_type=F32,
                                     precision=lax.Precision.HIGHEST)

    return pl.pallas_call(body, name="bias_band_bwd", out_shape=jax.ShapeDtypeStruct((H, N_BUCKETS), F32),
                          compiler_params=pltpu.CompilerParams(vmem_limit_bytes=VMEM_LIMIT))(dband, onehot)


def _band_valid(n):
    i = lax.broadcasted_iota(jnp.int32, (WINDOW, 2 * WINDOW), 0)
    j = lax.broadcasted_iota(jnp.int32, (WINDOW, 2 * WINDOW), 1)
    return (j > i) & (j <= i + WINDOW) & ((n > 0) | (j >= WINDOW))


def _head_norm(v, gain):
    r = lax.rsqrt(jnp.mean(v * v, axis=-1, keepdims=True) + EPS)
    vhat = v * r
    return r, vhat, vhat * gain


def _attn_fwd(name, qkv, bias, qg, kg, sinks):
    S, QW = qkv.shape
    H = bias.shape[0]
    D = H * HEAD_DIM
    KV = H // GQA_GROUP
    kvw = QW - D
    kvb = D // kvw
    nb = S // WINDOW
    scale = HEAD_DIM ** -0.5

    def body(q_ref, kc_ref, kp_ref, b_ref, qg_ref, kg_ref, s_ref, o_ref, l_ref):
        n = pl.program_id(0)
        valid = _band_valid(n)
        q = q_ref[...]
        kvc = kc_ref[...]
        kvp = kp_ref[...]
        outs, lses = [], []
        for kh in range(KV):
            ks = slice(kh * HEAD_DIM, (kh + 1) * HEAD_DIM)
            vs = slice((KV + kh) * HEAD_DIM, (KV + kh + 1) * HEAD_DIM)
            kb = jnp.concatenate([kvp[:, ks], kvc[:, ks]], axis=0)
            vb = jnp.concatenate([kvp[:, vs], kvc[:, vs]], axis=0).astype(BF16)
            kn = _head_norm(kb, kg_ref[...])[2].astype(BF16)
            for g in range(GQA_GROUP):
                h = kh * GQA_GROUP + g
                qn = _head_norm(q[:, h * HEAD_DIM:(h + 1) * HEAD_DIM], qg_ref[...])[2].astype(BF16)
                s = lax.dot_general(qn, kn, NT, preferred_element_type=F32) * scale
                s = jnp.where(valid, s + b_ref[h], NEG_INF)
                sink = s_ref[h:h + 1, :]
                m = jnp.maximum(jnp.max(s, axis=-1, keepdims=True), sink)
                p = jnp.exp(s - m)
                den = jnp.sum(p, axis=-1, keepdims=True) + jnp.exp(sink - m)
                p = p / den
                outs.append(jnp.dot(p.astype(BF16), vb, preferred_element_type=F32))
                lses.append(m + jnp.log(den))
        o_ref[...] = jnp.concatenate(outs, axis=1).astype(BF16)
        l_ref[...] = jnp.concatenate(lses, axis=1)

    const2 = lambda shape: pl.BlockSpec(shape, lambda n: (0, 0))
    return pl.pallas_call(
        body, name=name, grid=(nb,),
        in_specs=[pl.BlockSpec((WINDOW, D), lambda n: (n, 0)),
                  pl.BlockSpec((WINDOW, kvw), lambda n: (n, kvb)),
                  pl.BlockSpec((WINDOW, kvw), lambda n: (jnp.maximum(n - 1, 0), kvb)),
                  pl.BlockSpec(bias.shape, lambda n: (0, 0, 0)),
                  const2((1, HEAD_DIM)), const2((1, HEAD_DIM)), const2((H, 1))],
        out_specs=[pl.BlockSpec((WINDOW, D), lambda n: (n, 0)), pl.BlockSpec((WINDOW, H), lambda n: (n, 0))],
        out_shape=[jax.ShapeDtypeStruct((S, D), BF16), jax.ShapeDtypeStruct((S, H), F32)],
        compiler_params=_cparams("parallel"),
    )(qkv, qkv, qkv, bias, qg, kg, sinks)


def _attn_bwd(name, qkv, do, lse, bias, qg, kg, sinks):
    S, QW = qkv.shape
    H = bias.shape[0]
    D = H * HEAD_DIM
    KV = H // GQA_GROUP
    kvw = QW - D
    kvb = D // kvw
    nb = S // WINDOW
    scale = HEAD_DIM ** -0.5

    def body(q_ref, kc_ref, kp_ref, do_ref, l_ref, b_ref, qg_ref, kg_ref, s_ref,
             dq_ref, dkv_ref, db_ref, ds_ref, dqg_ref, dkg_ref, carry):
        n = pl.program_id(0)

        @pl.when(n == 0)
        def _():
            db_ref[...] = jnp.zeros(db_ref.shape, F32)
            ds_ref[...] = jnp.zeros(ds_ref.shape, F32)
            dqg_ref[...] = jnp.zeros(dqg_ref.shape, F32)
            dkg_ref[...] = jnp.zeros(dkg_ref.shape, F32)
            carry[...] = jnp.zeros(carry.shape, F32)

        @pl.when(n == nb)
        def _():
            dkv_ref[...] = carry[...].astype(BF16)

        @pl.when(n < nb)
        def _():
            valid = _band_valid(n)
            q = q_ref[...]
            kvc = kc_ref[...]
            kvp = kp_ref[...]
            do_all = do_ref[...]
            lse = l_ref[...]
            qgain = qg_ref[...]
            kgain = kg_ref[...]
            dqs, dsinks = [], []
            dk_parts, dv_parts = [], []
            dqg = jnp.zeros((1, HEAD_DIM), F32)
            dkg = jnp.zeros((1, HEAD_DIM), F32)
            for kh in range(KV):
                ks = slice(kh * HEAD_DIM, (kh + 1) * HEAD_DIM)
                vs = slice((KV + kh) * HEAD_DIM, (KV + kh + 1) * HEAD_DIM)
                kb = jnp.concatenate([kvp[:, ks], kvc[:, ks]], axis=0)
                vb = jnp.concatenate([kvp[:, vs], kvc[:, vs]], axis=0).astype(BF16)
                rk, khat, kn32 = _head_norm(kb, kgain)
                kn = kn32.astype(BF16)
                dkn = jnp.zeros((2 * WINDOW, HEAD_DIM), F32)
                dvb = jnp.zeros((2 * WINDOW, HEAD_DIM), F32)
                for g in range(GQA_GROUP):
                    h = kh * GQA_GROUP + g
                    hs = slice(h * HEAD_DIM, (h + 1) * HEAD_DIM)
                    rq, qhat, qn32 = _head_norm(q[:, hs], qgain)
                    qn = qn32.astype(BF16)
                    s = lax.dot_general(qn, kn, NT, preferred_element_type=F32) * scale
                    s = jnp.where(valid, s + b_ref[h], NEG_INF)
                    lse_h = lse[:, h:h + 1]
                    p = jnp.exp(s - lse_h)
                    do_h = do_all[:, hs]
                    dp = lax.dot_general(do_h, vb, NT, preferred_element_type=F32)
                    delta = jnp.sum(p * dp, axis=-1, keepdims=True)
                    ds = p * (dp - delta)
                    db_ref[h] += ds
                    psink = jnp.exp(s_ref[h:h + 1, :] - lse_h)
                    dsinks.append(-jnp.sum(psink * delta, axis=0, keepdims=True))
                    dsb = (ds * scale).astype(BF16)
                    dqn = jnp.dot(dsb, kn, preferred_element_type=F32)
                    dkn = dkn + lax.dot_general(dsb, qn, TN, preferred_element_type=F32)
                    dvb = dvb + lax.dot_general(p.astype(BF16), do_h, TN, preferred_element_type=F32)
                    dqg = dqg + jnp.sum(dqn * qhat, axis=0, keepdims=True)
                    dqh = dqn * qgain
                    dqs.append(rq * (dqh - qhat * jnp.mean(dqh * qhat, axis=-1, keepdims=True)))
                dkg = dkg + jnp.sum(dkn * khat, axis=0, keepdims=True)
                dkh = dkn * kgain
                dk_parts.append(rk * (dkh - khat * jnp.mean(dkh * khat, axis=-1, keepdims=True)))
                dv_parts.append(dvb)
            dq_ref[...] = jnp.concatenate(dqs, axis=1).astype(BF16)
            ds_ref[...] += jnp.concatenate(dsinks, axis=1)
            dqg_ref[...] += dqg
            dkg_ref[...] += dkg
            dkv = jnp.concatenate(dk_parts + dv_parts, axis=1)
            dkv_ref[...] = (carry[...] + dkv[0:WINDOW, :]).astype(BF16)
            carry[...] = dkv[WINDOW:2 * WINDOW, :]

    cur = lambda n: jnp.minimum(n, nb - 1)
    const2 = lambda shape: pl.BlockSpec(shape, lambda n: (0, 0))
    return pl.pallas_call(
        body, name=name, grid=(nb + 1,),
        in_specs=[pl.BlockSpec((WINDOW, D), lambda n: (cur(n), 0)),
                  pl.BlockSpec((WINDOW, kvw), lambda n: (cur(n), kvb)),
                  pl.BlockSpec((WINDOW, kvw), lambda n: (jnp.maximum(cur(n) - 1, 0), kvb)),
                  pl.BlockSpec((WINDOW, D), lambda n: (cur(n), 0)),
                  pl.BlockSpec((WINDOW, H), lambda n: (cur(n), 0)),
                  pl.BlockSpec(bias.shape, lambda n: (0, 0, 0)),
                  const2((1, HEAD_DIM)), const2((1, HEAD_DIM)), const2((H, 1))],
        out_specs=[pl.BlockSpec((WINDOW, D), lambda n: (cur(n), 0)),
                   pl.BlockSpec((WINDOW, kvw), lambda n: (jnp.maximum(n - 1, 0), 0)),
                   pl.BlockSpec(bias.shape, lambda n: (0, 0, 0)),
                   const2((1, H)), const2((1, HEAD_DIM)), const2((1, HEAD_DIM))],
        out_shape=[jax.ShapeDtypeStruct((S, D), BF16), jax.ShapeDtypeStruct((S, kvw), BF16),
                   jax.ShapeDtypeStruct(bias.shape, F32), jax.ShapeDtypeStruct((1, H), F32),
                   jax.ShapeDtypeStruct((1, HEAD_DIM), F32), jax.ShapeDtypeStruct((1, HEAD_DIM), F32)],
        scratch_shapes=[pltpu.VMEM((WINDOW, kvw), F32)],
        compiler_params=_cparams("arbitrary"),
    )(qkv, qkv, qkv, do, lse, bias, qg, kg, sinks)


def _window_sums(src, bufs, lo, n_rows, step_sign, col_groups):
    out = []
    for g, cols in enumerate(col_groups):
        prev = src
        for level in range(g + 1):
            k = step_sign * (1 << level)
            cur = bufs[level]
            cur[pl.ds(lo, n_rows), cols] = prev[pl.ds(lo, n_rows), cols] + prev[pl.ds(lo + k, n_rows), cols]
            prev = cur
        out.append(prev)
    return out


def _pool_fwd(name, x, gain, wp, scale):
    S, D = x.shape
    G, C = wp.shape[0], wp.shape[1]
    tm = _tile(S, 256, POOL_HALO)
    hb = tm // POOL_HALO
    HL = POOL_HALO
    groups = [slice(g * C, (g + 1) * C) for g in range(G)]

    def body(x_ref, xh_ref, g_ref, w_ref, sc_ref, o_ref, d_ref, ext, p2, p4, p8, p16):
        i = pl.program_id(0)
        gain_v = g_ref[...]
        xt = x_ref[...]
        h = _head_norm(xt, gain_v)[2]
        hh = _head_norm(xh_ref[...], gain_v)[2]
        ext[pl.ds(0, HL), :] = jnp.where(i > 0, hh, 0.0)
        ext[pl.ds(HL, tm), :] = h
        bufs = (p2, p4, p8, p16)
        for b in bufs:
            b[pl.ds(0, 8), :] = jnp.zeros((8, D), F32)
        sums = _window_sums(ext, bufs, 8, tm + HL - 8, -1, groups)
        t = i * tm + lax.broadcasted_iota(jnp.int32, (tm, 1), 0)
        for g, cols in enumerate(groups):
            cnt = jnp.minimum(t + 1, POOL_WINDOWS[g]).astype(F32)
            d = sums[g][pl.ds(HL, tm), cols] / cnt - h[:, cols]
            db = d.astype(BF16)
            d_ref[:, cols] = db
            y = jnp.dot(db, w_ref[g], preferred_element_type=F32)
            o_ref[:, cols] = xt[:, cols] + y * sc_ref[:, cols]

    row = pl.BlockSpec((tm, D), lambda i: (i, 0))
    vec = pl.BlockSpec((1, D), lambda i: (0, 0))
    return pl.pallas_call(
        body, name=name, grid=(S // tm,),
        in_specs=[row, pl.BlockSpec((HL, D), lambda i: (jnp.maximum(i * hb - 1, 0), 0)), vec,
                  pl.BlockSpec(wp.shape, lambda i: (0, 0, 0)), vec],
        out_specs=[row, row],
        out_shape=[jax.ShapeDtypeStruct((S, D), F32), jax.ShapeDtypeStruct((S, D), BF16)],
        scratch_shapes=[pltpu.VMEM((tm + HL, D), F32) for _ in range(5)],
        compiler_params=_cparams("parallel"),
    )(x, x, gain, wp, scale)


def _pool_bwd(name, dx1, x, gain, wp, scale, dsave):
    S, D = x.shape
    G, C = wp.shape[0], wp.shape[1]
    tm = _tile(S, 256, POOL_HALO)
    hb = tm // POOL_HALO
    HL = POOL_HALO
    nI = S // tm
    groups = [slice(g * C, (g + 1) * C) for g in range(G)]

    def body(dx_ref, dxh_ref, x_ref, g_ref, w_ref, sc_ref, ds_ref, o_ref, dw_ref, dsc_ref, dg_ref,
             ext, p2, p4, p8, p16):
        i = pl.program_id(0)

        @pl.when(i == 0)
        def _():
            dw_ref[...] = jnp.zeros(dw_ref.shape, F32)
            dsc_ref[...] = jnp.zeros(dsc_ref.shape, F32)
            dg_ref[...] = jnp.zeros(dg_ref.shape, F32)

        dx1t = dx_ref[...]
        sc = sc_ref[...]
        dys = (dx1t * sc).astype(BF16)
        dys_h = (dxh_ref[...] * sc).astype(BF16)
        t = i * tm + lax.broadcasted_iota(jnp.int32, (tm, 1), 0)
        th = (i + 1) * tm + lax.broadcasted_iota(jnp.int32, (HL, 1), 0)
        dds = []
        for g, cols in enumerate(groups):
            dsv = ds_ref[:, cols]
            y = jnp.dot(dsv, w_ref[g], preferred_element_type=F32)
            dsc_ref[:, cols] += jnp.sum(dx1t[:, cols] * y, axis=0, keepdims=True)
            dw_ref[g] += lax.dot_general(dsv, dys[:, cols], TN, preferred_element_type=F32)
            dd = lax.dot_general(dys[:, cols], w_ref[g], NT, preferred_element_type=F32)
            dd_h = lax.dot_general(dys_h[:, cols], w_ref[g], NT, preferred_element_type=F32)
            dds.append(dd)
            w = POOL_WINDOWS[g]
            ext[pl.ds(0, tm), cols] = dd / jnp.minimum(t + 1, w).astype(F32)
            e_h = dd_h / jnp.minimum(th + 1, w).astype(F32)
            ext[pl.ds(tm, HL), cols] = jnp.where(i < nI - 1, e_h, 0.0)
        bufs = (p2, p4, p8, p16)
        for b in bufs:
            b[pl.ds(tm + HL - 8, 8), :] = jnp.zeros((8, D), F32)
        sums = _window_sums(ext, bufs, 0, tm + HL - 8, 1, groups)
        dh = jnp.concatenate([sums[g][pl.ds(0, tm), cols] - dds[g] for g, cols in enumerate(groups)], axis=1)
        dx, dg = _rms_bwd_math(dh, x_ref[...], g_ref[...])
        o_ref[...] = dx1t + dx
        dg_ref[...] += dg

    row = pl.BlockSpec((tm, D), lambda i: (i, 0))
    vec = pl.BlockSpec((1, D), lambda i: (0, 0))
    last_h = S // HL - 1
    return pl.pallas_call(
        body, name=name, grid=(nI,),
        in_specs=[row, pl.BlockSpec((HL, D), lambda i: (jnp.minimum((i + 1) * hb, last_h), 0)), row, vec,
                  pl.BlockSpec(wp.shape, lambda i: (0, 0, 0)), vec, row],
        out_specs=[row, pl.BlockSpec(wp.shape, lambda i: (0, 0, 0)), vec, vec],
        out_shape=[jax.ShapeDtypeStruct((S, D), F32), jax.ShapeDtypeStruct(wp.shape, F32),
                   jax.ShapeDtypeStruct((1, D), F32), jax.ShapeDtypeStruct((1, D), F32)],
        scratch_shapes=[pltpu.VMEM((tm + HL, D), F32) for _ in range(5)],
        compiler_params=_cparams("arbitrary"),
    )(dx1, dx1, x, gain, wp, scale, dsave)


HBM_SPEC = pl.BlockSpec(memory_space=pltpu.HBM)


def _coords():
    return lax.axis_index("x"), lax.axis_index("y"), lax.axis_index("c")


def _allgather_big(shards):
    n = len(shards)

    def body(*refs):
        ins, outs = refs[:n], refs[n:2 * n]
        send_sems, recv_sems, local_sems = refs[2 * n:]
        x, y, c = _coords()
        me, xn, yn, dg = 4 * x + 2 * y + c, 4 * (1 - x) + 2 * y + c, 4 * x + 2 * (1 - y) + c, 4 * (1 - x) + 2 * (1 - y) + c
        XN, YN, SB = (1 - x, y, c), (x, 1 - y, c), (x, y, 1 - c)
        flip_c = lambda blk: blk + 1 - 2 * c

        def copy(o, k, src, dst, to):
            return pltpu.make_async_remote_copy(src_ref=src, dst_ref=dst, send_sem=send_sems.at[o * 8 + k],
                                                recv_sem=recv_sems.at[o * 8 + k], device_id=to, device_id_type=MESH)

        sends = []

        def start(cp):
            cp.start()
            sends.append(cp)

        locals_ = []
        for o in range(n):
            lc = pltpu.make_async_copy(ins[o], outs[o].at[me], local_sems.at[o])
            lc.start()
            locals_.append(lc)
            start(copy(o, 0, ins[o], outs[o].at[me], XN))
            start(copy(o, 1, ins[o], outs[o].at[me], YN))
            start(copy(o, 4, ins[o], outs[o].at[me], SB))
        for o in range(n):
            out = outs[o]
            copy(o, 0, ins[o], out.at[xn], XN).wait_recv()
            start(copy(o, 2, out.at[xn, 0], out.at[xn, 0], YN))
            start(copy(o, 5, out.at[xn], out.at[xn], SB))
            copy(o, 1, ins[o], out.at[yn], YN).wait_recv()
            start(copy(o, 3, out.at[yn, 1], out.at[yn, 1], XN))
            start(copy(o, 6, out.at[yn], out.at[yn], SB))
        for o in range(n):
            out = outs[o]
            copy(o, 2, out.at[dg, 0], out.at[dg, 0], YN).wait_recv()
            copy(o, 3, out.at[dg, 1], out.at[dg, 1], XN).wait_recv()
            start(copy(o, 7, out.at[dg], out.at[dg], SB))
        for o in range(n):
            out = outs[o]
            for k, blk in ((4, me), (5, xn), (6, yn), (7, dg)):
                copy(o, k, out.at[flip_c(blk)], out.at[flip_c(blk)], SB).wait_recv()
        for cp in sends:
            cp.wait_send()
        for lc in locals_:
            lc.wait()

    return pl.pallas_call(
        body, name="allgather_weights", in_specs=[HBM_SPEC] * n, out_specs=[HBM_SPEC] * n,
        out_shape=[jax.ShapeDtypeStruct((NDEV,) + s.shape, s.dtype) for s in shards],
        scratch_shapes=[pltpu.SemaphoreType.DMA((n * 8,)), pltpu.SemaphoreType.DMA((n * 8,)),
                        pltpu.SemaphoreType.DMA((n,))],
    )(*shards)


def _allgather_small(name, block):
    m_per, ncol = block.shape

    def body(x_ref, out_ref, send_sems, recv_sems, local_sem):
        x, y, c = _coords()
        me, sibling = (x, y, c), (x, y, 1 - c)
        chips = [(1 - x, y), (x, 1 - y), (1 - x, 1 - y)]

        def rows(px, py, pc):
            return out_ref.at[pl.ds((4 * px + 2 * py + pc) * m_per, m_per), :]

        def copy(k, block_of, to, src=None):
            return pltpu.make_async_remote_copy(
                src_ref=rows(*block_of) if src is None else src, dst_ref=rows(*block_of),
                send_sem=send_sems.at[k], recv_sem=recv_sems.at[k], device_id=to, device_id_type=MESH)

        mine = pltpu.make_async_copy(x_ref, rows(*me), local_sem)
        mine.start()
        first = [copy(0, me, sibling, src=x_ref)]
        first += [copy(1 + j, me, (*chip, c), src=x_ref) for j, chip in enumerate(chips)]
        for cp in first:
            cp.start()
        passed = [copy(4 + j, (*chip, c), sibling) for j, chip in enumerate(chips)]
        for j, chip in enumerate(chips):
            copy(1 + j, (*chip, c), me).wait_recv()
            passed[j].start()
        copy(0, sibling, me).wait_recv()
        for j, chip in enumerate(chips):
            copy(4 + j, (*chip, 1 - c), me).wait_recv()
        for cp in first + passed:
            cp.wait_send()
        mine.wait()

    return pl.pallas_call(
        body, name=name, out_shape=jax.ShapeDtypeStruct((NDEV * m_per, ncol), block.dtype),
        in_specs=[pl.BlockSpec(memory_space=pltpu.VMEM)], out_specs=pl.BlockSpec(memory_space=pltpu.VMEM),
        scratch_shapes=[pltpu.SemaphoreType.DMA((7,)), pltpu.SemaphoreType.DMA((7,)), pltpu.SemaphoreType.DMA],
        compiler_params=pltpu.CompilerParams(vmem_limit_bytes=VMEM_LIMIT),
    )(block)


def _run_exchange(name, srcs, out_structs, plan, n_copies):
    n_in, n_out = len(srcs), len(out_structs)

    def body(*refs):
        ins, outs = refs[:n_in], refs[n_in:n_in + n_out]
        send_sems, recv_sems = refs[n_in + n_out:]
        x, y, c = _coords()
        cps = []
        for k, (src, dst, peer) in enumerate(plan(x, y, c, ins, outs)):
            cp = pltpu.make_async_remote_copy(src_ref=src, dst_ref=dst, send_sem=send_sems.at[k],
                                              recv_sem=recv_sems.at[k], device_id=peer, device_id_type=MESH)
            cp.start()
            cps.append(cp)
        for cp in cps:
            cp.wait_recv()
        for cp in cps:
            cp.wait_send()

    return pl.pallas_call(
        body, name=name, in_specs=[HBM_SPEC] * n_in, out_specs=[HBM_SPEC] * n_out, out_shape=out_structs,
        scratch_shapes=[pltpu.SemaphoreType.DMA((n_copies,)), pltpu.SemaphoreType.DMA((n_copies,))],
    )(*srcs)


def _rs_stage_c(name, gs):
    def plan(x, y, c, ins, outs):
        sib = (x, y, 1 - c)
        return [(g.at[q, 1 - c], r.at[q], sib) for g, r in zip(ins, outs) for q in range(4)]

    outs = [jax.ShapeDtypeStruct((4,) + g.shape[2:], g.dtype) for g in gs]
    return _run_exchange(name, gs, outs, plan, 4 * len(gs))


def _rs_stage_ici(name, sends, first):
    def plan(x, y, c, ins, outs):
        XN, YN = (1 - x, y, c), (x, 1 - y, c)
        peers = (YN, XN) if first else (XN, YN)
        return [(s.at[h], r.at[h], peers[h]) for s, r in zip(ins, outs) for h in range(2)]

    outs = [jax.ShapeDtypeStruct(s.shape, s.dtype) for s in sends]
    return _run_exchange(name, sends, outs, plan, 2 * len(sends))


def _coord_vec():
    x, y, c = _coords()
    return jnp.stack([x, y, c]).astype(jnp.int32)


def _rs_add1(name, g, r1, coords):
    R, L = g.shape[3], g.shape[4]
    tr = _tile(R, 512, 16)

    def qk(h, idx, cr):
        return jnp.where(h == 0, 2 * idx + cr[1], 2 * cr[0] + idx)

    def qs(h, idx, cr):
        return jnp.where(h == 0, 2 * idx + 1 - cr[1], 2 * (1 - cr[0]) + idx)

    def body(cr, gk, rk, gsd, rsd, keep, send):
        keep[...] = gk[...] + rk[...]
        send[...] = (gsd[...] + rsd[...]).astype(BF16)

    gspec = lambda qf: pl.BlockSpec((None, None, None, tr, L), lambda h, idx, r, cr: (qf(h, idx, cr), cr[2], h, r, 0))
    rspec = lambda qf: pl.BlockSpec((None, None, tr, L), lambda h, idx, r, cr: (qf(h, idx, cr), h, r, 0))
    ospec = pl.BlockSpec((None, None, tr, L), lambda h, idx, r, cr: (h, idx, r, 0))
    return pl.pallas_call(
        body, name=name,
        grid_spec=pltpu.PrefetchScalarGridSpec(
            num_scalar_prefetch=1, grid=(2, 2, R // tr),
            in_specs=[gspec(qk), rspec(qk), gspec(qs), rspec(qs)], out_specs=[ospec, ospec]),
        out_shape=[jax.ShapeDtypeStruct((2, 2, R, L), F32), jax.ShapeDtypeStruct((2, 2, R, L), BF16)],
        compiler_params=_cparams("parallel", "parallel", "parallel"),
    )(coords, g, r1, g, r1)


def _rs_add2(name, keep2, recv2, coords):
    R, L = keep2.shape[2], keep2.shape[3]
    tr = _tile(R, 512, 16)

    def mine(h, cr):
        return jnp.where(h == 0, cr[0], cr[1])

    def body(cr, kk, rk, ks, rs, keep, send):
        keep[...] = kk[...] + rk[...].astype(F32)
        send[...] = (ks[...] + rs[...].astype(F32)).astype(BF16)

    sel = lambda f: pl.BlockSpec((None, None, tr, L), lambda h, r, cr: (h, f(h, cr), r, 0))
    ospec = pl.BlockSpec((None, tr, L), lambda h, r, cr: (h, r, 0))
    other = lambda h, cr: 1 - mine(h, cr)
    return pl.pallas_call(
        body, name=name,
        grid_spec=pltpu.PrefetchScalarGridSpec(
            num_scalar_prefetch=1, grid=(2, R // tr),
            in_specs=[sel(mine), sel(mine), sel(other), sel(other)], out_specs=[ospec, ospec]),
        out_shape=[jax.ShapeDtypeStruct((2, R, L), F32), jax.ShapeDtypeStruct((2, R, L), BF16)],
        compiler_params=_cparams("parallel", "parallel"),
    )(coords, keep2, recv2, keep2, recv2)


def _rs_add3(name, keep3, recv3):
    R, L = keep3.shape[1], keep3.shape[2]
    tr = _tile(R, 512, 16)

    def body(k, r, o):
        o[...] = k[...] + r[...].astype(F32)

    spec = pl.BlockSpec((None, tr, L), lambda h, r: (h, r, 0))
    return pl.pallas_call(body, name=name, grid=(2, R // tr), in_specs=[spec, spec], out_specs=spec,
                          out_shape=jax.ShapeDtypeStruct((2, R, L), F32),
                          compiler_params=_cparams("parallel", "parallel"))(keep3, recv3)


def _reduce_scatter(tag, gs, coords):
    canon = []
    for g in gs:
        per = int(np.prod(g.shape[1:]))
        L = g.shape[-1]
        canon.append(g.reshape(4, 2, 2, per // (2 * L), L))
    r1 = _rs_stage_c(f"rs_c_{tag}", canon)
    keep2, send2 = zip(*[_rs_add1(f"rs_add1_{tag}_{k}", g, r, coords) for k, (g, r) in enumerate(zip(canon, r1))])
    r2 = _rs_stage_ici(f"rs_ici1_{tag}", list(send2), True)
    keep3, send3 = zip(*[_rs_add2(f"rs_add2_{tag}_{k}", kp, r, coords) for k, (kp, r) in enumerate(zip(keep2, r2))])
    r3 = _rs_stage_ici(f"rs_ici2_{tag}", list(send3), False)
    return [_rs_add3(f"rs_add3_{tag}_{k}", kp, r) for k, (kp, r) in enumerate(zip(keep3, r3))]


def _adamw(name, w, g, m, v):
    R, L = w.shape
    tr = _tile(R, 256, 8)

    def body(w_ref, g_ref, m_ref, v_ref, d_ref, nm_ref, nv_ref):
        gv = g_ref[...]
        nm = ADAM_B1 * m_ref[...] + (1.0 - ADAM_B1) * gv
        nv = ADAM_B2 * v_ref[...] + (1.0 - ADAM_B2) * (gv * gv)
        m_hat = nm / (1.0 - ADAM_B1 ** ADAM_STEP)
        v_hat = nv / (1.0 - ADAM_B2 ** ADAM_STEP)
        d_ref[...] = -ADAM_LR * (m_hat / (jnp.sqrt(v_hat) + ADAM_EPS) + ADAM_WD * w_ref[...])
        nm_ref[...] = nm
        nv_ref[...] = nv

    spec = pl.BlockSpec((tr, L), lambda i: (i, 0))
    out = jax.ShapeDtypeStruct((R, L), F32)
    return pl.pallas_call(body, name=name, grid=(R // tr,), in_specs=[spec] * 4, out_specs=[spec] * 3,
                          out_shape=[out, out, out], compiler_params=_cparams("parallel"))(w, g, m, v)


def _sum_devices(name, gathered):
    _, R, L = gathered.shape
    tr = _tile(R, 512, 8)

    def body(g_ref, o_ref):
        acc = g_ref[0]
        for d in range(1, NDEV):
            acc = acc + g_ref[d]
        o_ref[...] = acc

    return pl.pallas_call(body, name=name, grid=(R // tr,),
                          in_specs=[pl.BlockSpec((NDEV, tr, L), lambda i: (0, i, 0))],
                          out_specs=pl.BlockSpec((tr, L), lambda i: (i, 0)),
                          out_shape=jax.ShapeDtypeStruct((R, L), F32),
                          compiler_params=_cparams("parallel"))(gathered)


def _pack(parts):
    flat, offs, pos = [], [], 0
    for p in parts:
        n = int(np.prod(p.shape))
        padded = -(-n // PACK_ALIGN) * PACK_ALIGN
        flat.append(jnp.pad(p.reshape(-1).astype(F32), (0, padded - n)))
        offs.append((pos, n, p.shape))
        pos += padded
    return jnp.concatenate(flat).reshape(-1, LANES), offs


def _unpack(packed, offs):
    flat = packed.reshape(-1)
    return [flat[pos:pos + n].reshape(shape) for pos, n, shape in offs]


def kernel(x, norm_mix, norm_ffn, rel_bias, attn_w_qkv, attn_q_gain, attn_k_gain, attn_sinks, attn_w_o, pool_w, pool_scale, ffn_w_up, ffn_conv_w, ffn_conv_b, ffn_w_down, loss_target, m_norm_mix, m_norm_ffn, m_rel_bias, m_attn_w_qkv, m_attn_q_gain, m_attn_k_gain, m_attn_sinks, m_attn_w_o, m_pool_w, m_pool_scale, m_ffn_w_up, m_ffn_conv_w, m_ffn_conv_b, m_ffn_w_down, v_norm_mix, v_norm_ffn, v_rel_bias, v_attn_w_qkv, v_attn_q_gain, v_attn_k_gain, v_attn_sinks, v_attn_w_o, v_pool_w, v_pool_scale, v_ffn_w_up, v_ffn_conv_w, v_ffn_conv_b, v_ffn_w_down):
    xs = x[0]
    target = loss_target[0]
    S, D = xs.shape
    depth = norm_mix.shape[0]
    H = D // HEAD_DIM
    n_attn, n_pool = attn_w_qkv.shape[0], pool_w.shape[0]
    QS = attn_w_qkv.shape[2]
    CB = ffn_w_up.shape[2]
    FB = ffn_w_down.shape[1]
    FF = FB * NDEV
    G, PC, C = pool_w.shape[1], pool_w.shape[2], pool_w.shape[3]
    xi, yi, ci = _coords()
    me = 4 * xi + 2 * yi + ci
    coords = _coord_vec()

    halves = lambda w: w.astype(BF16).reshape(2, -1, w.shape[-1])
    g_qkv, g_o, g_pool, g_up, g_down = _allgather_big(
        [halves(attn_w_qkv), halves(attn_w_o), halves(pool_w), halves(ffn_w_up), halves(ffn_w_down)])
    wqkv = g_qkv.reshape(NDEV, n_attn, D, QS).transpose(1, 2, 0, 3).reshape(n_attn, D, NDEV * QS)
    wo = g_o.reshape(NDEV, n_attn, D // NDEV, D).transpose(1, 0, 2, 3).reshape(n_attn, D, D)
    wp = g_pool.reshape(NDEV, n_pool, G, PC, C).transpose(1, 2, 0, 3, 4).reshape(n_pool, G, C, C)
    wup4 = g_up.reshape(NDEV, depth, D, CB)
    wd3 = g_down.reshape(NDEV, depth, FB, D).transpose(1, 0, 2, 3).reshape(depth, FF, D)

    small_in, small_in_offs = _pack([ffn_conv_w, pool_scale])
    gathered_in = _allgather_small("allgather_small_params", small_in).reshape(NDEV, -1)
    per_dev = [_unpack(gathered_in[d], small_in_offs) for d in range(NDEV)]
    conv_w_full = jnp.concatenate([p[0] for p in per_dev], axis=2)
    pool_scale_full = jnp.concatenate([p[1] for p in per_dev], axis=1)

    onehot = jnp.asarray(_t5_onehot())
    bias = _bias_band(rel_bias, onehot).reshape(H, WINDOW, 2 * WINDOW)

    saved = []
    cur = xs
    for i in range(depth):
        j = i // 2
        st = {"x0": cur}
        if i % 2 == 0:
            h = _rmsnorm_fwd(f"norm_mix_{i}", cur, norm_mix[i:i + 1])
            qkv = _mm_rows(f"qkv_{i}", h, wqkv[j], NN, F32)
            qg, kg, sk = attn_q_gain[j:j + 1], attn_k_gain[j:j + 1], attn_sinks[j].reshape(H, 1)
            o, lse = _attn_fwd(f"attn_fwd_{i}", qkv, bias, qg, kg, sk)
            x1 = _mm_rows(f"attn_out_{i}", o, wo[j], NN, F32, res=cur)
            st.update(h=h, qkv=qkv, o=o, lse=lse)
        else:
            x1, dsave = _pool_fwd(f"pool_fwd_{i}", cur, norm_mix[i:i + 1], wp[j], pool_scale_full[j:j + 1])
            st.update(dsave=dsave)
        h2 = _rmsnorm_fwd(f"norm_ffn_{i}", x1, norm_ffn[i:i + 1])
        u3, a = _ffn_up_fwd(f"ffn_up_{i}", h2, wup4, i, conv_w_full[i], ffn_conv_b[i:i + 1])
        cur = _ffn_down_fwd(f"ffn_down_{i}", a, wd3, i, x1)
        st.update(x1=x1, h2=h2, u3=u3, a=a)
        saved.append(st)

    loss_tile, dcur = _loss_fwd_bwd(cur, target)

    g_up_l, g_down_l = [None] * depth, [None] * depth
    g_qkv_l, g_o_l, g_pool_l = [None] * n_attn, [None] * n_attn, [None] * n_pool
    d_norm_mix, d_norm_ffn = [None] * depth, [None] * depth
    d_conv_w, d_conv_b = [None] * depth, [None] * depth
    d_qg, d_kg, d_sinks, d_pscale = [None] * n_attn, [None] * n_attn, [None] * n_attn, [None] * n_pool
    d_band = None
    for i in reversed(range(depth)):
        j = i // 2
        st = saved[i]
        du3, dc = _ffn_da_bwd(f"ffn_da_{i}", dcur, wd3, i, st["u3"], conv_w_full[i], ffn_conv_b[i:i + 1])
        d_conv_w[i] = jnp.concatenate([dc[0, 0:3], dc[1, 0:3]], axis=1)
        d_conv_b[i] = jnp.concatenate([dc[0, 3], dc[1, 3]], axis=0)
        dwdown = _mm_tn(f"ffn_dwdown_{i}", st["a"], dcur, tm_pref=CB if CB % LANES == 0 else 1024).reshape(NDEV, FB, D)
        dwup = _ffn_dwup_bwd(f"ffn_dwup_{i}", st["h2"], du3)
        dh2 = _ffn_dh_bwd(f"ffn_dh_{i}", du3, wup4, i)
        dx1, dg = _rmsnorm_bwd(f"norm_ffn_bwd_{i}", dh2, st["x1"], norm_ffn[i:i + 1], dcur)
        d_norm_ffn[i] = dg[0]
        grads = [dwup, dwdown]
        if i % 2 == 0:
            do = _mm_rows(f"attn_do_{i}", dx1, wo[j], NT, BF16)
            dwo = _mm_tn(f"attn_dwo_{i}", st["o"], dx1).reshape(NDEV, D // NDEV, D)
            qg, kg, sk = attn_q_gain[j:j + 1], attn_k_gain[j:j + 1], attn_sinks[j].reshape(H, 1)
            dq, dkv, db, dsk, dqg, dkg = _attn_bwd(f"attn_bwd_{i}", st["qkv"], do, st["lse"], bias, qg, kg, sk)
            d_band = db if d_band is None else d_band + db
            d_sinks[j], d_qg[j], d_kg[j] = dsk[0], dqg[0], dkg[0]
            dqkv = jnp.concatenate([dq, dkv], axis=1)
            dwqkv = _mm_tn(f"attn_dwqkv_{i}", st["h"], dqkv, tn_pref=1280)
            dwqkv = dwqkv.reshape(D, NDEV, QS).transpose(1, 0, 2)
            dh = _mm_rows(f"attn_dh_{i}", dqkv, wqkv[j], NT, F32)
            dcur, dg = _rmsnorm_bwd(f"norm_mix_bwd_{i}", dh, st["x0"], norm_mix[i:i + 1], dx1)
            grads += [dwqkv, dwo]
        else:
            dcur, dwp, dps, dg = _pool_bwd(f"pool_bwd_{i}", dx1, st["x0"], norm_mix[i:i + 1], wp[j],
                                           pool_scale_full[j:j + 1], st["dsave"])
            d_pscale[j] = dps[0]
            grads += [dwp.reshape(G, NDEV, PC, C).transpose(1, 0, 2, 3)]
        d_norm_mix[i] = dg[0]
        red = _reduce_scatter(f"l{i}", grads, coords)
        g_up_l[i] = red[0].reshape(D, CB)
        g_down_l[i] = red[1].reshape(FB, D)
        if i % 2 == 0:
            g_qkv_l[j] = red[2].reshape(D, QS)
            g_o_l[j] = red[3].reshape(D // NDEV, D)
        else:
            g_pool_l[j] = red[2].reshape(G, PC, C)

    d_rel = _bias_band_bwd(d_band.reshape(H, -1), onehot)

    small_parts = [loss_tile, jnp.stack(d_norm_mix), jnp.stack(d_norm_ffn), d_rel, jnp.stack(d_qg),
                   jnp.stack(d_kg), jnp.stack(d_sinks), jnp.stack(d_pscale), jnp.stack(d_conv_w),
                   jnp.stack(d_conv_b)]
    small, small_offs = _pack(small_parts)
    gathered = _allgather_small("allgather_small_grads", small).reshape(NDEV, -1, LANES)
    summed = _unpack(_sum_devices("sum_small_grads", gathered), small_offs)
    loss = summed[0][0, 0]
    (g_norm_mix, g_norm_ffn, g_rel, g_qg, g_kg, g_sinks, g_pscale_full, g_conv_w_full, g_conv_b) = summed[1:]
    g_pscale = lax.dynamic_slice_in_dim(g_pscale_full, me * (D // NDEV), D // NDEV, axis=1)
    g_conv_w = lax.dynamic_slice_in_dim(g_conv_w_full, me * CB, CB, axis=2)

    grads = {
        "norm_mix": g_norm_mix, "norm_ffn": g_norm_ffn, "rel_bias": g_rel, "attn_w_qkv": jnp.stack(g_qkv_l),
        "attn_q_gain": g_qg, "attn_k_gain": g_kg, "attn_sinks": g_sinks, "attn_w_o": jnp.stack(g_o_l),
        "pool_w": jnp.stack(g_pool_l), "pool_scale": g_pscale, "ffn_w_up": jnp.stack(g_up_l),
        "ffn_conv_w": g_conv_w, "ffn_conv_b": g_conv_b, "ffn_w_down": jnp.stack(g_down_l),
    }
    weights = {
        "norm_mix": (norm_mix, m_norm_mix, v_norm_mix), "norm_ffn": (norm_ffn, m_norm_ffn, v_norm_ffn),
        "rel_bias": (rel_bias, m_rel_bias, v_rel_bias), "attn_w_qkv": (attn_w_qkv, m_attn_w_qkv, v_attn_w_qkv),
        "attn_q_gain": (attn_q_gain, m_attn_q_gain, v_attn_q_gain),
        "attn_k_gain": (attn_k_gain, m_attn_k_gain, v_attn_k_gain),
        "attn_sinks": (attn_sinks, m_attn_sinks, v_attn_sinks), "attn_w_o": (attn_w_o, m_attn_w_o, v_attn_w_o),
        "pool_w": (pool_w, m_pool_w, v_pool_w), "pool_scale": (pool_scale, m_pool_scale, v_pool_scale),
        "ffn_w_up": (ffn_w_up, m_ffn_w_up, v_ffn_w_up), "ffn_conv_w": (ffn_conv_w, m_ffn_conv_w, v_ffn_conv_w),
        "ffn_conv_b": (ffn_conv_b, m_ffn_conv_b, v_ffn_conv_b), "ffn_w_down": (ffn_w_down, m_ffn_w_down, v_ffn_w_down),
    }
    names = list(weights)
    big = ("attn_w_qkv", "attn_w_o", "pool_w", "ffn_w_up", "ffn_w_down")
    upd = {}
    for nm in big:
        w, m, v = weights[nm]
        two_d = lambda t: t.reshape(-1, w.shape[-1])
        d_, m_, v_ = _adamw(f"adamw_{nm}", two_d(w), two_d(grads[nm]), two_d(m), two_d(v))
        upd[nm] = (d_.reshape(w.shape), m_.reshape(w.shape), v_.reshape(w.shape))
    small_names = [nm for nm in names if nm not in big]
    pw, offs = _pack([weights[nm][0] for nm in small_names])
    pg, _ = _pack([grads[nm] for nm in small_names])
    pm, _ = _pack([weights[nm][1] for nm in small_names])
    pv, _ = _pack([weights[nm][2] for nm in small_names])
    d_, m_, v_ = _adamw("adamw_small", pw, pg, pm, pv)
    for nm, dd, mm, vv in zip(small_names, _unpack(d_, offs), _unpack(m_, offs), _unpack(v_, offs)):
        upd[nm] = (dd, mm, vv)

    grad_x = dcur[None]
    return (loss, grad_x, *[grads[nm].reshape(weights[nm][0].shape) for nm in names],
            *[upd[nm][0] for nm in names], *[upd[nm][1] for nm in names], *[upd[nm][2] for nm in names])
```

```python
import numpy as np
import jax
import jax.numpy as jnp
from jax import lax
from jax.experimental import pallas as pl
from jax.experimental.pallas import tpu as pltpu

F32 = jnp.float32
BF16 = jnp.bfloat16
MESH = pl.DeviceIdType.MESH

NDEV = 8
HEAD_DIM = 64
GQA_GROUP = 8
WINDOW = 128
N_BUCKETS = 32
MAX_DISTANCE = 128
POOL_WINDOWS = (2, 4, 8, 16)
POOL_HALO = 32
EPS = 1e-6
NEG_INF = -1e30
ADAM_LR = 0.001
ADAM_B1 = 0.9
ADAM_B2 = 0.999
ADAM_EPS = 1e-08
ADAM_WD = 0.01
ADAM_STEP = 10

V7X_VMEM_BYTES = 64 * 1024 * 1024
VMEM_LIMIT = V7X_VMEM_BYTES - 8 * 1024 * 1024
LANES = 128
SUBLANES = 8
PACK_ALIGN = SUBLANES * LANES

NN = (((1,), (0,)), ((), ()))
NT = (((1,), (1,)), ((), ()))
TN = (((0,), (0,)), ((), ()))


def _tile(dim, pref, align):
    t = min(pref, dim)
    t -= t % align
    while t >= align:
        if dim % t == 0:
            return t
        t -= align
    return dim


def _cparams(*sem):
    return pltpu.CompilerParams(dimension_semantics=sem, vmem_limit_bytes=VMEM_LIMIT)


def _bf(v):
    return v if v.dtype == BF16 else v.astype(BF16)


def _mm(name, a, b, *, grid, a_spec, b_spec, o_spec, out_shape, contract, acc_shape, res=None):
    nk = grid[2]

    def body(*refs):
        if res is not None:
            a_ref, b_ref, r_ref, o_ref = refs[:4]
            scr = refs[4:]
        else:
            a_ref, b_ref, o_ref = refs[:3]
            r_ref = None
            scr = refs[3:]
        part = lax.dot_general(_bf(a_ref[...]), _bf(b_ref[...]), contract, preferred_element_type=F32)

        def finish(acc):
            if r_ref is not None:
                acc = acc + r_ref[...]
            o_ref[...] = acc.astype(o_ref.dtype)

        if nk == 1:
            finish(part)
        else:
            acc_ref = scr[0]
            k = pl.program_id(2)

            @pl.when(k == 0)
            def _():
                acc_ref[...] = part

            @pl.when(k > 0)
            def _():
                acc_ref[...] += part

            @pl.when(k == nk - 1)
            def _():
                finish(acc_ref[...])

    in_specs = [a_spec, b_spec] + ([o_spec] if res is not None else [])
    args = (a, b) + ((res,) if res is not None else ())
    return pl.pallas_call(
        body, name=name, grid=grid, in_specs=in_specs, out_specs=o_spec, out_shape=out_shape,
        scratch_shapes=[pltpu.VMEM(acc_shape, F32)] if nk > 1 else [],
        compiler_params=_cparams("parallel", "parallel", "arbitrary"),
    )(*args)


def _mm_rows(name, a, b, contract, out_dtype, res=None, tm_pref=512):
    S, K = a.shape
    N = b.shape[1] if contract == NN else b.shape[0]
    tm = _tile(S, tm_pref, 16)
    return _mm(name, a, b, grid=(1, S // tm, 1),
               a_spec=pl.BlockSpec((tm, K), lambda p, q, k: (q, 0)),
               b_spec=pl.BlockSpec(b.shape, lambda p, q, k: (0, 0)),
               o_spec=pl.BlockSpec((tm, N), lambda p, q, k: (q, 0)),
               out_shape=jax.ShapeDtypeStruct((S, N), out_dtype), contract=contract,
               acc_shape=(tm, N), res=res)


def _mm_tn(name, a, b, tm_pref=1024, tn_pref=1024, tk_pref=2048):
    S, M = a.shape
    N = b.shape[1]
    tm, tn, tk = _tile(M, tm_pref, LANES), _tile(N, tn_pref, LANES), _tile(S, tk_pref, 16)
    return _mm(name, a, b, grid=(M // tm, N // tn, S // tk),
               a_spec=pl.BlockSpec((tk, tm), lambda p, q, k: (k, p)),
               b_spec=pl.BlockSpec((tk, tn), lambda p, q, k: (k, q)),
               o_spec=pl.BlockSpec((tm, tn), lambda p, q, k: (p, q)),
               out_shape=jax.ShapeDtypeStruct((M, N), F32), contract=TN, acc_shape=(tm, tn))


def _rmsnorm_fwd(name, x, gain):
    S, D = x.shape
    tm = _tile(S, 512, 16)

    def body(x_ref, g_ref, o_ref):
        xf = x_ref[...]
        r = lax.rsqrt(jnp.mean(xf * xf, axis=-1, keepdims=True) + EPS)
        o_ref[...] = (xf * r * g_ref[...]).astype(o_ref.dtype)

    return pl.pallas_call(
        body, name=name, grid=(S // tm,),
        in_specs=[pl.BlockSpec((tm, D), lambda i: (i, 0)), pl.BlockSpec((1, D), lambda i: (0, 0))],
        out_specs=pl.BlockSpec((tm, D), lambda i: (i, 0)),
        out_shape=jax.ShapeDtypeStruct((S, D), BF16), compiler_params=_cparams("parallel"),
    )(x, gain)


def _rms_bwd_math(dh, xf, gain):
    r = lax.rsqrt(jnp.mean(xf * xf, axis=-1, keepdims=True) + EPS)
    xhat = xf * r
    dxh = dh * gain
    dx = r * (dxh - xhat * jnp.mean(dxh * xhat, axis=-1, keepdims=True))
    return dx, jnp.sum(dh * xhat, axis=0, keepdims=True)


def _rmsnorm_bwd(name, dh, x, gain, dres):
    S, D = x.shape
    tm = _tile(S, 256, 16)

    def body(dh_ref, x_ref, g_ref, dr_ref, dx_ref, dxb_ref, dg_ref):
        dx, dg = _rms_bwd_math(dh_ref[...], x_ref[...], g_ref[...])
        dx = dr_ref[...] + dx
        dx_ref[...] = dx
        dxb_ref[...] = dx.astype(BF16)

        @pl.when(pl.program_id(0) == 0)
        def _():
            dg_ref[...] = dg

        @pl.when(pl.program_id(0) > 0)
        def _():
            dg_ref[...] += dg

    row = pl.BlockSpec((tm, D), lambda i: (i, 0))
    vec = pl.BlockSpec((1, D), lambda i: (0, 0))
    return pl.pallas_call(
        body, name=name, grid=(S // tm,), in_specs=[row, row, vec, row], out_specs=[row, row, vec],
        out_shape=[jax.ShapeDtypeStruct((S, D), F32), jax.ShapeDtypeStruct((S, D), BF16),
                   jax.ShapeDtypeStruct((1, D), F32)],
        compiler_params=_cparams("arbitrary"),
    )(dh, x, gain, dres)


def _loss_fwd_bwd(y, target):
    S, D = y.shape
    tm = _tile(S, 512, 16)

    def body(y_ref, t_ref, l_ref, dy_ref, dyb_ref):
        e = y_ref[...] - t_ref[...]
        dy = e * (1.0 / D)
        dy_ref[...] = dy
        dyb_ref[...] = dy.astype(BF16)
        part = 0.5 * jnp.sum(jnp.mean(e * e, axis=-1, keepdims=True), axis=0, keepdims=True)
        part = jnp.broadcast_to(part, (SUBLANES, LANES))

        @pl.when(pl.program_id(0) == 0)
        def _():
            l_ref[...] = part

        @pl.when(pl.program_id(0) > 0)
        def _():
            l_ref[...] += part

    row = pl.BlockSpec((tm, D), lambda i: (i, 0))
    return pl.pallas_call(
        body, name="loss", grid=(S // tm,), in_specs=[row, row],
        out_specs=[pl.BlockSpec((SUBLANES, LANES), lambda i: (0, 0)), row, row],
        out_shape=[jax.ShapeDtypeStruct((SUBLANES, LANES), F32), jax.ShapeDtypeStruct((S, D), F32),
                   jax.ShapeDtypeStruct((S, D), BF16)],
        compiler_params=_cparams("arbitrary"),
    )(y, target)


def _sigmoid(v):
    return 1.0 / (1.0 + jnp.exp(-v))


def _ffn_up_fwd(name, h2, wup3, layer, cw, cb):
    S, D = h2.shape
    FF = wup3.shape[2] // 2
    CB = _tile(FF, 1408, LANES)
    NJ = FF // CB
    tm = _tile(S, 512, 16)
    nI = S // tm

    def body(h_ref, wg_ref, wv_ref, cwg_ref, cwv_ref, cbg_ref, cbv_ref, u_ref, uc_ref, a_ref, ext_g, ext_v):
        @pl.when(pl.program_id(1) == 0)
        def _():
            ext_g[pl.ds(0, 8), :] = jnp.zeros((8, CB), F32)
            ext_v[pl.ds(0, 8), :] = jnp.zeros((8, CB), F32)

        h = h_ref[...]

        def conv(w_ref, cw_ref, cb_ref, ext, slot):
            u = jnp.dot(h, w_ref[...], preferred_element_type=F32)
            u_ref[slot] = u.astype(BF16)
            ext[pl.ds(8, tm), :] = u
            uc = (cw_ref[0:1, :] * ext[pl.ds(6, tm), :] + cw_ref[1:2, :] * ext[pl.ds(7, tm), :]
                  + cw_ref[2:3, :] * u + cb_ref[...])
            ext[pl.ds(0, 8), :] = ext[pl.ds(tm, 8), :]
            uc_ref[slot] = uc.astype(BF16)
            return uc

        gc = conv(wg_ref, cwg_ref, cbg_ref, ext_g, 0)
        vc = conv(wv_ref, cwv_ref, cbv_ref, ext_v, 1)
        a_ref[...] = (gc * _sigmoid(gc) * vc).astype(BF16)

    def wspec(off):
        return pl.BlockSpec((None, D, CB), lambda j, i: (layer, 0, j + off))

    def cspec(rows, off):
        return pl.BlockSpec((rows, CB), lambda j, i: (0, j + off))

    pair = pl.BlockSpec((2, tm, CB), lambda j, i: (0, i, j))
    return pl.pallas_call(
        body, name=name, grid=(NJ, nI),
        in_specs=[pl.BlockSpec((tm, D), lambda j, i: (i, 0)), wspec(0), wspec(NJ),
                  cspec(3, 0), cspec(3, NJ), cspec(1, 0), cspec(1, NJ)],
        out_specs=[pair, pair, pl.BlockSpec((tm, CB), lambda j, i: (i, j))],
        out_shape=[jax.ShapeDtypeStruct((2, S, FF), BF16), jax.ShapeDtypeStruct((2, S, FF), BF16),
                   jax.ShapeDtypeStruct((S, FF), BF16)],
        scratch_shapes=[pltpu.VMEM((tm + 8, CB), F32), pltpu.VMEM((tm + 8, CB), F32)],
        compiler_params=_cparams("parallel", "arbitrary"),
    )(h2, wup3, wup3, cw, cw, cb, cb)


def _ffn_da_bwd(name, dyb, wd3, layer, u3, uc3, cw):
    S, D = dyb.shape
    FF = wd3.shape[1]
    CB = _tile(FF, 1408, LANES)
    NJ = FF // CB
    tm = _tile(S, 256, 16)
    nI = S // tm

    def body(dy_ref, wd_ref, u_ref, uc_ref, cwg_ref, cwv_ref, du_ref, dc_ref, ext_dg, ext_dv):
        i = pl.program_id(1)

        @pl.when(i == 0)
        def _():
            ext_dg[pl.ds(tm, 8), :] = jnp.zeros((8, CB), F32)
            ext_dv[pl.ds(tm, 8), :] = jnp.zeros((8, CB), F32)
            dc_ref[...] = jnp.zeros(dc_ref.shape, F32)

        da = lax.dot_general(dy_ref[...], wd_ref[...], NT, preferred_element_type=F32)
        gc = uc_ref[0].astype(F32)
        vc = uc_ref[1].astype(F32)
        sig = _sigmoid(gc)
        d_vc = da * (gc * sig)
        d_gc = da * vc * (sig * (1.0 + gc * (1.0 - sig)))

        def back(slot, d_uc, ext, cw_ref):
            u = u_ref[slot].astype(F32)
            ext[pl.ds(0, tm), :] = d_uc
            dp1 = ext[pl.ds(1, tm), :]
            dp2 = ext[pl.ds(2, tm), :]
            du = cw_ref[2:3, :] * d_uc + cw_ref[1:2, :] * dp1 + cw_ref[0:1, :] * dp2
            ext[pl.ds(tm, 8), :] = d_uc[0:8, :]
            du_ref[slot] = du.astype(BF16)
            dc_ref[slot, 0:1, :] += jnp.sum(dp2 * u, axis=0, keepdims=True)
            dc_ref[slot, 1:2, :] += jnp.sum(dp1 * u, axis=0, keepdims=True)
            dc_ref[slot, 2:3, :] += jnp.sum(d_uc * u, axis=0, keepdims=True)
            dc_ref[slot, 3:4, :] += jnp.sum(d_uc, axis=0, keepdims=True)

        back(0, d_gc, ext_dg, cwg_ref)
        back(1, d_vc, ext_dv, cwv_ref)

    def rev(i):
        return nI - 1 - i

    pair = pl.BlockSpec((2, tm, CB), lambda j, i: (0, rev(i), j))
    return pl.pallas_call(
        body, name=name, grid=(NJ, nI),
        in_specs=[pl.BlockSpec((tm, D), lambda j, i: (rev(i), 0)),
                  pl.BlockSpec((None, CB, D), lambda j, i: (layer, j, 0)), pair, pair,
                  pl.BlockSpec((3, CB), lambda j, i: (0, j)), pl.BlockSpec((3, CB), lambda j, i: (0, j + NJ))],
        out_specs=[pair, pl.BlockSpec((2, 8, CB), lambda j, i: (0, 0, j))],
        out_shape=[jax.ShapeDtypeStruct((2, S, FF), BF16), jax.ShapeDtypeStruct((2, 8, FF), F32)],
        scratch_shapes=[pltpu.VMEM((tm + 8, CB), F32) for _ in range(2)],
        compiler_params=_cparams("parallel", "arbitrary"),
    )(dyb, wd3, u3, uc3, cw, cw)


def _ffn_down_fwd(name, a, wd3, layer, res):
    S, FF = a.shape
    D = wd3.shape[2]
    tm = _tile(S, 256, 16)
    tk = _tile(FF, 2816, LANES)
    return _mm(name, a, wd3, grid=(1, S // tm, FF // tk),
               a_spec=pl.BlockSpec((tm, tk), lambda p, q, k: (q, k)),
               b_spec=pl.BlockSpec((None, tk, D), lambda p, q, k: (layer, k, 0)),
               o_spec=pl.BlockSpec((tm, D), lambda p, q, k: (q, 0)),
               out_shape=jax.ShapeDtypeStruct((S, D), F32), contract=NN, acc_shape=(tm, D), res=res)


def _ffn_dh_bwd(name, du3, wup3, layer):
    _, S, FF = du3.shape
    D = wup3.shape[1]
    tm = _tile(S, 512, 16)
    tk = _tile(FF, 2816, LANES)
    nh = FF // tk
    return _mm(name, du3, wup3, grid=(1, S // tm, 2 * nh),
               a_spec=pl.BlockSpec((None, tm, tk), lambda p, q, k: (k // nh, q, k % nh)),
               b_spec=pl.BlockSpec((None, D, tk), lambda p, q, k: (layer, 0, k)),
               o_spec=pl.BlockSpec((tm, D), lambda p, q, k: (q, 0)),
               out_shape=jax.ShapeDtypeStruct((S, D), F32), contract=NT, acc_shape=(tm, D))


def _ffn_dwup_bwd(name, h2, du3):
    S, D = h2.shape
    FF = du3.shape[2]
    NJ = NDEV // 2
    CB = FF // NJ
    tm, tk = _tile(D, 1024, LANES), _tile(S, 2048, 16)
    return _mm(name, h2, du3, grid=(NDEV, D // tm, S // tk),
               a_spec=pl.BlockSpec((tk, tm), lambda p, q, k: (k, q)),
               b_spec=pl.BlockSpec((None, tk, CB), lambda p, q, k: (p // NJ, k, p % NJ)),
               o_spec=pl.BlockSpec((None, tm, CB), lambda p, q, k: (p, q, 0)),
               out_shape=jax.ShapeDtypeStruct((NDEV, D, CB), F32), contract=TN, acc_shape=(tm, CB))


def _t5_onehot():
    i = np.arange(WINDOW)[:, None]
    j = np.arange(2 * WINDOW)[None, :]
    n = np.maximum(WINDOW + i - j, 0)
    max_exact = N_BUCKETS // 2
    nf = np.maximum(n, 1).astype(np.float32)
    large = max_exact + (np.log(nf / max_exact) / np.log(MAX_DISTANCE / max_exact)
                         * (N_BUCKETS - max_exact)).astype(np.int32)
    large = np.minimum(large, N_BUCKETS - 1)
    bucket = np.where(n < max_exact, n, large).astype(np.int32).reshape(-1)
    return (np.arange(N_BUCKETS)[:, None] == bucket[None, :]).astype(np.float32)


def _bias_band(rel_bias, onehot):
    H = rel_bias.shape[0]
    n = onehot.shape[1]

    def body(r_ref, oh_ref, o_ref):
        o_ref[...] = jnp.dot(r_ref[...], oh_ref[...], preferred_element_type=F32,
                             precision=lax.Precision.HIGHEST)

    return pl.pallas_call(body, name="bias_band", out_shape=jax.ShapeDtypeStruct((H, n), F32),
                          compiler_params=pltpu.CompilerParams(vmem_limit_bytes=VMEM_LIMIT))(rel_bias, onehot)


def _bias_band_bwd(dband, onehot):
    H = dband.shape[0]

    def body(d_ref, oh_ref, o_ref):
        o_ref[...] = lax.dot_general(d_ref[...], oh_ref[...], NT, preferred_element_type=F32,
                                     precision=lax.Precision.HIGHEST)

    return pl.pallas_call(body, name="bias_band_bwd", out_shape=jax.ShapeDtypeStruct((H, N_BUCKETS), F32),
                          compiler_params=pltpu.CompilerParams(vmem_limit_bytes=VMEM_LIMIT))(dband, onehot)


def _band_valid(n):
    i = lax.broadcasted_iota(jnp.int32, (WINDOW, 2 * WINDOW), 0)
    j = lax.broadcasted_iota(jnp.int32, (WINDOW, 2 * WINDOW), 1)
    return (j > i) & (j <= i + WINDOW) & ((n > 0) | (j >= WINDOW))


def _head_norm(v, gain):
    r = lax.rsqrt(jnp.mean(v * v, axis=-1, keepdims=True) + EPS)
    vhat = v * r
    return r, vhat, vhat * gain


def _stack_heads(t, kh):
    return jnp.concatenate([t[:, (kh * GQA_GROUP + g) * HEAD_DIM:(kh * GQA_GROUP + g + 1) * HEAD_DIM]
                            for g in range(GQA_GROUP)], axis=0)


def _unstack_heads(t8):
    return jnp.concatenate([t8[g * WINDOW:(g + 1) * WINDOW, :] for g in range(GQA_GROUP)], axis=1)


def _band_scores(qn, kn, b_ref, kh, valid, scale):
    s = lax.dot_general(qn, kn, NT, preferred_element_type=F32) * scale
    s = s.reshape(GQA_GROUP, WINDOW, 2 * WINDOW) + b_ref[kh * GQA_GROUP:(kh + 1) * GQA_GROUP]
    s = jnp.where(valid[None], s, NEG_INF)
    return s.reshape(GQA_GROUP * WINDOW, 2 * WINDOW)


def _sink_rows(s_ref, kh):
    return jnp.concatenate([jnp.broadcast_to(s_ref[kh * GQA_GROUP + g:kh * GQA_GROUP + g + 1, :], (WINDOW, 1))
                            for g in range(GQA_GROUP)], axis=0)


def _attn_fwd(name, qkv, bias, qg, kg, sinks):
    S, QW = qkv.shape
    H = bias.shape[0]
    D = H * HEAD_DIM
    KV = H // GQA_GROUP
    kvw = QW - D
    kvb = D // kvw
    nb = S // WINDOW
    scale = HEAD_DIM ** -0.5

    def body(q_ref, kc_ref, kp_ref, b_ref, qg_ref, kg_ref, s_ref, o_ref, l_ref):
        n = pl.program_id(0)
        valid = _band_valid(n)
        q = q_ref[...]
        kvc = kc_ref[...]
        kvp = kp_ref[...]
        lane = lax.broadcasted_iota(jnp.int32, (WINDOW, H), 1)
        lse_all = jnp.zeros((WINDOW, H), F32)
        outs = []
        for kh in range(KV):
            ks = slice(kh * HEAD_DIM, (kh + 1) * HEAD_DIM)
            vs = slice((KV + kh) * HEAD_DIM, (KV + kh + 1) * HEAD_DIM)
            kb = jnp.concatenate([kvp[:, ks], kvc[:, ks]], axis=0)
            vb = jnp.concatenate([kvp[:, vs], kvc[:, vs]], axis=0).astype(BF16)
            kn = _head_norm(kb, kg_ref[...])[2].astype(BF16)
            qn = _head_norm(_stack_heads(q, kh), qg_ref[...])[2].astype(BF16)
            s = _band_scores(qn, kn, b_ref, kh, valid, scale)
            sink = _sink_rows(s_ref, kh)
            m = jnp.maximum(jnp.max(s, axis=-1, keepdims=True), sink)
            p = jnp.exp(s - m)
            den = jnp.sum(p, axis=-1, keepdims=True) + jnp.exp(sink - m)
            p = p * (1.0 / den)
            outs.append(_unstack_heads(jnp.dot(p.astype(BF16), vb, preferred_element_type=F32)))
            lse = m + jnp.log(den)
            for g in range(GQA_GROUP):
                lse_all = jnp.where(lane == kh * GQA_GROUP + g, lse[g * WINDOW:(g + 1) * WINDOW, :], lse_all)
        o_ref[...] = jnp.concatenate(outs, axis=1).astype(BF16)
        l_ref[...] = lse_all

    const2 = lambda shape: pl.BlockSpec(shape, lambda n: (0, 0))
    return pl.pallas_call(
        body, name=name, grid=(nb,),
        in_specs=[pl.BlockSpec((WINDOW, D), lambda n: (n, 0)),
                  pl.BlockSpec((WINDOW, kvw), lambda n: (n, kvb)),
                  pl.BlockSpec((WINDOW, kvw), lambda n: (jnp.maximum(n - 1, 0), kvb)),
                  pl.BlockSpec(bias.shape, lambda n: (0, 0, 0)),
                  const2((1, HEAD_DIM)), const2((1, HEAD_DIM)), const2((H, 1))],
        out_specs=[pl.BlockSpec((WINDOW, D), lambda n: (n, 0)), pl.BlockSpec((WINDOW, H), lambda n: (n, 0))],
        out_shape=[jax.ShapeDtypeStruct((S, D), BF16), jax.ShapeDtypeStruct((S, H), F32)],
        compiler_params=_cparams("parallel"),
    )(qkv, qkv, qkv, bias, qg, kg, sinks)


def _attn_bwd(name, qkv, do, lse, bias, qg, kg, sinks):
    S, QW = qkv.shape
    H = bias.shape[0]
    D = H * HEAD_DIM
    KV = H // GQA_GROUP
    kvw = QW - D
    kvb = D // kvw
    nb = S // WINDOW
    scale = HEAD_DIM ** -0.5

    def body(q_ref, kc_ref, kp_ref, do_ref, l_ref, b_ref, qg_ref, kg_ref, s_ref,
             dq_ref, dkv_ref, db_ref, ds_ref, dqg_ref, dkg_ref, carry):
        n = pl.program_id(0)

        @pl.when(n == 0)
        def _():
            db_ref[...] = jnp.zeros(db_ref.shape, F32)
            ds_ref[...] = jnp.zeros(ds_ref.shape, F32)
            dqg_ref[...] = jnp.zeros(dqg_ref.shape, F32)
            dkg_ref[...] = jnp.zeros(dkg_ref.shape, F32)
            carry[...] = jnp.zeros(carry.shape, F32)

        @pl.when(n == nb)
        def _():
            dkv_ref[...] = carry[...].astype(BF16)

        @pl.when(n < nb)
        def _():
            valid = _band_valid(n)
            q = q_ref[...]
            kvc = kc_ref[...]
            kvp = kp_ref[...]
            do_all = do_ref[...]
            lse = l_ref[...]
            qgain = qg_ref[...]
            kgain = kg_ref[...]
            dqs, dk_parts, dv_parts = [], [], []
            dqg = jnp.zeros((1, HEAD_DIM), F32)
            dkg = jnp.zeros((1, HEAD_DIM), F32)
            for kh in range(KV):
                ks = slice(kh * HEAD_DIM, (kh + 1) * HEAD_DIM)
                vs = slice((KV + kh) * HEAD_DIM, (KV + kh + 1) * HEAD_DIM)
                heads = slice(kh * GQA_GROUP, (kh + 1) * GQA_GROUP)
                kb = jnp.concatenate([kvp[:, ks], kvc[:, ks]], axis=0)
                vb = jnp.concatenate([kvp[:, vs], kvc[:, vs]], axis=0).astype(BF16)
                rk, khat, kn32 = _head_norm(kb, kgain)
                kn = kn32.astype(BF16)
                rq, qhat, qn32 = _head_norm(_stack_heads(q, kh), qgain)
                qn = qn32.astype(BF16)
                s = _band_scores(qn, kn, b_ref, kh, valid, scale)
                lse8 = jnp.concatenate([lse[:, kh * GQA_GROUP + g:kh * GQA_GROUP + g + 1]
                                        for g in range(GQA_GROUP)], axis=0)
                p = jnp.exp(s - lse8)
                do8 = _stack_heads(do_all, kh)
                dp = lax.dot_general(do8, vb, NT, preferred_element_type=F32)
                delta = jnp.sum(p * dp, axis=-1, keepdims=True)
                ds = p * (dp - delta)
                db_ref[heads] += ds.reshape(GQA_GROUP, WINDOW, 2 * WINDOW)
                psink = jnp.exp(_sink_rows(s_ref, kh) - lse8)
                ds_ref[heads, :] += -jnp.sum((psink * delta).reshape(GQA_GROUP, WINDOW, 1), axis=1)
                dsb = (ds * scale).astype(BF16)
                dqn = jnp.dot(dsb, kn, preferred_element_type=F32)
                dkn = lax.dot_general(dsb, qn, TN, preferred_element_type=F32)
                dv_parts.append(lax.dot_general(p.astype(BF16), do8, TN, preferred_element_type=F32))
                dqg = dqg + jnp.sum(dqn * qhat, axis=0, keepdims=True)
                dqh = dqn * qgain
                dqs.append(_unstack_heads(rq * (dqh - qhat * jnp.mean(dqh * qhat, axis=-1, keepdims=True))))
                dkg = dkg + jnp.sum(dkn * khat, axis=0, keepdims=True)
                dkh = dkn * kgain
                dk_parts.append(rk * (dkh - khat * jnp.mean(dkh * khat, axis=-1, keepdims=True)))
            dq_ref[...] = jnp.concatenate(dqs, axis=1).astype(BF16)
            dqg_ref[...] += dqg
            dkg_ref[...] += dkg
            dkv = jnp.concatenate(dk_parts + dv_parts, axis=1)
            dkv_ref[...] = (carry[...] + dkv[0:WINDOW, :]).astype(BF16)
            carry[...] = dkv[WINDOW:2 * WINDOW, :]

    cur = lambda n: jnp.minimum(n, nb - 1)
    const2 = lambda shape: pl.BlockSpec(shape, lambda n: (0, 0))
    return pl.pallas_call(
        body, name=name, grid=(nb + 1,),
        in_specs=[pl.BlockSpec((WINDOW, D), lambda n: (cur(n), 0)),
                  pl.BlockSpec((WINDOW, kvw), lambda n: (cur(n), kvb)),
                  pl.BlockSpec((WINDOW, kvw), lambda n: (jnp.maximum(cur(n) - 1, 0), kvb)),
                  pl.BlockSpec((WINDOW, D), lambda n: (cur(n), 0)),
                  pl.BlockSpec((WINDOW, H), lambda n: (cur(n), 0)),
                  pl.BlockSpec(bias.shape, lambda n: (0, 0, 0)),
                  const2((1, HEAD_DIM)), const2((1, HEAD_DIM)), const2((H, 1))],
        out_specs=[pl.BlockSpec((WINDOW, D), lambda n: (cur(n), 0)),
                   pl.BlockSpec((WINDOW, kvw), lambda n: (jnp.maximum(n - 1, 0), 0)),
                   pl.BlockSpec(bias.shape, lambda n: (0, 0, 0)),
                   const2((H, 1)), const2((1, HEAD_DIM)), const2((1, HEAD_DIM))],
        out_shape=[jax.ShapeDtypeStruct((S, D), BF16), jax.ShapeDtypeStruct((S, kvw), BF16),
                   jax.ShapeDtypeStruct(bias.shape, F32), jax.ShapeDtypeStruct((H, 1), F32),
                   jax.ShapeDtypeStruct((1, HEAD_DIM), F32), jax.ShapeDtypeStruct((1, HEAD_DIM), F32)],
        scratch_shapes=[pltpu.VMEM((WINDOW, kvw), F32)],
        compiler_params=_cparams("arbitrary"),
    )(qkv, qkv, qkv, do, lse, bias, qg, kg, sinks)


def _window_sums(src, bufs, lo, n_rows, step_sign, col_groups):
    out = []
    for g, cols in enumerate(col_groups):
        prev = src
        for level in range(g + 1):
            k = step_sign * (1 << level)
            cur = bufs[level]
            cur[pl.ds(lo, n_rows), cols] = prev[pl.ds(lo, n_rows), cols] + prev[pl.ds(lo + k, n_rows), cols]
            prev = cur
        out.append(prev)
    return out


def _pool_fwd(name, x, gain, wp, scale):
    S, D = x.shape
    G, C = wp.shape[0], wp.shape[1]
    tm = _tile(S, 256, POOL_HALO)
    hb = tm // POOL_HALO
    HL = POOL_HALO
    groups = [slice(g * C, (g + 1) * C) for g in range(G)]

    def body(x_ref, xh_ref, g_ref, w_ref, sc_ref, o_ref, d_ref, ext, p2, p4, p8, p16):
        i = pl.program_id(0)
        gain_v = g_ref[...]
        xt = x_ref[...]
        h = _head_norm(xt, gain_v)[2]
        hh = _head_norm(xh_ref[...], gain_v)[2]
        ext[pl.ds(0, HL), :] = jnp.where(i > 0, hh, 0.0)
        ext[pl.ds(HL, tm), :] = h
        bufs = (p2, p4, p8, p16)
        for b in bufs:
            b[pl.ds(0, 8), :] = jnp.zeros((8, D), F32)
        sums = _window_sums(ext, bufs, 8, tm + HL - 8, -1, groups)
        t = i * tm + lax.broadcasted_iota(jnp.int32, (tm, 1), 0)
        for g, cols in enumerate(groups):
            cnt = jnp.minimum(t + 1, POOL_WINDOWS[g]).astype(F32)
            d = sums[g][pl.ds(HL, tm), cols] / cnt - h[:, cols]
            db = d.astype(BF16)
            d_ref[:, cols] = db
            y = jnp.dot(db, w_ref[g], preferred_element_type=F32)
            o_ref[:, cols] = xt[:, cols] + y * sc_ref[:, cols]

    row = pl.BlockSpec((tm, D), lambda i: (i, 0))
    vec = pl.BlockSpec((1, D), lambda i: (0, 0))
    return pl.pallas_call(
        body, name=name, grid=(S // tm,),
        in_specs=[row, pl.BlockSpec((HL, D), lambda i: (jnp.maximum(i * hb - 1, 0), 0)), vec,
                  pl.BlockSpec(wp.shape, lambda i: (0, 0, 0)), vec],
        out_specs=[row, row],
        out_shape=[jax.ShapeDtypeStruct((S, D), F32), jax.ShapeDtypeStruct((S, D), BF16)],
        scratch_shapes=[pltpu.VMEM((tm + HL, D), F32) for _ in range(5)],
        compiler_params=_cparams("parallel"),
    )(x, x, gain, wp, scale)


def _pool_bwd(name, dx1, x, gain, wp, scale, dsave):
    S, D = x.shape
    G, C = wp.shape[0], wp.shape[1]
    tm = _tile(S, 256, POOL_HALO)
    hb = tm // POOL_HALO
    HL = POOL_HALO
    nI = S // tm
    groups = [slice(g * C, (g + 1) * C) for g in range(G)]

    def body(dx_ref, dxh_ref, x_ref, g_ref, w_ref, sc_ref, ds_ref, o_ref, ob_ref, dw_ref, dsc_ref, dg_ref,
             ext, p2, p4, p8, p16):
        i = pl.program_id(0)

        @pl.when(i == 0)
        def _():
            dw_ref[...] = jnp.zeros(dw_ref.shape, F32)
            dsc_ref[...] = jnp.zeros(dsc_ref.shape, F32)
            dg_ref[...] = jnp.zeros(dg_ref.shape, F32)

        dx1t = dx_ref[...]
        sc = sc_ref[...]
        dys = (dx1t * sc).astype(BF16)
        dys_h = (dxh_ref[...] * sc).astype(BF16)
        t = i * tm + lax.broadcasted_iota(jnp.int32, (tm, 1), 0)
        th = (i + 1) * tm + lax.broadcasted_iota(jnp.int32, (HL, 1), 0)
        dds = []
        for g, cols in enumerate(groups):
            dsv = ds_ref[:, cols]
            y = jnp.dot(dsv, w_ref[g], preferred_element_type=F32)
            dsc_ref[:, cols] += jnp.sum(dx1t[:, cols] * y, axis=0, keepdims=True)
            dw_ref[g] += lax.dot_general(dsv, dys[:, cols], TN, preferred_element_type=F32)
            dd = lax.dot_general(dys[:, cols], w_ref[g], NT, preferred_element_type=F32)
            dd_h = lax.dot_general(dys_h[:, cols], w_ref[g], NT, preferred_element_type=F32)
            dds.append(dd)
            w = POOL_WINDOWS[g]
            ext[pl.ds(0, tm), cols] = dd / jnp.minimum(t + 1, w).astype(F32)
            e_h = dd_h / jnp.minimum(th + 1, w).astype(F32)
            ext[pl.ds(tm, HL), cols] = jnp.where(i < nI - 1, e_h, 0.0)
        bufs = (p2, p4, p8, p16)
        for b in bufs:
            b[pl.ds(tm + HL - 8, 8), :] = jnp.zeros((8, D), F32)
        sums = _window_sums(ext, bufs, 0, tm + HL - 8, 1, groups)
        dh = jnp.concatenate([sums[g][pl.ds(0, tm), cols] - dds[g] for g, cols in enumerate(groups)], axis=1)
        dx, dg = _rms_bwd_math(dh, x_ref[...], g_ref[...])
        dx = dx1t + dx
        o_ref[...] = dx
        ob_ref[...] = dx.astype(BF16)
        dg_ref[...] += dg

    row = pl.BlockSpec((tm, D), lambda i: (i, 0))
    vec = pl.BlockSpec((1, D), lambda i: (0, 0))
    last_h = S // HL - 1
    return pl.pallas_call(
        body, name=name, grid=(nI,),
        in_specs=[row, pl.BlockSpec((HL, D), lambda i: (jnp.minimum((i + 1) * hb, last_h), 0)), row, vec,
                  pl.BlockSpec(wp.shape, lambda i: (0, 0, 0)), vec, row],
        out_specs=[row, row, pl.BlockSpec(wp.shape, lambda i: (0, 0, 0)), vec, vec],
        out_shape=[jax.ShapeDtypeStruct((S, D), F32), jax.ShapeDtypeStruct((S, D), BF16),
                   jax.ShapeDtypeStruct(wp.shape, F32),
                   jax.ShapeDtypeStruct((1, D), F32), jax.ShapeDtypeStruct((1, D), F32)],
        scratch_shapes=[pltpu.VMEM((tm + HL, D), F32) for _ in range(5)],
        compiler_params=_cparams("arbitrary"),
    )(dx1, dx1, x, gain, wp, scale, dsave)


HBM_SPEC = pl.BlockSpec(memory_space=pltpu.HBM)


def _coords():
    return lax.axis_index("x"), lax.axis_index("y"), lax.axis_index("c")


def _allgather_big(shards):
    n = len(shards)

    def body(*refs):
        ins, outs = refs[:n], refs[n:2 * n]
        send_sems, recv_sems, local_sems = refs[2 * n:]
        x, y, c = _coords()
        me, xn, yn, dg = 4 * x + 2 * y + c, 4 * (1 - x) + 2 * y + c, 4 * x + 2 * (1 - y) + c, 4 * (1 - x) + 2 * (1 - y) + c
        XN, YN, SB = (1 - x, y, c), (x, 1 - y, c), (x, y, 1 - c)
        flip_c = lambda blk: blk + 1 - 2 * c

        def copy(o, k, src, dst, to):
            return pltpu.make_async_remote_copy(src_ref=src, dst_ref=dst, send_sem=send_sems.at[o * 8 + k],
                                                recv_sem=recv_sems.at[o * 8 + k], device_id=to, device_id_type=MESH)

        sends = []

        def start(cp):
            cp.start()
            sends.append(cp)

        locals_ = []
        for o in range(n):
            lc = pltpu.make_async_copy(ins[o], outs[o].at[me], local_sems.at[o])
            lc.start()
            locals_.append(lc)
            start(copy(o, 0, ins[o], outs[o].at[me], XN))
            start(copy(o, 1, ins[o], outs[o].at[me], YN))
            start(copy(o, 4, ins[o], outs[o].at[me], SB))
        for o in range(n):
            out = outs[o]
            copy(o, 0, ins[o], out.at[xn], XN).wait_recv()
            start(copy(o, 2, out.at[xn, 0], out.at[xn, 0], YN))
            start(copy(o, 5, out.at[xn], out.at[xn], SB))
            copy(o, 1, ins[o], out.at[yn], YN).wait_recv()
            start(copy(o, 3, out.at[yn, 1], out.at[yn, 1], XN))
            start(copy(o, 6, out.at[yn], out.at[yn], SB))
        for o in range(n):
            out = outs[o]
            copy(o, 2, out.at[dg, 0], out.at[dg, 0], YN).wait_recv()
            copy(o, 3, out.at[dg, 1], out.at[dg, 1], XN).wait_recv()
            start(copy(o, 7, out.at[dg], out.at[dg], SB))
        for o in range(n):
            out = outs[o]
            for k, blk in ((4, me), (5, xn), (6, yn), (7, dg)):
                copy(o, k, out.at[flip_c(blk)], out.at[flip_c(blk)], SB).wait_recv()
        for cp in sends:
            cp.wait_send()
        for lc in locals_:
            lc.wait()

    return pl.pallas_call(
        body, name="allgather_weights", in_specs=[HBM_SPEC] * n, out_specs=[HBM_SPEC] * n,
        out_shape=[jax.ShapeDtypeStruct((NDEV,) + s.shape, s.dtype) for s in shards],
        scratch_shapes=[pltpu.SemaphoreType.DMA((n * 8,)), pltpu.SemaphoreType.DMA((n * 8,)),
                        pltpu.SemaphoreType.DMA((n,))],
    )(*shards)


def _allgather_small(name, block):
    m_per, ncol = block.shape

    def body(x_ref, out_ref, send_sems, recv_sems, local_sem):
        x, y, c = _coords()
        me, sibling = (x, y, c), (x, y, 1 - c)
        chips = [(1 - x, y), (x, 1 - y), (1 - x, 1 - y)]

        def rows(px, py, pc):
            return out_ref.at[pl.ds((4 * px + 2 * py + pc) * m_per, m_per), :]

        def copy(k, block_of, to, src=None):
            return pltpu.make_async_remote_copy(
                src_ref=rows(*block_of) if src is None else src, dst_ref=rows(*block_of),
                send_sem=send_sems.at[k], recv_sem=recv_sems.at[k], device_id=to, device_id_type=MESH)

        mine = pltpu.make_async_copy(x_ref, rows(*me), local_sem)
        mine.start()
        first = [copy(0, me, sibling, src=x_ref)]
        first += [copy(1 + j, me, (*chip, c), src=x_ref) for j, chip in enumerate(chips)]
        for cp in first:
            cp.start()
        passed = [copy(4 + j, (*chip, c), sibling) for j, chip in enumerate(chips)]
        for j, chip in enumerate(chips):
            copy(1 + j, (*chip, c), me).wait_recv()
            passed[j].start()
        copy(0, sibling, me).wait_recv()
        for j, chip in enumerate(chips):
            copy(4 + j, (*chip, 1 - c), me).wait_recv()
        for cp in first + passed:
            cp.wait_send()
        mine.wait()

    return pl.pallas_call(
        body, name=name, out_shape=jax.ShapeDtypeStruct((NDEV * m_per, ncol), block.dtype),
        in_specs=[pl.BlockSpec(memory_space=pltpu.VMEM)], out_specs=pl.BlockSpec(memory_space=pltpu.VMEM),
        scratch_shapes=[pltpu.SemaphoreType.DMA((7,)), pltpu.SemaphoreType.DMA((7,)), pltpu.SemaphoreType.DMA],
        compiler_params=pltpu.CompilerParams(vmem_limit_bytes=VMEM_LIMIT),
    )(block)


def _run_exchange(name, srcs, out_structs, plan, n_copies):
    n_in, n_out = len(srcs), len(out_structs)

    def body(*refs):
        ins, outs = refs[:n_in], refs[n_in:n_in + n_out]
        send_sems, recv_sems = refs[n_in + n_out:]
        x, y, c = _coords()
        cps = []
        for k, (src, dst, peer) in enumerate(plan(x, y, c, ins, outs)):
            cp = pltpu.make_async_remote_copy(src_ref=src, dst_ref=dst, send_sem=send_sems.at[k],
                                              recv_sem=recv_sems.at[k], device_id=peer, device_id_type=MESH)
            cp.start()
            cps.append(cp)
        for cp in cps:
            cp.wait_recv()
        for cp in cps:
            cp.wait_send()

    return pl.pallas_call(
        body, name=name, in_specs=[HBM_SPEC] * n_in, out_specs=[HBM_SPEC] * n_out, out_shape=out_structs,
        scratch_shapes=[pltpu.SemaphoreType.DMA((n_copies,)), pltpu.SemaphoreType.DMA((n_copies,))],
    )(*srcs)


def _rs_stage_c(name, gs):
    def plan(x, y, c, ins, outs):
        sib = (x, y, 1 - c)
        return [(g.at[q, 1 - c], r.at[q], sib) for g, r in zip(ins, outs) for q in range(4)]

    outs = [jax.ShapeDtypeStruct((4,) + g.shape[2:], g.dtype) for g in gs]
    return _run_exchange(name, gs, outs, plan, 4 * len(gs))


def _rs_stage_ici(name, sends, first):
    def plan(x, y, c, ins, outs):
        XN, YN = (1 - x, y, c), (x, 1 - y, c)
        peers = (YN, XN) if first else (XN, YN)
        return [(s.at[h], r.at[h], peers[h]) for s, r in zip(ins, outs) for h in range(2)]

    outs = [jax.ShapeDtypeStruct(s.shape, s.dtype) for s in sends]
    return _run_exchange(name, sends, outs, plan, 2 * len(sends))


def _coord_vec():
    x, y, c = _coords()
    return jnp.stack([x, y, c]).astype(jnp.int32)


def _rs_add1(name, g, r1, coords):
    R, L = g.shape[3], g.shape[4]
    tr = _tile(R, 512, 16)

    def qk(h, idx, cr):
        return jnp.where(h == 0, 2 * idx + cr[1], 2 * cr[0] + idx)

    def qs(h, idx, cr):
        return jnp.where(h == 0, 2 * idx + 1 - cr[1], 2 * (1 - cr[0]) + idx)

    def body(cr, gk, rk, gsd, rsd, keep, send):
        keep[...] = gk[...] + rk[...]
        send[...] = (gsd[...] + rsd[...]).astype(BF16)

    gspec = lambda qf: pl.BlockSpec((None, None, None, tr, L), lambda h, idx, r, cr: (qf(h, idx, cr), cr[2], h, r, 0))
    rspec = lambda qf: pl.BlockSpec((None, None, tr, L), lambda h, idx, r, cr: (qf(h, idx, cr), h, r, 0))
    ospec = pl.BlockSpec((None, None, tr, L), lambda h, idx, r, cr: (h, idx, r, 0))
    return pl.pallas_call(
        body, name=name,
        grid_spec=pltpu.PrefetchScalarGridSpec(
            num_scalar_prefetch=1, grid=(2, 2, R // tr),
            in_specs=[gspec(qk), rspec(qk), gspec(qs), rspec(qs)], out_specs=[ospec, ospec]),
        out_shape=[jax.ShapeDtypeStruct((2, 2, R, L), F32), jax.ShapeDtypeStruct((2, 2, R, L), BF16)],
        compiler_params=_cparams("parallel", "parallel", "parallel"),
    )(coords, g, r1, g, r1)


def _rs_add2(name, keep2, recv2, coords):
    R, L = keep2.shape[2], keep2.shape[3]
    tr = _tile(R, 512, 16)

    def mine(h, cr):
        return jnp.where(h == 0, cr[0], cr[1])

    def body(cr, kk, rk, ks, rs, keep, send):
        keep[...] = kk[...] + rk[...].astype(F32)
        send[...] = (ks[...] + rs[...].astype(F32)).astype(BF16)

    sel = lambda f: pl.BlockSpec((None, None, tr, L), lambda h, r, cr: (h, f(h, cr), r, 0))
    ospec = pl.BlockSpec((None, tr, L), lambda h, r, cr: (h, r, 0))
    other = lambda h, cr: 1 - mine(h, cr)
    return pl.pallas_call(
        body, name=name,
        grid_spec=pltpu.PrefetchScalarGridSpec(
            num_scalar_prefetch=1, grid=(2, R // tr),
            in_specs=[sel(mine), sel(mine), sel(other), sel(other)], out_specs=[ospec, ospec]),
        out_shape=[jax.ShapeDtypeStruct((2, R, L), F32), jax.ShapeDtypeStruct((2, R, L), BF16)],
        compiler_params=_cparams("parallel", "parallel"),
    )(coords, keep2, recv2, keep2, recv2)


def _rs_add3(name, keep3, recv3):
    R, L = keep3.shape[1], keep3.shape[2]
    tr = _tile(R, 512, 16)

    def body(k, r, o):
        o[...] = k[...] + r[...].astype(F32)

    spec = pl.BlockSpec((None, tr, L), lambda h, r: (h, r, 0))
    return pl.pallas_call(body, name=name, grid=(2, R // tr), in_specs=[spec, spec], out_specs=spec,
                          out_shape=jax.ShapeDtypeStruct((2, R, L), F32),
                          compiler_params=_cparams("parallel", "parallel"))(keep3, recv3)


def _reduce_scatter(tag, gs, coords):
    canon = []
    for g in gs:
        per = int(np.prod(g.shape[1:]))
        L = g.shape[-1]
        canon.append(g.reshape(4, 2, 2, per // (2 * L), L))
    r1 = _rs_stage_c(f"rs_c_{tag}", canon)
    keep2, send2 = zip(*[_rs_add1(f"rs_add1_{tag}_{k}", g, r, coords) for k, (g, r) in enumerate(zip(canon, r1))])
    r2 = _rs_stage_ici(f"rs_ici1_{tag}", list(send2), True)
    keep3, send3 = zip(*[_rs_add2(f"rs_add2_{tag}_{k}", kp, r, coords) for k, (kp, r) in enumerate(zip(keep2, r2))])
    r3 = _rs_stage_ici(f"rs_ici2_{tag}", list(send3), False)
    return [_rs_add3(f"rs_add3_{tag}_{k}", kp, r) for k, (kp, r) in enumerate(zip(keep3, r3))]


def _adamw(name, w, g, m, v):
    R, L = w.shape
    tr = _tile(R, 256, 8)

    def body(w_ref, g_ref, m_ref, v_ref, d_ref, nm_ref, nv_ref):
        gv = g_ref[...]
        nm = ADAM_B1 * m_ref[...] + (1.0 - ADAM_B1) * gv
        nv = ADAM_B2 * v_ref[...] + (1.0 - ADAM_B2) * (gv * gv)
        m_hat = nm / (1.0 - ADAM_B1 ** ADAM_STEP)
        v_hat = nv / (1.0 - ADAM_B2 ** ADAM_STEP)
        d_ref[...] = -ADAM_LR * (m_hat / (jnp.sqrt(v_hat) + ADAM_EPS) + ADAM_WD * w_ref[...])
        nm_ref[...] = nm
        nv_ref[...] = nv

    spec = pl.BlockSpec((tr, L), lambda i: (i, 0))
    out = jax.ShapeDtypeStruct((R, L), F32)
    return pl.pallas_call(body, name=name, grid=(R // tr,), in_specs=[spec] * 4, out_specs=[spec] * 3,
                          out_shape=[out, out, out], compiler_params=_cparams("parallel"))(w, g, m, v)


def _sum_devices(name, gathered):
    _, R, L = gathered.shape
    tr = _tile(R, 512, 8)

    def body(g_ref, o_ref):
        acc = g_ref[0]
        for d in range(1, NDEV):
            acc = acc + g_ref[d]
        o_ref[...] = acc

    return pl.pallas_call(body, name=name, grid=(R // tr,),
                          in_specs=[pl.BlockSpec((NDEV, tr, L), lambda i: (0, i, 0))],
                          out_specs=pl.BlockSpec((tr, L), lambda i: (i, 0)),
                          out_shape=jax.ShapeDtypeStruct((R, L), F32),
                          compiler_params=_cparams("parallel"))(gathered)


def _pack(parts):
    flat, offs, pos = [], [], 0
    for p in parts:
        n = int(np.prod(p.shape))
        padded = -(-n // PACK_ALIGN) * PACK_ALIGN
        flat.append(jnp.pad(p.reshape(-1).astype(F32), (0, padded - n)))
        offs.append((pos, n, p.shape))
        pos += padded
    return jnp.concatenate(flat).reshape(-1, LANES), offs


def _unpack(packed, offs):
    flat = packed.reshape(-1)
    return [flat[pos:pos + n].reshape(shape) for pos, n, shape in offs]


def kernel(x, norm_mix, norm_ffn, rel_bias, attn_w_qkv, attn_q_gain, attn_k_gain, attn_sinks, attn_w_o, pool_w, pool_scale, ffn_w_up, ffn_conv_w, ffn_conv_b, ffn_w_down, loss_target, m_norm_mix, m_norm_ffn, m_rel_bias, m_attn_w_qkv, m_attn_q_gain, m_attn_k_gain, m_attn_sinks, m_attn_w_o, m_pool_w, m_pool_scale, m_ffn_w_up, m_ffn_conv_w, m_ffn_conv_b, m_ffn_w_down, v_norm_mix, v_norm_ffn, v_rel_bias, v_attn_w_qkv, v_attn_q_gain, v_attn_k_gain, v_attn_sinks, v_attn_w_o, v_pool_w, v_pool_scale, v_ffn_w_up, v_ffn_conv_w, v_ffn_conv_b, v_ffn_w_down):
    xs = x[0]
    target = loss_target[0]
    S, D = xs.shape
    depth = norm_mix.shape[0]
    H = D // HEAD_DIM
    n_attn, n_pool = attn_w_qkv.shape[0], pool_w.shape[0]
    QS = attn_w_qkv.shape[2]
    CB = ffn_w_up.shape[2]
    FB = ffn_w_down.shape[1]
    FF = FB * NDEV
    G, PC, C = pool_w.shape[1], pool_w.shape[2], pool_w.shape[3]
    xi, yi, ci = _coords()
    me = 4 * xi + 2 * yi + ci
    coords = _coord_vec()

    halves = lambda w: w.astype(BF16).reshape(2, -1, w.shape[-1])
    g_qkv, g_o, g_pool, g_up, g_down = _allgather_big(
        [halves(attn_w_qkv), halves(attn_w_o), halves(pool_w), halves(ffn_w_up), halves(ffn_w_down)])
    wqkv = g_qkv.reshape(NDEV, n_attn, D, QS).transpose(1, 2, 0, 3).reshape(n_attn, D, NDEV * QS)
    wo = g_o.reshape(NDEV, n_attn, D // NDEV, D).transpose(1, 0, 2, 3).reshape(n_attn, D, D)
    wp = g_pool.reshape(NDEV, n_pool, G, PC, C).transpose(1, 2, 0, 3, 4).reshape(n_pool, G, C, C)
    wup3 = g_up.reshape(NDEV, depth, D, CB).transpose(1, 2, 0, 3).reshape(depth, D, NDEV * CB)
    wd3 = g_down.reshape(NDEV, depth, FB, D).transpose(1, 0, 2, 3).reshape(depth, FF, D)

    small_in, small_in_offs = _pack([ffn_conv_w, pool_scale])
    gathered_in = _allgather_small("allgather_small_params", small_in).reshape(NDEV, -1)
    per_dev = [_unpack(gathered_in[d], small_in_offs) for d in range(NDEV)]
    conv_w_full = jnp.concatenate([p[0] for p in per_dev], axis=2)
    pool_scale_full = jnp.concatenate([p[1] for p in per_dev], axis=1)

    onehot = jnp.asarray(_t5_onehot())
    bias = _bias_band(rel_bias, onehot).reshape(H, WINDOW, 2 * WINDOW)

    saved = []
    cur = xs
    for i in range(depth):
        j = i // 2
        st = {"x0": cur}
        if i % 2 == 0:
            h = _rmsnorm_fwd(f"norm_mix_{i}", cur, norm_mix[i:i + 1])
            qkv = _mm_rows(f"qkv_{i}", h, wqkv[j], NN, F32)
            qg, kg, sk = attn_q_gain[j:j + 1], attn_k_gain[j:j + 1], attn_sinks[j].reshape(H, 1)
            o, lse = _attn_fwd(f"attn_fwd_{i}", qkv, bias, qg, kg, sk)
            x1 = _mm_rows(f"attn_out_{i}", o, wo[j], NN, F32, res=cur)
            st.update(h=h, qkv=qkv, o=o, lse=lse)
        else:
            x1, dsave = _pool_fwd(f"pool_fwd_{i}", cur, norm_mix[i:i + 1], wp[j], pool_scale_full[j:j + 1])
            st.update(dsave=dsave)
        h2 = _rmsnorm_fwd(f"norm_ffn_{i}", x1, norm_ffn[i:i + 1])
        u3, uc3, a = _ffn_up_fwd(f"ffn_up_{i}", h2, wup3, i, conv_w_full[i], ffn_conv_b[i:i + 1])
        cur = _ffn_down_fwd(f"ffn_down_{i}", a, wd3, i, x1)
        st.update(x1=x1, h2=h2, u3=u3, uc3=uc3, a=a)
        saved.append(st)

    loss_tile, dcur, dcur_b = _loss_fwd_bwd(cur, target)

    g_up_l, g_down_l = [None] * depth, [None] * depth
    g_qkv_l, g_o_l, g_pool_l = [None] * n_attn, [None] * n_attn, [None] * n_pool
    d_norm_mix, d_norm_ffn = [None] * depth, [None] * depth
    d_conv_w, d_conv_b = [None] * depth, [None] * depth
    d_qg, d_kg, d_sinks, d_pscale = [None] * n_attn, [None] * n_attn, [None] * n_attn, [None] * n_pool
    d_band = None
    for i in reversed(range(depth)):
        j = i // 2
        st = saved[i]
        du3, dc = _ffn_da_bwd(f"ffn_da_{i}", dcur_b, wd3, i, st["u3"], st["uc3"], conv_w_full[i])
        d_conv_w[i] = jnp.concatenate([dc[0, 0:3], dc[1, 0:3]], axis=1)
        d_conv_b[i] = jnp.concatenate([dc[0, 3], dc[1, 3]], axis=0)
        dwdown = _mm_tn(f"ffn_dwdown_{i}", st["a"], dcur_b, tm_pref=CB if CB % LANES == 0 else 1024).reshape(NDEV, FB, D)
        dwup = _ffn_dwup_bwd(f"ffn_dwup_{i}", st["h2"], du3)
        dh2 = _ffn_dh_bwd(f"ffn_dh_{i}", du3, wup3, i)
        dx1, dx1_b, dg = _rmsnorm_bwd(f"norm_ffn_bwd_{i}", dh2, st["x1"], norm_ffn[i:i + 1], dcur)
        d_norm_ffn[i] = dg[0]
        grads = [dwup, dwdown]
        if i % 2 == 0:
            do = _mm_rows(f"attn_do_{i}", dx1_b, wo[j], NT, BF16)
            dwo = _mm_tn(f"attn_dwo_{i}", st["o"], dx1_b).reshape(NDEV, D // NDEV, D)
            qg, kg, sk = attn_q_gain[j:j + 1], attn_k_gain[j:j + 1], attn_sinks[j].reshape(H, 1)
            dq, dkv, db, dsk, dqg, dkg = _attn_bwd(f"attn_bwd_{i}", st["qkv"], do, st["lse"], bias, qg, kg, sk)
            d_band = db if d_band is None else d_band + db
            d_sinks[j], d_qg[j], d_kg[j] = dsk[:, 0], dqg[0], dkg[0]
            dqkv = jnp.concatenate([dq, dkv], axis=1)
            dwqkv = _mm_tn(f"attn_dwqkv_{i}", st["h"], dqkv, tn_pref=1280)
            dwqkv = dwqkv.reshape(D, NDEV, QS).transpose(1, 0, 2)
            dh = _mm_rows(f"attn_dh_{i}", dqkv, wqkv[j], NT, F32)
            dcur, dcur_b, dg = _rmsnorm_bwd(f"norm_mix_bwd_{i}", dh, st["x0"], norm_mix[i:i + 1], dx1)
            grads += [dwqkv, dwo]
        else:
            dcur, dcur_b, dwp, dps, dg = _pool_bwd(f"pool_bwd_{i}", dx1, st["x0"], norm_mix[i:i + 1], wp[j],
                                           pool_scale_full[j:j + 1], st["dsave"])
            d_pscale[j] = dps[0]
            grads += [dwp.reshape(G, NDEV, PC, C).transpose(1, 0, 2, 3)]
        d_norm_mix[i] = dg[0]
        red = _reduce_scatter(f"l{i}", grads, coords)
        g_up_l[i] = red[0].reshape(D, CB)
        g_down_l[i] = red[1].reshape(FB, D)
        if i % 2 == 0:
            g_qkv_l[j] = red[2].reshape(D, QS)
            g_o_l[j] = red[3].reshape(D // NDEV, D)
        else:
            g_pool_l[j] = red[2].reshape(G, PC, C)

    d_rel = _bias_band_bwd(d_band.reshape(H, -1), onehot)

    small_parts = [loss_tile, jnp.stack(d_norm_mix), jnp.stack(d_norm_ffn), d_rel, jnp.stack(d_qg),
                   jnp.stack(d_kg), jnp.stack(d_sinks), jnp.stack(d_pscale), jnp.stack(d_conv_w),
                   jnp.stack(d_conv_b)]
    small, small_offs = _pack(small_parts)
    gathered = _allgather_small("allgather_small_grads", small).reshape(NDEV, -1, LANES)
    summed = _unpack(_sum_devices("sum_small_grads", gathered), small_offs)
    loss = summed[0][0, 0]
    (g_norm_mix, g_norm_ffn, g_rel, g_qg, g_kg, g_sinks, g_pscale_full, g_conv_w_full, g_conv_b) = summed[1:]
    g_pscale = lax.dynamic_slice_in_dim(g_pscale_full, me * (D // NDEV), D // NDEV, axis=1)
    g_conv_w = lax.dynamic_slice_in_dim(g_conv_w_full, me * CB, CB, axis=2)

    grads = {
        "norm_mix": g_norm_mix, "norm_ffn": g_norm_ffn, "rel_bias": g_rel, "attn_w_qkv": jnp.stack(g_qkv_l),
        "attn_q_gain": g_qg, "attn_k_gain": g_kg, "attn_sinks": g_sinks, "attn_w_o": jnp.stack(g_o_l),
        "pool_w": jnp.stack(g_pool_l), "pool_scale": g_pscale, "ffn_w_up": jnp.stack(g_up_l),
        "ffn_conv_w": g_conv_w, "ffn_conv_b": g_conv_b, "ffn_w_down": jnp.stack(g_down_l),
    }
    weights = {
        "norm_mix": (norm_mix, m_norm_mix, v_norm_mix), "norm_ffn": (norm_ffn, m_norm_ffn, v_norm_ffn),
        "rel_bias": (rel_bias, m_rel_bias, v_rel_bias), "attn_w_qkv": (attn_w_qkv, m_attn_w_qkv, v_attn_w_qkv),
        "attn_q_gain": (attn_q_gain, m_attn_q_gain, v_attn_q_gain),
        "attn_k_gain": (attn_k_gain, m_attn_k_gain, v_attn_k_gain),
        "attn_sinks": (attn_sinks, m_attn_sinks, v_attn_sinks), "attn_w_o": (attn_w_o, m_attn_w_o, v_attn_w_o),
        "pool_w": (pool_w, m_pool_w, v_pool_w), "pool_scale": (pool_scale, m_pool_scale, v_pool_scale),
        "ffn_w_up": (ffn_w_up, m_ffn_w_up, v_ffn_w_up), "ffn_conv_w": (ffn_conv_w, m_ffn_conv_w, v_ffn_conv_w),
        "ffn_conv_b": (ffn_conv_b, m_ffn_conv_b, v_ffn_conv_b), "ffn_w_down": (ffn_w_down, m_ffn_w_down, v_ffn_w_down),
    }
    names = list(weights)
    big = ("attn_w_qkv", "attn_w_o", "pool_w", "ffn_w_up", "ffn_w_down")
    upd = {}
    for nm in big:
        w, m, v = weights[nm]
        two_d = lambda t: t.reshape(-1, w.shape[-1])
        d_, m_, v_ = _adamw(f"adamw_{nm}", two_d(w), two_d(grads[nm]), two_d(m), two_d(v))
        upd[nm] = (d_.reshape(w.shape), m_.reshape(w.shape), v_.reshape(w.shape))
    small_names = [nm for nm in names if nm not in big]
    pw, offs = _pack([weights[nm][0] for nm in small_names])
    pg, _ = _pack([grads[nm] for nm in small_names])
    pm, _ = _pack([weights[nm][1] for nm in small_names])
    pv, _ = _pack([weights[nm][2] for nm in small_names])
    d_, m_, v_ = _adamw("adamw_small", pw, pg, pm, pv)
    for nm, dd, mm, vv in zip(small_names, _unpack(d_, offs), _unpack(m_, offs), _unpack(v_, offs)):
        upd[nm] = (dd, mm, vv)

    grad_x = dcur[None]
    return (loss, grad_x, *[grads[nm].reshape(weights[nm][0].shape) for nm in names],
            *[upd[nm][0] for nm in names], *[upd[nm][1] for nm in names], *[upd[nm][2] for nm in names])
```

```python
import numpy as np
import jax
import jax.numpy as jnp
from jax import lax
from jax.experimental import pallas as pl
from jax.experimental.pallas import tpu as pltpu

F32 = jnp.float32
BF16 = jnp.bfloat16
MESH = pl.DeviceIdType.MESH

NDEV = 8
HEAD_DIM = 64
GQA_GROUP = 8
WINDOW = 128
N_BUCKETS = 32
MAX_DISTANCE = 128
POOL_WINDOWS = (2, 4, 8, 16)
POOL_HALO = 32
EPS = 1e-6
NEG_INF = -1e30
ADAM_LR = 0.001
ADAM_B1 = 0.9
ADAM_B2 = 0.999
ADAM_EPS = 1e-08
ADAM_WD = 0.01
ADAM_STEP = 10

V7X_VMEM_BYTES = 64 * 1024 * 1024
VMEM_LIMIT = V7X_VMEM_BYTES - 8 * 1024 * 1024
LANES = 128
SUBLANES = 8
PACK_ALIGN = SUBLANES * LANES

NN = (((1,), (0,)), ((), ()))
NT = (((1,), (1,)), ((), ()))
TN = (((0,), (0,)), ((), ()))


def _tile(dim, pref, align):
    t = min(pref, dim)
    t -= t % align
    while t >= align:
        if dim % t == 0:
            return t
        t -= align
    return dim


def _cparams(*sem):
    return pltpu.CompilerParams(dimension_semantics=sem, vmem_limit_bytes=VMEM_LIMIT)


def _bf(v):
    return v if v.dtype == BF16 else v.astype(BF16)


class _Exchange:
    def __init__(self, srcs, out_structs, plan, n_copies):
        self.srcs, self.out_structs, self.plan, self.n_copies = list(srcs), list(out_structs), plan, n_copies

    def in_specs(self):
        return [HBM_SPEC] * len(self.srcs)

    def out_specs(self):
        return [HBM_SPEC] * len(self.out_structs)

    def scratch(self):
        return [pltpu.SemaphoreType.DMA((self.n_copies,)), pltpu.SemaphoreType.DMA((self.n_copies,))]

    def run(self, in_refs, out_refs, send_sems, recv_sems, is_first, is_last):
        def copies():
            x, y, c = _coords()
            return [pltpu.make_async_remote_copy(src_ref=src, dst_ref=dst, send_sem=send_sems.at[k],
                                                 recv_sem=recv_sems.at[k], device_id=peer, device_id_type=MESH)
                    for k, (src, dst, peer) in enumerate(self.plan(x, y, c, in_refs, out_refs))]

        def start():
            for cp in copies():
                cp.start()

        def finish():
            cps = copies()
            for cp in cps:
                cp.wait_recv()
            for cp in cps:
                cp.wait_send()

        if is_first is True and is_last is True:
            start()
            finish()
        else:
            pl.when(is_first)(start)
            pl.when(is_last)(finish)


def _grid_edges(grid):
    first, last = True, True
    for ax, n in enumerate(grid):
        first = jnp.logical_and(first, pl.program_id(ax) == 0)
        last = jnp.logical_and(last, pl.program_id(ax) == n - 1)
    return first, last


def _mm(name, a, b, *, grid, a_spec, b_spec, o_spec, out_shape, contract, acc_shape, res=None, comm=None):
    nk = grid[2]
    n_main = 3 if res is not None else 2
    n_ci = len(comm.srcs) if comm else 0
    n_co = len(comm.out_structs) if comm else 0

    def body(*refs):
        a_ref, b_ref = refs[:2]
        r_ref = refs[2] if res is not None else None
        o_ref = refs[n_main + n_ci]
        scr = refs[n_main + n_ci + 1 + n_co:]
        if comm:
            first, last = _grid_edges(grid)
            comm.run(refs[n_main:n_main + n_ci], refs[n_main + n_ci + 1:n_main + n_ci + 1 + n_co],
                     scr[-2], scr[-1], first, last)
        part = lax.dot_general(_bf(a_ref[...]), _bf(b_ref[...]), contract, preferred_element_type=F32)

        def finish(acc):
            if r_ref is not None:
                acc = acc + r_ref[...]
            o_ref[...] = acc.astype(o_ref.dtype)

        if nk == 1:
            finish(part)
        else:
            acc_ref = scr[0]
            k = pl.program_id(2)

            @pl.when(k == 0)
            def _():
                acc_ref[...] = part

            @pl.when(k > 0)
            def _():
                acc_ref[...] += part

            @pl.when(k == nk - 1)
            def _():
                finish(acc_ref[...])

    in_specs = [a_spec, b_spec] + ([o_spec] if res is not None else [])
    args = (a, b) + ((res,) if res is not None else ())
    out_specs, out_shapes = o_spec, out_shape
    scratch = [pltpu.VMEM(acc_shape, F32)] if nk > 1 else []
    if comm:
        in_specs += comm.in_specs()
        args += tuple(comm.srcs)
        out_specs = [o_spec] + comm.out_specs()
        out_shapes = [out_shape] + comm.out_structs
        scratch += comm.scratch()
    return pl.pallas_call(
        body, name=name, grid=grid, in_specs=in_specs, out_specs=out_specs, out_shape=out_shapes,
        scratch_shapes=scratch,
        compiler_params=_cparams(*(("arbitrary",) * 3 if comm else ("parallel", "parallel", "arbitrary"))),
    )(*args)


def _mm_rows(name, a, b, contract, out_dtype, res=None, tm_pref=512, comm=None):
    S, K = a.shape
    N = b.shape[1] if contract == NN else b.shape[0]
    tm = _tile(S, tm_pref, 16)
    return _mm(name, a, b, grid=(1, S // tm, 1),
               a_spec=pl.BlockSpec((tm, K), lambda p, q, k: (q, 0)),
               b_spec=pl.BlockSpec(b.shape, lambda p, q, k: (0, 0)),
               o_spec=pl.BlockSpec((tm, N), lambda p, q, k: (q, 0)),
               out_shape=jax.ShapeDtypeStruct((S, N), out_dtype), contract=contract,
               acc_shape=(tm, N), res=res, comm=comm)


def _mm_tn(name, a, b, tm_pref=1024, tn_pref=1024, tk_pref=2048, comm=None):
    S, M = a.shape
    N = b.shape[1]
    tm, tn, tk = _tile(M, tm_pref, LANES), _tile(N, tn_pref, LANES), _tile(S, tk_pref, 16)
    return _mm(name, a, b, grid=(M // tm, N // tn, S // tk),
               a_spec=pl.BlockSpec((tk, tm), lambda p, q, k: (k, p)),
               b_spec=pl.BlockSpec((tk, tn), lambda p, q, k: (k, q)),
               o_spec=pl.BlockSpec((tm, tn), lambda p, q, k: (p, q)),
               out_shape=jax.ShapeDtypeStruct((M, N), F32), contract=TN, acc_shape=(tm, tn), comm=comm)


def _rmsnorm_fwd(name, x, gain):
    S, D = x.shape
    tm = _tile(S, 512, 16)

    def body(x_ref, g_ref, o_ref):
        xf = x_ref[...]
        r = lax.rsqrt(jnp.mean(xf * xf, axis=-1, keepdims=True) + EPS)
        o_ref[...] = (xf * r * g_ref[...]).astype(o_ref.dtype)

    return pl.pallas_call(
        body, name=name, grid=(S // tm,),
        in_specs=[pl.BlockSpec((tm, D), lambda i: (i, 0)), pl.BlockSpec((1, D), lambda i: (0, 0))],
        out_specs=pl.BlockSpec((tm, D), lambda i: (i, 0)),
        out_shape=jax.ShapeDtypeStruct((S, D), BF16), compiler_params=_cparams("parallel"),
    )(x, gain)


def _rms_bwd_math(dh, xf, gain):
    r = lax.rsqrt(jnp.mean(xf * xf, axis=-1, keepdims=True) + EPS)
    xhat = xf * r
    dxh = dh * gain
    dx = r * (dxh - xhat * jnp.mean(dxh * xhat, axis=-1, keepdims=True))
    return dx, jnp.sum(dh * xhat, axis=0, keepdims=True)


def _rmsnorm_bwd(name, dh, x, gain, dres):
    S, D = x.shape
    tm = _tile(S, 256, 16)

    def body(dh_ref, x_ref, g_ref, dr_ref, dx_ref, dxb_ref, dg_ref):
        dx, dg = _rms_bwd_math(dh_ref[...], x_ref[...], g_ref[...])
        dx = dr_ref[...] + dx
        dx_ref[...] = dx
        dxb_ref[...] = dx.astype(BF16)

        @pl.when(pl.program_id(0) == 0)
        def _():
            dg_ref[...] = dg

        @pl.when(pl.program_id(0) > 0)
        def _():
            dg_ref[...] += dg

    row = pl.BlockSpec((tm, D), lambda i: (i, 0))
    vec = pl.BlockSpec((1, D), lambda i: (0, 0))
    return pl.pallas_call(
        body, name=name, grid=(S // tm,), in_specs=[row, row, vec, row], out_specs=[row, row, vec],
        out_shape=[jax.ShapeDtypeStruct((S, D), F32), jax.ShapeDtypeStruct((S, D), BF16),
                   jax.ShapeDtypeStruct((1, D), F32)],
        compiler_params=_cparams("arbitrary"),
    )(dh, x, gain, dres)


def _loss_fwd_bwd(y, target):
    S, D = y.shape
    tm = _tile(S, 512, 16)

    def body(y_ref, t_ref, l_ref, dy_ref, dyb_ref):
        e = y_ref[...] - t_ref[...]
        dy = e * (1.0 / D)
        dy_ref[...] = dy
        dyb_ref[...] = dy.astype(BF16)
        part = 0.5 * jnp.sum(jnp.mean(e * e, axis=-1, keepdims=True), axis=0, keepdims=True)
        part = jnp.broadcast_to(part, (SUBLANES, LANES))

        @pl.when(pl.program_id(0) == 0)
        def _():
            l_ref[...] = part

        @pl.when(pl.program_id(0) > 0)
        def _():
            l_ref[...] += part

    row = pl.BlockSpec((tm, D), lambda i: (i, 0))
    return pl.pallas_call(
        body, name="loss", grid=(S // tm,), in_specs=[row, row],
        out_specs=[pl.BlockSpec((SUBLANES, LANES), lambda i: (0, 0)), row, row],
        out_shape=[jax.ShapeDtypeStruct((SUBLANES, LANES), F32), jax.ShapeDtypeStruct((S, D), F32),
                   jax.ShapeDtypeStruct((S, D), BF16)],
        compiler_params=_cparams("arbitrary"),
    )(y, target)


def _sigmoid(v):
    return 1.0 / (1.0 + jnp.exp(-v))


def _shift_rows(v, k, edge8, down):
    tm = v.shape[0]
    sub = lax.broadcasted_iota(jnp.int32, edge8.shape, 0)
    if down:
        r = pltpu.roll(v, k, axis=0)
        head = jnp.where(sub < k, pltpu.roll(edge8, k, axis=0), r[0:8, :])
        return jnp.concatenate([head, r[8:, :]], axis=0)
    r = pltpu.roll(v, tm - k, axis=0)
    tail = jnp.where(sub >= 8 - k, pltpu.roll(edge8, 8 - k, axis=0), r[tm - 8:tm, :])
    return jnp.concatenate([r[:tm - 8, :], tail], axis=0)


def _ffn_up_fwd(name, h2, wup, cw, cb, ag=None):
    S, D = h2.shape
    FF = wup.shape[1] // 2
    CB = _tile(FF, 1408, LANES)
    NJ = FF // CB
    tm = _tile(S, 512, 16)
    nI = S // tm
    n_steps = NJ * nI
    n_ag = len(ag.items) if ag else 0
    schedule = (0, (9 * n_steps) // 20, (7 * n_steps) // 10, n_steps - 1)

    def body(*refs):
        h_ref, wg_ref, wv_ref, cwg_ref, cwv_ref, cbg_ref, cbv_ref = refs[:7]
        u_ref, uc_ref, a_ref = refs[7 + n_ag:10 + n_ag]
        edge_g, edge_v = refs[10 + 2 * n_ag:12 + 2 * n_ag]
        if ag:
            ag.run_at(pl.program_id(0) * nI + pl.program_id(1), schedule, refs[7:7 + n_ag],
                      refs[10 + n_ag:10 + 2 * n_ag], *refs[12 + 2 * n_ag:])

        @pl.when(pl.program_id(1) == 0)
        def _():
            edge_g[...] = jnp.zeros((8, CB), F32)
            edge_v[...] = jnp.zeros((8, CB), F32)

        h = h_ref[...]

        def conv(w_ref, cw_ref, cb_ref, edge, slot):
            u = jnp.dot(h, w_ref[...], preferred_element_type=F32)
            u_ref[slot] = u.astype(BF16)
            prev8 = edge[...]
            uc = (cw_ref[0:1, :] * _shift_rows(u, 2, prev8, True) + cw_ref[1:2, :] * _shift_rows(u, 1, prev8, True)
                  + cw_ref[2:3, :] * u + cb_ref[...])
            edge[...] = u[tm - 8:tm, :]
            uc_ref[slot] = uc.astype(BF16)
            return uc

        gc = conv(wg_ref, cwg_ref, cbg_ref, edge_g, 0)
        vc = conv(wv_ref, cwv_ref, cbv_ref, edge_v, 1)
        a_ref[...] = (gc * _sigmoid(gc) * vc).astype(BF16)

    def wspec(off):
        return pl.BlockSpec((D, CB), lambda j, i: (0, j + off))

    def cspec(rows, off):
        return pl.BlockSpec((rows, CB), lambda j, i: (0, j + off))

    pair = pl.BlockSpec((2, tm, CB), lambda j, i: (0, i, j))
    in_specs = [pl.BlockSpec((tm, D), lambda j, i: (i, 0)), wspec(0), wspec(NJ),
                cspec(3, 0), cspec(3, NJ), cspec(1, 0), cspec(1, NJ)]
    out_specs = [pair, pair, pl.BlockSpec((tm, CB), lambda j, i: (i, j))]
    out_shape = [jax.ShapeDtypeStruct((2, S, FF), BF16), jax.ShapeDtypeStruct((2, S, FF), BF16),
                 jax.ShapeDtypeStruct((S, FF), BF16)]
    scratch = [pltpu.VMEM((8, CB), F32), pltpu.VMEM((8, CB), F32)]
    args = (h2, wup, wup, cw, cw, cb, cb)
    if ag:
        in_specs += ag.in_specs()
        out_specs += ag.out_specs()
        out_shape += ag.out_structs()
        scratch += ag.scratch()
        args += tuple(ag.srcs())
    return pl.pallas_call(
        body, name=name, grid=(NJ, nI), in_specs=in_specs, out_specs=out_specs, out_shape=out_shape,
        scratch_shapes=scratch, compiler_params=_cparams("arbitrary", "arbitrary"),
    )(*args)


def _ffn_da_bwd(name, dyb, wd, u3, uc3, cw):
    S, D = dyb.shape
    FF = wd.shape[0]
    CB = _tile(FF, 1408, LANES)
    NJ = FF // CB
    tm = _tile(S, 256, 16)
    nI = S // tm

    def body(dy_ref, wd_ref, u_ref, uc_ref, cwg_ref, cwv_ref, du_ref, dc_ref, edge_g, edge_v):
        i = pl.program_id(1)

        @pl.when(i == 0)
        def _():
            edge_g[...] = jnp.zeros((8, CB), F32)
            edge_v[...] = jnp.zeros((8, CB), F32)
            dc_ref[...] = jnp.zeros(dc_ref.shape, F32)

        da = lax.dot_general(dy_ref[...], wd_ref[...], NT, preferred_element_type=F32)
        gc = uc_ref[0].astype(F32)
        vc = uc_ref[1].astype(F32)
        sig = _sigmoid(gc)
        d_vc = da * (gc * sig)
        d_gc = da * vc * (sig * (1.0 + gc * (1.0 - sig)))

        def back(slot, d_uc, edge, cw_ref):
            u = u_ref[slot].astype(F32)
            next8 = edge[...]
            dp1 = _shift_rows(d_uc, 1, next8, False)
            dp2 = _shift_rows(d_uc, 2, next8, False)
            du = cw_ref[2:3, :] * d_uc + cw_ref[1:2, :] * dp1 + cw_ref[0:1, :] * dp2
            edge[...] = d_uc[0:8, :]
            du_ref[slot] = du.astype(BF16)
            dc_ref[slot, 0:1, :] += jnp.sum(dp2 * u, axis=0, keepdims=True)
            dc_ref[slot, 1:2, :] += jnp.sum(dp1 * u, axis=0, keepdims=True)
            dc_ref[slot, 2:3, :] += jnp.sum(d_uc * u, axis=0, keepdims=True)
            dc_ref[slot, 3:4, :] += jnp.sum(d_uc, axis=0, keepdims=True)

        back(0, d_gc, edge_g, cwg_ref)
        back(1, d_vc, edge_v, cwv_ref)

    def rev(i):
        return nI - 1 - i

    pair = pl.BlockSpec((2, tm, CB), lambda j, i: (0, rev(i), j))
    return pl.pallas_call(
        body, name=name, grid=(NJ, nI),
        in_specs=[pl.BlockSpec((tm, D), lambda j, i: (rev(i), 0)),
                  pl.BlockSpec((CB, D), lambda j, i: (j, 0)), pair, pair,
                  pl.BlockSpec((3, CB), lambda j, i: (0, j)), pl.BlockSpec((3, CB), lambda j, i: (0, j + NJ))],
        out_specs=[pair, pl.BlockSpec((2, 8, CB), lambda j, i: (0, 0, j))],
        out_shape=[jax.ShapeDtypeStruct((2, S, FF), BF16), jax.ShapeDtypeStruct((2, 8, FF), F32)],
        scratch_shapes=[pltpu.VMEM((8, CB), F32) for _ in range(2)],
        compiler_params=_cparams("parallel", "arbitrary"),
    )(dyb, wd, u3, uc3, cw, cw)


def _ffn_down_fwd(name, a, wd, res):
    S, FF = a.shape
    D = wd.shape[1]
    tm = _tile(S, 512, 16)
    tk = _tile(FF, 1408, LANES)
    return _mm(name, a, wd, grid=(1, S // tm, FF // tk),
               a_spec=pl.BlockSpec((tm, tk), lambda p, q, k: (q, k)),
               b_spec=pl.BlockSpec((tk, D), lambda p, q, k: (k, 0)),
               o_spec=pl.BlockSpec((tm, D), lambda p, q, k: (q, 0)),
               out_shape=jax.ShapeDtypeStruct((S, D), F32), contract=NN, acc_shape=(tm, D), res=res)


def _ffn_dh_bwd(name, du3, wup, comm=None):
    _, S, FF = du3.shape
    D = wup.shape[0]
    tm = _tile(S, 512, 16)
    tk = _tile(FF, 2816, LANES)
    nh = FF // tk
    return _mm(name, du3, wup, grid=(1, S // tm, 2 * nh),
               a_spec=pl.BlockSpec((None, tm, tk), lambda p, q, k: (k // nh, q, k % nh)),
               b_spec=pl.BlockSpec((D, tk), lambda p, q, k: (0, k)),
               o_spec=pl.BlockSpec((tm, D), lambda p, q, k: (q, 0)),
               out_shape=jax.ShapeDtypeStruct((S, D), F32), contract=NT, acc_shape=(tm, D), comm=comm)


def _ffn_dwup_bwd(name, h2, du3, comm=None):
    S, D = h2.shape
    FF = du3.shape[2]
    NJ = NDEV // 2
    CB = FF // NJ
    tm, tk = _tile(D, 1024, LANES), _tile(S, 2048, 16)
    return _mm(name, h2, du3, grid=(NDEV, D // tm, S // tk),
               a_spec=pl.BlockSpec((tk, tm), lambda p, q, k: (k, q)),
               b_spec=pl.BlockSpec((None, tk, CB), lambda p, q, k: (p // NJ, k, p % NJ)),
               o_spec=pl.BlockSpec((None, tm, CB), lambda p, q, k: (p, q, 0)),
               out_shape=jax.ShapeDtypeStruct((NDEV, D, CB), F32), contract=TN, acc_shape=(tm, CB), comm=comm)


def _t5_onehot():
    i = np.arange(WINDOW)[:, None]
    j = np.arange(2 * WINDOW)[None, :]
    n = np.maximum(WINDOW + i - j, 0)
    max_exact = N_BUCKETS // 2
    nf = np.maximum(n, 1).astype(np.float32)
    large = max_exact + (np.log(nf / max_exact) / np.log(MAX_DISTANCE / max_exact)
                         * (N_BUCKETS - max_exact)).astype(np.int32)
    large = np.minimum(large, N_BUCKETS - 1)
    bucket = np.where(n < max_exact, n, large).astype(np.int32).reshape(-1)
    return (np.arange(N_BUCKETS)[:, None] == bucket[None, :]).astype(np.float32)


def _bias_band(rel_bias, onehot):
    H = rel_bias.shape[0]
    n = onehot.shape[1]

    def body(r_ref, oh_ref, o_ref):
        o_ref[...] = jnp.dot(r_ref[...], oh_ref[...], preferred_element_type=F32,
                             precision=lax.Precision.HIGHEST)

    return pl.pallas_call(body, name="bias_band", out_shape=jax.ShapeDtypeStruct((H, n), F32),
                          compiler_params=pltpu.CompilerParams(vmem_limit_bytes=VMEM_LIMIT))(rel_bias, onehot)


def _bias_band_bwd(dband, onehot):
    H = dband.shape[0]

    def body(d_ref, oh_ref, o_ref):
        o_ref[...] = lax.dot_general(d_ref[...], oh_ref[...], NT, preferred_element_type=F32,
                                     precision=lax.Precision.HIGHEST)

    return pl.pallas_call(body, name="bias_band_bwd", out_shape=jax.ShapeDtypeStruct((H, N_BUCKETS), F32),
                          compiler_params=pltpu.CompilerParams(vmem_limit_bytes=VMEM_LIMIT))(dband, onehot)


def _band_valid(n):
    i = lax.broadcasted_iota(jnp.int32, (WINDOW, 2 * WINDOW), 0)
    j = lax.broadcasted_iota(jnp.int32, (WINDOW, 2 * WINDOW), 1)
    return (j > i) & (j <= i + WINDOW) & ((n > 0) | (j >= WINDOW))


def _head_norm(v, gain):
    r = lax.rsqrt(jnp.mean(v * v, axis=-1, keepdims=True) + EPS)
    vhat = v * r
    return r, vhat, vhat * gain


def _stack_heads(t, kh):
    return jnp.concatenate([t[:, (kh * GQA_GROUP + g) * HEAD_DIM:(kh * GQA_GROUP + g + 1) * HEAD_DIM]
                            for g in range(GQA_GROUP)], axis=0)


def _unstack_heads(t8):
    return jnp.concatenate([t8[g * WINDOW:(g + 1) * WINDOW, :] for g in range(GQA_GROUP)], axis=1)


def _band_scores(qn, kn, b_ref, kh, valid, scale):
    s = lax.dot_general(qn, kn, NT, preferred_element_type=F32) * scale
    s = s.reshape(GQA_GROUP, WINDOW, 2 * WINDOW) + b_ref[kh * GQA_GROUP:(kh + 1) * GQA_GROUP]
    s = jnp.where(valid[None], s, NEG_INF)
    return s.reshape(GQA_GROUP * WINDOW, 2 * WINDOW)


def _sink_rows(s_ref, kh):
    return jnp.concatenate([jnp.broadcast_to(s_ref[kh * GQA_GROUP + g:kh * GQA_GROUP + g + 1, :], (WINDOW, 1))
                            for g in range(GQA_GROUP)], axis=0)


def _attn_fwd(name, qkv, bias, qg, kg, sinks):
    S, QW = qkv.shape
    H = bias.shape[0]
    D = H * HEAD_DIM
    KV = H // GQA_GROUP
    kvw = QW - D
    kvb = D // kvw
    nb = S // WINDOW
    scale = HEAD_DIM ** -0.5

    def body(q_ref, kc_ref, kp_ref, b_ref, qg_ref, kg_ref, s_ref, o_ref, l_ref):
        n = pl.program_id(0)
        valid = _band_valid(n)
        q = q_ref[...]
        kvc = kc_ref[...]
        kvp = kp_ref[...]
        lane = lax.broadcasted_iota(jnp.int32, (WINDOW, H), 1)
        lse_all = jnp.zeros((WINDOW, H), F32)
        outs = []
        for kh in range(KV):
            ks = slice(kh * HEAD_DIM, (kh + 1) * HEAD_DIM)
            vs = slice((KV + kh) * HEAD_DIM, (KV + kh + 1) * HEAD_DIM)
            kb = jnp.concatenate([kvp[:, ks], kvc[:, ks]], axis=0)
            vb = jnp.concatenate([kvp[:, vs], kvc[:, vs]], axis=0).astype(BF16)
            kn = _head_norm(kb, kg_ref[...])[2].astype(BF16)
            qn = _head_norm(_stack_heads(q, kh), qg_ref[...])[2].astype(BF16)
            s = _band_scores(qn, kn, b_ref, kh, valid, scale)
            sink = _sink_rows(s_ref, kh)
            m = jnp.maximum(jnp.max(s, axis=-1, keepdims=True), sink)
            p = jnp.exp(s - m)
            den = jnp.sum(p, axis=-1, keepdims=True) + jnp.exp(sink - m)
            p = p * (1.0 / den)
            outs.append(_unstack_heads(jnp.dot(p.astype(BF16), vb, preferred_element_type=F32)))
            lse = m + jnp.log(den)
            for g in range(GQA_GROUP):
                lse_all = jnp.where(lane == kh * GQA_GROUP + g, lse[g * WINDOW:(g + 1) * WINDOW, :], lse_all)
        o_ref[...] = jnp.concatenate(outs, axis=1).astype(BF16)
        l_ref[...] = lse_all

    const2 = lambda shape: pl.BlockSpec(shape, lambda n: (0, 0))
    return pl.pallas_call(
        body, name=name, grid=(nb,),
        in_specs=[pl.BlockSpec((WINDOW, D), lambda n: (n, 0)),
                  pl.BlockSpec((WINDOW, kvw), lambda n: (n, kvb)),
                  pl.BlockSpec((WINDOW, kvw), lambda n: (jnp.maximum(n - 1, 0), kvb)),
                  pl.BlockSpec(bias.shape, lambda n: (0, 0, 0)),
                  const2((1, HEAD_DIM)), const2((1, HEAD_DIM)), const2((H, 1))],
        out_specs=[pl.BlockSpec((WINDOW, D), lambda n: (n, 0)), pl.BlockSpec((WINDOW, H), lambda n: (n, 0))],
        out_shape=[jax.ShapeDtypeStruct((S, D), BF16), jax.ShapeDtypeStruct((S, H), F32)],
        compiler_params=_cparams("parallel"),
    )(qkv, qkv, qkv, bias, qg, kg, sinks)


def _attn_bwd(name, qkv, do, lse, bias, qg, kg, sinks):
    S, QW = qkv.shape
    H = bias.shape[0]
    D = H * HEAD_DIM
    KV = H // GQA_GROUP
    kvw = QW - D
    kvb = D // kvw
    nb = S // WINDOW
    scale = HEAD_DIM ** -0.5

    def body(q_ref, kc_ref, kp_ref, do_ref, l_ref, b_ref, qg_ref, kg_ref, s_ref,
             dq_ref, dkv_ref, db_ref, ds_ref, dqg_ref, dkg_ref, carry):
        n = pl.program_id(0)

        @pl.when(n == 0)
        def _():
            db_ref[...] = jnp.zeros(db_ref.shape, F32)
            ds_ref[...] = jnp.zeros(ds_ref.shape, F32)
            dqg_ref[...] = jnp.zeros(dqg_ref.shape, F32)
            dkg_ref[...] = jnp.zeros(dkg_ref.shape, F32)
            carry[...] = jnp.zeros(carry.shape, F32)

        @pl.when(n == nb)
        def _():
            dkv_ref[...] = carry[...].astype(BF16)

        @pl.when(n < nb)
        def _():
            valid = _band_valid(n)
            q = q_ref[...]
            kvc = kc_ref[...]
            kvp = kp_ref[...]
            do_all = do_ref[...]
            lse = l_ref[...]
            qgain = qg_ref[...]
            kgain = kg_ref[...]
            dqs, dk_parts, dv_parts = [], [], []
            dqg = jnp.zeros((1, HEAD_DIM), F32)
            dkg = jnp.zeros((1, HEAD_DIM), F32)
            for kh in range(KV):
                ks = slice(kh * HEAD_DIM, (kh + 1) * HEAD_DIM)
                vs = slice((KV + kh) * HEAD_DIM, (KV + kh + 1) * HEAD_DIM)
                heads = slice(kh * GQA_GROUP, (kh + 1) * GQA_GROUP)
                kb = jnp.concatenate([kvp[:, ks], kvc[:, ks]], axis=0)
                vb = jnp.concatenate([kvp[:, vs], kvc[:, vs]], axis=0).astype(BF16)
                rk, khat, kn32 = _head_norm(kb, kgain)
                kn = kn32.astype(BF16)
                rq, qhat, qn32 = _head_norm(_stack_heads(q, kh), qgain)
                qn = qn32.astype(BF16)
                s = _band_scores(qn, kn, b_ref, kh, valid, scale)
                lse8 = jnp.concatenate([lse[:, kh * GQA_GROUP + g:kh * GQA_GROUP + g + 1]
                                        for g in range(GQA_GROUP)], axis=0)
                p = jnp.exp(s - lse8)
                do8 = _stack_heads(do_all, kh)
                dp = lax.dot_general(do8, vb, NT, preferred_element_type=F32)
                delta = jnp.sum(p * dp, axis=-1, keepdims=True)
                ds = p * (dp - delta)
                db_ref[heads] += ds.reshape(GQA_GROUP, WINDOW, 2 * WINDOW)
                psink = jnp.exp(_sink_rows(s_ref, kh) - lse8)
                ds_ref[heads, :] += -jnp.sum((psink * delta).reshape(GQA_GROUP, WINDOW, 1), axis=1)
                dsb = (ds * scale).astype(BF16)
                dqn = jnp.dot(dsb, kn, preferred_element_type=F32)
                dkn = lax.dot_general(dsb, qn, TN, preferred_element_type=F32)
                dv_parts.append(lax.dot_general(p.astype(BF16), do8, TN, preferred_element_type=F32))
                dqg = dqg + jnp.sum(dqn * qhat, axis=0, keepdims=True)
                dqh = dqn * qgain
                dqs.append(_unstack_heads(rq * (dqh - qhat * jnp.mean(dqh * qhat, axis=-1, keepdims=True))))
                dkg = dkg + jnp.sum(dkn * khat, axis=0, keepdims=True)
                dkh = dkn * kgain
                dk_parts.append(rk * (dkh - khat * jnp.mean(dkh * khat, axis=-1, keepdims=True)))
            dq_ref[...] = jnp.concatenate(dqs, axis=1).astype(BF16)
            dqg_ref[...] += dqg
            dkg_ref[...] += dkg
            dkv = jnp.concatenate(dk_parts + dv_parts, axis=1)
            dkv_ref[...] = (carry[...] + dkv[0:WINDOW, :]).astype(BF16)
            carry[...] = dkv[WINDOW:2 * WINDOW, :]

    cur = lambda n: jnp.minimum(n, nb - 1)
    const2 = lambda shape: pl.BlockSpec(shape, lambda n: (0, 0))
    return pl.pallas_call(
        body, name=name, grid=(nb + 1,),
        in_specs=[pl.BlockSpec((WINDOW, D), lambda n: (cur(n), 0)),
                  pl.BlockSpec((WINDOW, kvw), lambda n: (cur(n), kvb)),
                  pl.BlockSpec((WINDOW, kvw), lambda n: (jnp.maximum(cur(n) - 1, 0), kvb)),
                  pl.BlockSpec((WINDOW, D), lambda n: (cur(n), 0)),
                  pl.BlockSpec((WINDOW, H), lambda n: (cur(n), 0)),
                  pl.BlockSpec(bias.shape, lambda n: (0, 0, 0)),
                  const2((1, HEAD_DIM)), const2((1, HEAD_DIM)), const2((H, 1))],
        out_specs=[pl.BlockSpec((WINDOW, D), lambda n: (cur(n), 0)),
                   pl.BlockSpec((WINDOW, kvw), lambda n: (jnp.maximum(n - 1, 0), 0)),
                   pl.BlockSpec(bias.shape, lambda n: (0, 0, 0)),
                   const2((H, 1)), const2((1, HEAD_DIM)), const2((1, HEAD_DIM))],
        out_shape=[jax.ShapeDtypeStruct((S, D), BF16), jax.ShapeDtypeStruct((S, kvw), BF16),
                   jax.ShapeDtypeStruct(bias.shape, F32), jax.ShapeDtypeStruct((H, 1), F32),
                   jax.ShapeDtypeStruct((1, HEAD_DIM), F32), jax.ShapeDtypeStruct((1, HEAD_DIM), F32)],
        scratch_shapes=[pltpu.VMEM((WINDOW, kvw), F32)],
        compiler_params=_cparams("arbitrary"),
    )(qkv, qkv, qkv, do, lse, bias, qg, kg, sinks)


def _window_sums(src, bufs, lo, n_rows, step_sign, col_groups):
    out = []
    for g, cols in enumerate(col_groups):
        prev = src
        for level in range(g + 1):
            k = step_sign * (1 << level)
            cur = bufs[level]
            cur[pl.ds(lo, n_rows), cols] = prev[pl.ds(lo, n_rows), cols] + prev[pl.ds(lo + k, n_rows), cols]
            prev = cur
        out.append(prev)
    return out


def _pool_fwd(name, x, gain, wp, scale):
    S, D = x.shape
    G, C = wp.shape[0], wp.shape[1]
    tm = _tile(S, 256, POOL_HALO)
    hb = tm // POOL_HALO
    HL = POOL_HALO
    groups = [slice(g * C, (g + 1) * C) for g in range(G)]

    def body(x_ref, xh_ref, g_ref, w_ref, sc_ref, o_ref, d_ref, ext, p2, p4, p8, p16):
        i = pl.program_id(0)
        gain_v = g_ref[...]
        xt = x_ref[...]
        h = _head_norm(xt, gain_v)[2]
        hh = _head_norm(xh_ref[...], gain_v)[2]
        ext[pl.ds(0, HL), :] = jnp.where(i > 0, hh, 0.0)
        ext[pl.ds(HL, tm), :] = h
        bufs = (p2, p4, p8, p16)
        for b in bufs:
            b[pl.ds(0, 8), :] = jnp.zeros((8, D), F32)
        sums = _window_sums(ext, bufs, 8, tm + HL - 8, -1, groups)
        t = i * tm + lax.broadcasted_iota(jnp.int32, (tm, 1), 0)
        for g, cols in enumerate(groups):
            cnt = jnp.minimum(t + 1, POOL_WINDOWS[g]).astype(F32)
            d = sums[g][pl.ds(HL, tm), cols] / cnt - h[:, cols]
            db = d.astype(BF16)
            d_ref[:, cols] = db
            y = jnp.dot(db, w_ref[g], preferred_element_type=F32)
            o_ref[:, cols] = xt[:, cols] + y * sc_ref[:, cols]

    row = pl.BlockSpec((tm, D), lambda i: (i, 0))
    vec = pl.BlockSpec((1, D), lambda i: (0, 0))
    return pl.pallas_call(
        body, name=name, grid=(S // tm,),
        in_specs=[row, pl.BlockSpec((HL, D), lambda i: (jnp.maximum(i * hb - 1, 0), 0)), vec,
                  pl.BlockSpec(wp.shape, lambda i: (0, 0, 0)), vec],
        out_specs=[row, row],
        out_shape=[jax.ShapeDtypeStruct((S, D), F32), jax.ShapeDtypeStruct((S, D), BF16)],
        scratch_shapes=[pltpu.VMEM((tm + HL, D), F32) for _ in range(5)],
        compiler_params=_cparams("parallel"),
    )(x, x, gain, wp, scale)


def _pool_bwd(name, dx1, x, gain, wp, scale, dsave):
    S, D = x.shape
    G, C = wp.shape[0], wp.shape[1]
    tm = _tile(S, 256, POOL_HALO)
    hb = tm // POOL_HALO
    HL = POOL_HALO
    nI = S // tm
    groups = [slice(g * C, (g + 1) * C) for g in range(G)]

    def body(dx_ref, dxh_ref, x_ref, g_ref, w_ref, sc_ref, ds_ref, o_ref, ob_ref, dw_ref, dsc_ref, dg_ref,
             ext, p2, p4, p8, p16):
        i = pl.program_id(0)

        @pl.when(i == 0)
        def _():
            dw_ref[...] = jnp.zeros(dw_ref.shape, F32)
            dsc_ref[...] = jnp.zeros(dsc_ref.shape, F32)
            dg_ref[...] = jnp.zeros(dg_ref.shape, F32)

        dx1t = dx_ref[...]
        sc = sc_ref[...]
        dys = (dx1t * sc).astype(BF16)
        dys_h = (dxh_ref[...] * sc).astype(BF16)
        t = i * tm + lax.broadcasted_iota(jnp.int32, (tm, 1), 0)
        th = (i + 1) * tm + lax.broadcasted_iota(jnp.int32, (HL, 1), 0)
        dds = []
        for g, cols in enumerate(groups):
            dsv = ds_ref[:, cols]
            y = jnp.dot(dsv, w_ref[g], preferred_element_type=F32)
            dsc_ref[:, cols] += jnp.sum(dx1t[:, cols] * y, axis=0, keepdims=True)
            dw_ref[g] += lax.dot_general(dsv, dys[:, cols], TN, preferred_element_type=F32)
            dd = lax.dot_general(dys[:, cols], w_ref[g], NT, preferred_element_type=F32)
            dd_h = lax.dot_general(dys_h[:, cols], w_ref[g], NT, preferred_element_type=F32)
            dds.append(dd)
            w = POOL_WINDOWS[g]
            ext[pl.ds(0, tm), cols] = dd / jnp.minimum(t + 1, w).astype(F32)
            e_h = dd_h / jnp.minimum(th + 1, w).astype(F32)
            ext[pl.ds(tm, HL), cols] = jnp.where(i < nI - 1, e_h, 0.0)
        bufs = (p2, p4, p8, p16)
        for b in bufs:
            b[pl.ds(tm + HL - 8, 8), :] = jnp.zeros((8, D), F32)
        sums = _window_sums(ext, bufs, 0, tm + HL - 8, 1, groups)
        dh = jnp.concatenate([sums[g][pl.ds(0, tm), cols] - dds[g] for g, cols in enumerate(groups)], axis=1)
        dx, dg = _rms_bwd_math(dh, x_ref[...], g_ref[...])
        dx = dx1t + dx
        o_ref[...] = dx
        ob_ref[...] = dx.astype(BF16)
        dg_ref[...] += dg

    row = pl.BlockSpec((tm, D), lambda i: (i, 0))
    vec = pl.BlockSpec((1, D), lambda i: (0, 0))
    last_h = S // HL - 1
    return pl.pallas_call(
        body, name=name, grid=(nI,),
        in_specs=[row, pl.BlockSpec((HL, D), lambda i: (jnp.minimum((i + 1) * hb, last_h), 0)), row, vec,
                  pl.BlockSpec(wp.shape, lambda i: (0, 0, 0)), vec, row],
        out_specs=[row, row, pl.BlockSpec(wp.shape, lambda i: (0, 0, 0)), vec, vec],
        out_shape=[jax.ShapeDtypeStruct((S, D), F32), jax.ShapeDtypeStruct((S, D), BF16),
                   jax.ShapeDtypeStruct(wp.shape, F32),
                   jax.ShapeDtypeStruct((1, D), F32), jax.ShapeDtypeStruct((1, D), F32)],
        scratch_shapes=[pltpu.VMEM((tm + HL, D), F32) for _ in range(5)],
        compiler_params=_cparams("arbitrary"),
    )(dx1, dx1, x, gain, wp, scale, dsave)


HBM_SPEC = pl.BlockSpec(memory_space=pltpu.HBM)


def _coords():
    return lax.axis_index("x"), lax.axis_index("y"), lax.axis_index("c")


class _AgItem:
    def __init__(self, src, out_struct, slot, half):
        self.src, self.out_struct, self.slot, self.half = src, out_struct, slot, half


def _rows_item(src, n_rows_total):
    r, ncol = src.shape
    return _AgItem(src, jax.ShapeDtypeStruct((n_rows_total, ncol), src.dtype),
                   lambda out, d: out.at[pl.ds(pl.multiple_of(d * r, 16), r), :],
                   lambda ref, h: ref.at[pl.ds(h * (r // 2), r // 2), :])


def _cols_item(src, n_cols_total):
    nrow, cb = src.shape
    return _AgItem(src, jax.ShapeDtypeStruct((nrow, n_cols_total), src.dtype),
                   lambda out, d: out.at[:, pl.ds(pl.multiple_of(d * cb, LANES), cb)],
                   lambda ref, h: ref.at[pl.ds(h * (nrow // 2), nrow // 2), :])


def _lead_item(src):
    return _AgItem(src, jax.ShapeDtypeStruct((NDEV,) + src.shape, src.dtype),
                   lambda out, d: out.at[d],
                   lambda ref, h: ref.at[pl.ds(h * (src.shape[0] // 2), src.shape[0] // 2)])


def _pool_item(src, c_total):
    g, pc, c = src.shape
    return _AgItem(src, jax.ShapeDtypeStruct((g, c_total, c), src.dtype),
                   lambda out, d: out.at[:, pl.ds(pl.multiple_of(d * pc, 16), pc), :],
                   lambda ref, h: ref.at[pl.ds(h * (g // 2), g // 2)])


class _AllGather:
    N_PHASES = 4

    def __init__(self, items):
        self.items = list(items)

    def srcs(self):
        return [it.src for it in self.items]

    def out_structs(self):
        return [it.out_struct for it in self.items]

    def in_specs(self):
        return [HBM_SPEC] * len(self.items)

    def out_specs(self):
        return [HBM_SPEC] * len(self.items)

    def scratch(self):
        n = len(self.items)
        return [pltpu.SemaphoreType.DMA((8 * n,)), pltpu.SemaphoreType.DMA((8 * n,)), pltpu.SemaphoreType.DMA((n,))]

    def phase(self, ph, ins, outs, send_sems, recv_sems, local_sems):
        x, y, c = _coords()
        me, xn = 4 * x + 2 * y + c, 4 * (1 - x) + 2 * y + c
        yn, dg = 4 * x + 2 * (1 - y) + c, 4 * (1 - x) + 2 * (1 - y) + c
        XN, YN, SB = (1 - x, y, c), (x, 1 - y, c), (x, y, 1 - c)
        sib = lambda blk: blk + 1 - 2 * c
        for o, it in enumerate(self.items):
            slot = lambda d, it=it, o=o: it.slot(outs[o], d)
            half = it.half
            table = [
                (ins[o], slot(me), XN, slot(xn)),
                (ins[o], slot(me), YN, slot(yn)),
                (half(slot(xn), 0), half(slot(xn), 0), YN, half(slot(dg), 0)),
                (half(slot(yn), 1), half(slot(yn), 1), XN, half(slot(dg), 1)),
                (ins[o], slot(me), SB, slot(sib(me))),
                (slot(xn), slot(xn), SB, slot(sib(xn))),
                (slot(yn), slot(yn), SB, slot(sib(yn))),
                (slot(dg), slot(dg), SB, slot(sib(dg))),
            ]

            def send(k, table=table, o=o):
                src, dst, peer, _ = table[k]
                return pltpu.make_async_remote_copy(src_ref=src, dst_ref=dst, send_sem=send_sems.at[8 * o + k],
                                                    recv_sem=recv_sems.at[8 * o + k], device_id=peer, device_id_type=MESH)

            def arrived(k, table=table, o=o):
                land = table[k][3]
                pltpu.make_async_remote_copy(src_ref=land, dst_ref=land, send_sem=send_sems.at[8 * o + k],
                                             recv_sem=recv_sems.at[8 * o + k], device_id=table[k][2],
                                             device_id_type=MESH).wait_recv()

            local = pltpu.make_async_copy(ins[o], slot(me), local_sems.at[o])
            if ph == 0:
                local.start()
                for k in (0, 1, 4):
                    send(k).start()
            elif ph == 1:
                arrived(0)
                send(2).start()
                send(5).start()
                arrived(1)
                send(3).start()
                send(6).start()
            elif ph == 2:
                arrived(2)
                arrived(3)
                send(7).start()
            else:
                for k in (4, 5, 6, 7):
                    arrived(k)
                for k in range(8):
                    send(k).wait_send()
                local.wait()

    def run_at(self, step, schedule, ins, outs, send_sems, recv_sems, local_sems):
        for ph in range(self.N_PHASES):
            @pl.when(step == schedule[ph])
            def _(ph=ph):
                self.phase(ph, ins, outs, send_sems, recv_sems, local_sems)


def _allgather_now(name, ag):
    n = len(ag.items)

    def body(*refs):
        for ph in range(ag.N_PHASES):
            ag.phase(ph, refs[:n], refs[n:2 * n], *refs[2 * n:])

    return pl.pallas_call(body, name=name, in_specs=ag.in_specs(), out_specs=ag.out_specs(),
                          out_shape=ag.out_structs(), scratch_shapes=ag.scratch())(*ag.srcs())


def _allgather_small(name, block):
    m_per, ncol = block.shape

    def body(x_ref, out_ref, send_sems, recv_sems, local_sem):
        x, y, c = _coords()
        me, sibling = (x, y, c), (x, y, 1 - c)
        chips = [(1 - x, y), (x, 1 - y), (1 - x, 1 - y)]

        def rows(px, py, pc):
            return out_ref.at[pl.ds((4 * px + 2 * py + pc) * m_per, m_per), :]

        def copy(k, block_of, to, src=None):
            return pltpu.make_async_remote_copy(
                src_ref=rows(*block_of) if src is None else src, dst_ref=rows(*block_of),
                send_sem=send_sems.at[k], recv_sem=recv_sems.at[k], device_id=to, device_id_type=MESH)

        mine = pltpu.make_async_copy(x_ref, rows(*me), local_sem)
        mine.start()
        first = [copy(0, me, sibling, src=x_ref)]
        first += [copy(1 + j, me, (*chip, c), src=x_ref) for j, chip in enumerate(chips)]
        for cp in first:
            cp.start()
        passed = [copy(4 + j, (*chip, c), sibling) for j, chip in enumerate(chips)]
        for j, chip in enumerate(chips):
            copy(1 + j, (*chip, c), me).wait_recv()
            passed[j].start()
        copy(0, sibling, me).wait_recv()
        for j, chip in enumerate(chips):
            copy(4 + j, (*chip, 1 - c), me).wait_recv()
        for cp in first + passed:
            cp.wait_send()
        mine.wait()

    return pl.pallas_call(
        body, name=name, out_shape=jax.ShapeDtypeStruct((NDEV * m_per, ncol), block.dtype),
        in_specs=[pl.BlockSpec(memory_space=pltpu.VMEM)], out_specs=pl.BlockSpec(memory_space=pltpu.VMEM),
        scratch_shapes=[pltpu.SemaphoreType.DMA((7,)), pltpu.SemaphoreType.DMA((7,)), pltpu.SemaphoreType.DMA],
        compiler_params=pltpu.CompilerParams(vmem_limit_bytes=VMEM_LIMIT),
    )(block)


def _exchange_now(name, ex):
    n_in, n_out = len(ex.srcs), len(ex.out_structs)

    def body(*refs):
        ex.run(refs[:n_in], refs[n_in:n_in + n_out], refs[n_in + n_out], refs[n_in + n_out + 1], True, True)

    return pl.pallas_call(body, name=name, in_specs=ex.in_specs(), out_specs=ex.out_specs(),
                          out_shape=ex.out_structs, scratch_shapes=ex.scratch())(*ex.srcs)


def _rs_stage_c(gs):
    def plan(x, y, c, ins, outs):
        sib = (x, y, 1 - c)
        return [(g.at[q, 1 - c], r.at[q], sib) for g, r in zip(ins, outs) for q in range(4)]

    outs = [jax.ShapeDtypeStruct((4,) + g.shape[2:], g.dtype) for g in gs]
    return _Exchange(gs, outs, plan, 4 * len(gs))


def _rs_stage_ici(sends, first):
    def plan(x, y, c, ins, outs):
        XN, YN = (1 - x, y, c), (x, 1 - y, c)
        peers = (YN, XN) if first else (XN, YN)
        return [(s.at[h], r.at[h], peers[h]) for s, r in zip(ins, outs) for h in range(2)]

    outs = [jax.ShapeDtypeStruct(s.shape, s.dtype) for s in sends]
    return _Exchange(sends, outs, plan, 2 * len(sends))


def _coord_vec():
    x, y, c = _coords()
    return jnp.stack([x, y, c]).astype(jnp.int32)


def _rs_add1(name, g, r1, coords):
    R, L = g.shape[3], g.shape[4]
    tr = _tile(R, 512, 16)

    def qk(h, idx, cr):
        return jnp.where(h == 0, 2 * idx + cr[1], 2 * cr[0] + idx)

    def qs(h, idx, cr):
        return jnp.where(h == 0, 2 * idx + 1 - cr[1], 2 * (1 - cr[0]) + idx)

    def body(cr, gk, rk, gsd, rsd, keep, send):
        keep[...] = gk[...] + rk[...]
        send[...] = (gsd[...] + rsd[...]).astype(BF16)

    gspec = lambda qf: pl.BlockSpec((None, None, None, tr, L), lambda h, idx, r, cr: (qf(h, idx, cr), cr[2], h, r, 0))
    rspec = lambda qf: pl.BlockSpec((None, None, tr, L), lambda h, idx, r, cr: (qf(h, idx, cr), h, r, 0))
    ospec = pl.BlockSpec((None, None, tr, L), lambda h, idx, r, cr: (h, idx, r, 0))
    return pl.pallas_call(
        body, name=name,
        grid_spec=pltpu.PrefetchScalarGridSpec(
            num_scalar_prefetch=1, grid=(2, 2, R // tr),
            in_specs=[gspec(qk), rspec(qk), gspec(qs), rspec(qs)], out_specs=[ospec, ospec]),
        out_shape=[jax.ShapeDtypeStruct((2, 2, R, L), F32), jax.ShapeDtypeStruct((2, 2, R, L), BF16)],
        compiler_params=_cparams("parallel", "parallel", "parallel"),
    )(coords, g, r1, g, r1)


def _rs_add2(name, keep2, recv2, coords):
    R, L = keep2.shape[2], keep2.shape[3]
    tr = _tile(R, 512, 16)

    def mine(h, cr):
        return jnp.where(h == 0, cr[0], cr[1])

    def body(cr, kk, rk, ks, rs, keep, send):
        keep[...] = kk[...] + rk[...].astype(F32)
        send[...] = (ks[...] + rs[...].astype(F32)).astype(BF16)

    sel = lambda f: pl.BlockSpec((None, None, tr, L), lambda h, r, cr: (h, f(h, cr), r, 0))
    ospec = pl.BlockSpec((None, tr, L), lambda h, r, cr: (h, r, 0))
    other = lambda h, cr: 1 - mine(h, cr)
    return pl.pallas_call(
        body, name=name,
        grid_spec=pltpu.PrefetchScalarGridSpec(
            num_scalar_prefetch=1, grid=(2, R // tr),
            in_specs=[sel(mine), sel(mine), sel(other), sel(other)], out_specs=[ospec, ospec]),
        out_shape=[jax.ShapeDtypeStruct((2, R, L), F32), jax.ShapeDtypeStruct((2, R, L), BF16)],
        compiler_params=_cparams("parallel", "parallel"),
    )(coords, keep2, recv2, keep2, recv2)


def _rs_add3(name, keep3, recv3):
    R, L = keep3.shape[1], keep3.shape[2]
    tr = _tile(R, 512, 16)

    def body(k, r, o):
        o[...] = k[...] + r[...].astype(F32)

    spec = pl.BlockSpec((None, tr, L), lambda h, r: (h, r, 0))
    return pl.pallas_call(body, name=name, grid=(2, R // tr), in_specs=[spec, spec], out_specs=spec,
                          out_shape=jax.ShapeDtypeStruct((2, R, L), F32),
                          compiler_params=_cparams("parallel", "parallel"))(keep3, recv3)


class _ReduceScatter:
    N_STAGES = 3

    def __init__(self, tag, gs, coords):
        self.tag, self.coords, self.stage, self.result = tag, coords, 0, None
        self.full = []
        for g in gs:
            per = int(np.prod(g.shape[1:]))
            L = g.shape[-1]
            self.full.append(g.reshape(4, 2, 2, per // (2 * L), L))
        self.keep, self.send = None, None

    def exchange(self):
        if self.stage == 0:
            return _rs_stage_c(self.full)
        return _rs_stage_ici(list(self.send), self.stage == 1)

    def absorb(self, recv):
        names = [f"rs_add{self.stage + 1}_{self.tag}_{k}" for k in range(len(self.full))]
        if self.stage == 0:
            pairs = [_rs_add1(nm, g, r, self.coords) for nm, g, r in zip(names, self.full, recv)]
            self.keep, self.send = zip(*pairs)
        elif self.stage == 1:
            pairs = [_rs_add2(nm, kp, r, self.coords) for nm, kp, r in zip(names, self.keep, recv)]
            self.keep, self.send = zip(*pairs)
        else:
            self.result = [_rs_add3(nm, kp, r) for nm, kp, r in zip(names, self.keep, recv)]
        self.stage += 1

    def finish_now(self):
        while self.stage < self.N_STAGES:
            self.absorb(_exchange_now(f"rs_x{self.stage}_{self.tag}", self.exchange()))
        return self.result


def _adamw(name, w, g, m, v):
    R, L = w.shape
    tr = _tile(R, 256, 8)

    def body(w_ref, g_ref, m_ref, v_ref, d_ref, nm_ref, nv_ref):
        gv = g_ref[...]
        nm = ADAM_B1 * m_ref[...] + (1.0 - ADAM_B1) * gv
        nv = ADAM_B2 * v_ref[...] + (1.0 - ADAM_B2) * (gv * gv)
        m_hat = nm / (1.0 - ADAM_B1 ** ADAM_STEP)
        v_hat = nv / (1.0 - ADAM_B2 ** ADAM_STEP)
        d_ref[...] = -ADAM_LR * (m_hat / (jnp.sqrt(v_hat) + ADAM_EPS) + ADAM_WD * w_ref[...])
        nm_ref[...] = nm
        nv_ref[...] = nv

    spec = pl.BlockSpec((tr, L), lambda i: (i, 0))
    out = jax.ShapeDtypeStruct((R, L), F32)
    return pl.pallas_call(body, name=name, grid=(R // tr,), in_specs=[spec] * 4, out_specs=[spec] * 3,
                          out_shape=[out, out, out], compiler_params=_cparams("parallel"))(w, g, m, v)


def _sum_devices(name, gathered):
    _, R, L = gathered.shape
    tr = _tile(R, 512, 8)

    def body(g_ref, o_ref):
        acc = g_ref[0]
        for d in range(1, NDEV):
            acc = acc + g_ref[d]
        o_ref[...] = acc

    return pl.pallas_call(body, name=name, grid=(R // tr,),
                          in_specs=[pl.BlockSpec((NDEV, tr, L), lambda i: (0, i, 0))],
                          out_specs=pl.BlockSpec((tr, L), lambda i: (i, 0)),
                          out_shape=jax.ShapeDtypeStruct((R, L), F32),
                          compiler_params=_cparams("parallel"))(gathered)


def _pack(parts):
    flat, offs, pos = [], [], 0
    for p in parts:
        n = int(np.prod(p.shape))
        padded = -(-n // PACK_ALIGN) * PACK_ALIGN
        flat.append(jnp.pad(p.reshape(-1).astype(F32), (0, padded - n)))
        offs.append((pos, n, p.shape))
        pos += padded
    return jnp.concatenate(flat).reshape(-1, LANES), offs


def _unpack(packed, offs):
    flat = packed.reshape(-1)
    return [flat[pos:pos + n].reshape(shape) for pos, n, shape in offs]


def kernel(x, norm_mix, norm_ffn, rel_bias, attn_w_qkv, attn_q_gain, attn_k_gain, attn_sinks, attn_w_o, pool_w, pool_scale, ffn_w_up, ffn_conv_w, ffn_conv_b, ffn_w_down, loss_target, m_norm_mix, m_norm_ffn, m_rel_bias, m_attn_w_qkv, m_attn_q_gain, m_attn_k_gain, m_attn_sinks, m_attn_w_o, m_pool_w, m_pool_scale, m_ffn_w_up, m_ffn_conv_w, m_ffn_conv_b, m_ffn_w_down, v_norm_mix, v_norm_ffn, v_rel_bias, v_attn_w_qkv, v_attn_q_gain, v_attn_k_gain, v_attn_sinks, v_attn_w_o, v_pool_w, v_pool_scale, v_ffn_w_up, v_ffn_conv_w, v_ffn_conv_b, v_ffn_w_down):
    xs = x[0]
    target = loss_target[0]
    S, D = xs.shape
    depth = norm_mix.shape[0]
    H = D // HEAD_DIM
    n_attn, n_pool = attn_w_qkv.shape[0], pool_w.shape[0]
    QS = attn_w_qkv.shape[2]
    CB = ffn_w_up.shape[2]
    FB = ffn_w_down.shape[1]
    FF = FB * NDEV
    G, PC, C = pool_w.shape[1], pool_w.shape[2], pool_w.shape[3]
    xi, yi, ci = _coords()
    me = 4 * xi + 2 * yi + ci
    coords = _coord_vec()

    def layer_allgather(i):
        j = i // 2
        items = [_cols_item(ffn_w_up[i].astype(BF16), NDEV * CB), _rows_item(ffn_w_down[i].astype(BF16), FF)]
        if i % 2 == 0:
            items += [_lead_item(attn_w_qkv[j].astype(BF16)), _rows_item(attn_w_o[j].astype(BF16), D)]
        else:
            items += [_pool_item(pool_w[j].astype(BF16), C)]
        return _AllGather(items)

    def unpack_weights(i, outs):
        w = {"up": outs[0], "down": outs[1]}
        if i % 2 == 0:
            w["qkv"] = outs[2].transpose(1, 0, 2).reshape(D, NDEV * QS)
            w["o"] = outs[3]
        else:
            w["pool"] = outs[2]
        return w

    weights_of = [None] * depth
    weights_of[0] = unpack_weights(0, _allgather_now("allgather_weights_0", layer_allgather(0)))

    small_in, small_in_offs = _pack([ffn_conv_w, pool_scale])
    gathered_in = _allgather_small("allgather_small_params", small_in).reshape(NDEV, -1)
    per_dev = [_unpack(gathered_in[d], small_in_offs) for d in range(NDEV)]
    conv_w_full = jnp.concatenate([p[0] for p in per_dev], axis=2)
    pool_scale_full = jnp.concatenate([p[1] for p in per_dev], axis=1)

    onehot = jnp.asarray(_t5_onehot())
    bias = _bias_band(rel_bias, onehot).reshape(H, WINDOW, 2 * WINDOW)

    saved = []
    cur = xs
    for i in range(depth):
        j = i // 2
        w = weights_of[i]
        st = {"x0": cur}
        if i % 2 == 0:
            h = _rmsnorm_fwd(f"norm_mix_{i}", cur, norm_mix[i:i + 1])
            qkv = _mm_rows(f"qkv_{i}", h, w["qkv"], NN, F32)
            qg, kg, sk = attn_q_gain[j:j + 1], attn_k_gain[j:j + 1], attn_sinks[j].reshape(H, 1)
            o, lse = _attn_fwd(f"attn_fwd_{i}", qkv, bias, qg, kg, sk)
            x1 = _mm_rows(f"attn_out_{i}", o, w["o"], NN, F32, res=cur)
            st.update(h=h, qkv=qkv, o=o, lse=lse)
        else:
            x1, dsave = _pool_fwd(f"pool_fwd_{i}", cur, norm_mix[i:i + 1], w["pool"], pool_scale_full[j:j + 1])
            st.update(dsave=dsave)
        h2 = _rmsnorm_fwd(f"norm_ffn_{i}", x1, norm_ffn[i:i + 1])
        ag = layer_allgather(i + 1) if i + 1 < depth else None
        u3, uc3, a, *gathered_next = _ffn_up_fwd(f"ffn_up_{i}", h2, w["up"], conv_w_full[i], ffn_conv_b[i:i + 1], ag)
        if ag:
            weights_of[i + 1] = unpack_weights(i + 1, gathered_next)
        cur = _ffn_down_fwd(f"ffn_down_{i}", a, w["down"], x1)
        st.update(x1=x1, h2=h2, u3=u3, uc3=uc3, a=a)
        saved.append(st)

    loss_tile, dcur, dcur_b = _loss_fwd_bwd(cur, target)

    g_up_l, g_down_l = [None] * depth, [None] * depth
    g_qkv_l, g_o_l, g_pool_l = [None] * n_attn, [None] * n_attn, [None] * n_pool
    d_norm_mix, d_norm_ffn = [None] * depth, [None] * depth
    d_conv_w, d_conv_b = [None] * depth, [None] * depth
    d_qg, d_kg, d_sinks, d_pscale = [None] * n_attn, [None] * n_attn, [None] * n_attn, [None] * n_pool
    d_band = None

    def store_reduced(i, red):
        g_up_l[i] = red[0].reshape(D, CB)
        g_down_l[i] = red[1].reshape(FB, D)
        if i % 2 == 0:
            g_qkv_l[i // 2] = red[2].reshape(D, QS)
            g_o_l[i // 2] = red[3].reshape(D // NDEV, D)
        else:
            g_pool_l[i // 2] = red[2].reshape(G, PC, C)

    pending = None

    def with_exchange(call):
        if pending is None or pending.stage >= pending.N_STAGES:
            return call(None)
        out, *recv = call(pending.exchange())
        pending.absorb(recv)
        return out

    for i in reversed(range(depth)):
        j = i // 2
        st = saved[i]
        w = weights_of[i]
        du3, dc = _ffn_da_bwd(f"ffn_da_{i}", dcur_b, w["down"], st["u3"], st["uc3"], conv_w_full[i])
        d_conv_w[i] = jnp.concatenate([dc[0, 0:3], dc[1, 0:3]], axis=1)
        d_conv_b[i] = jnp.concatenate([dc[0, 3], dc[1, 3]], axis=0)
        dwdown = with_exchange(lambda ex: _mm_tn(f"ffn_dwdown_{i}", st["a"], dcur_b,
                                                 tm_pref=CB if CB % LANES == 0 else 1024, comm=ex))
        dwdown = dwdown.reshape(NDEV, FB, D)
        dwup = with_exchange(lambda ex: _ffn_dwup_bwd(f"ffn_dwup_{i}", st["h2"], du3, comm=ex))
        dh2 = with_exchange(lambda ex: _ffn_dh_bwd(f"ffn_dh_{i}", du3, w["up"], comm=ex))
        if pending is not None:
            store_reduced(i + 1, pending.finish_now())
        dx1, dx1_b, dg = _rmsnorm_bwd(f"norm_ffn_bwd_{i}", dh2, st["x1"], norm_ffn[i:i + 1], dcur)
        d_norm_ffn[i] = dg[0]
        grads = [dwup, dwdown]
        if i % 2 == 0:
            do = _mm_rows(f"attn_do_{i}", dx1_b, w["o"], NT, BF16)
            dwo = _mm_tn(f"attn_dwo_{i}", st["o"], dx1_b).reshape(NDEV, D // NDEV, D)
            qg, kg, sk = attn_q_gain[j:j + 1], attn_k_gain[j:j + 1], attn_sinks[j].reshape(H, 1)
            dq, dkv, db, dsk, dqg, dkg = _attn_bwd(f"attn_bwd_{i}", st["qkv"], do, st["lse"], bias, qg, kg, sk)
            d_band = db if d_band is None else d_band + db
            d_sinks[j], d_qg[j], d_kg[j] = dsk[:, 0], dqg[0], dkg[0]
            dqkv = jnp.concatenate([dq, dkv], axis=1)
            dwqkv = _mm_tn(f"attn_dwqkv_{i}", st["h"], dqkv, tn_pref=1280)
            dwqkv = dwqkv.reshape(D, NDEV, QS).transpose(1, 0, 2)
            dh = _mm_rows(f"attn_dh_{i}", dqkv, w["qkv"], NT, F32)
            dcur, dcur_b, dg = _rmsnorm_bwd(f"norm_mix_bwd_{i}", dh, st["x0"], norm_mix[i:i + 1], dx1)
            grads += [dwqkv, dwo]
        else:
            dcur, dcur_b, dwp, dps, dg = _pool_bwd(f"pool_bwd_{i}", dx1, st["x0"], norm_mix[i:i + 1], w["pool"],
                                                   pool_scale_full[j:j + 1], st["dsave"])
            d_pscale[j] = dps[0]
            grads += [dwp.reshape(G, NDEV, PC, C).transpose(1, 0, 2, 3)]
        d_norm_mix[i] = dg[0]
        pending = _ReduceScatter(f"l{i}", grads, coords)
    store_reduced(0, pending.finish_now())

    d_rel = _bias_band_bwd(d_band.reshape(H, -1), onehot)

    small_parts = [loss_tile, jnp.stack(d_norm_mix), jnp.stack(d_norm_ffn), d_rel, jnp.stack(d_qg),
                   jnp.stack(d_kg), jnp.stack(d_sinks), jnp.stack(d_pscale), jnp.stack(d_conv_w),
                   jnp.stack(d_conv_b)]
    small, small_offs = _pack(small_parts)
    gathered = _allgather_small("allgather_small_grads", small).reshape(NDEV, -1, LANES)
    summed = _unpack(_sum_devices("sum_small_grads", gathered), small_offs)
    loss = summed[0][0, 0]
    (g_norm_mix, g_norm_ffn, g_rel, g_qg, g_kg, g_sinks, g_pscale_full, g_conv_w_full, g_conv_b) = summed[1:]
    g_pscale = lax.dynamic_slice_in_dim(g_pscale_full, me * (D // NDEV), D // NDEV, axis=1)
    g_conv_w = lax.dynamic_slice_in_dim(g_conv_w_full, me * CB, CB, axis=2)

    grads = {
        "norm_mix": g_norm_mix, "norm_ffn": g_norm_ffn, "rel_bias": g_rel, "attn_w_qkv": jnp.stack(g_qkv_l),
        "attn_q_gain": g_qg, "attn_k_gain": g_kg, "attn_sinks": g_sinks, "attn_w_o": jnp.stack(g_o_l),
        "pool_w": jnp.stack(g_pool_l), "pool_scale": g_pscale, "ffn_w_up": jnp.stack(g_up_l),
        "ffn_conv_w": g_conv_w, "ffn_conv_b": g_conv_b, "ffn_w_down": jnp.stack(g_down_l),
    }
    weights = {
        "norm_mix": (norm_mix, m_norm_mix, v_norm_mix), "norm_ffn": (norm_ffn, m_norm_ffn, v_norm_ffn),
        "rel_bias": (rel_bias, m_rel_bias, v_rel_bias), "attn_w_qkv": (attn_w_qkv, m_attn_w_qkv, v_attn_w_qkv),
        "attn_q_gain": (attn_q_gain, m_attn_q_gain, v_attn_q_gain),
        "attn_k_gain": (attn_k_gain, m_attn_k_gain, v_attn_k_gain),
        "attn_sinks": (attn_sinks, m_attn_sinks, v_attn_sinks), "attn_w_o": (attn_w_o, m_attn_w_o, v_attn_w_o),
        "pool_w": (pool_w, m_pool_w, v_pool_w), "pool_scale": (pool_scale, m_pool_scale, v_pool_scale),
        "ffn_w_up": (ffn_w_up, m_ffn_w_up, v_ffn_w_up), "ffn_conv_w": (ffn_conv_w, m_ffn_conv_w, v_ffn_conv_w),
        "ffn_conv_b": (ffn_conv_b, m_ffn_conv_b, v_ffn_conv_b), "ffn_w_down": (ffn_w_down, m_ffn_w_down, v_ffn_w_down),
    }
    names = list(weights)
    big = ("attn_w_qkv", "attn_w_o", "pool_w", "ffn_w_up", "ffn_w_down")
    upd = {}
    for nm in big:
        w, m, v = weights[nm]
        two_d = lambda t: t.reshape(-1, w.shape[-1])
        d_, m_, v_ = _adamw(f"adamw_{nm}", two_d(w), two_d(grads[nm]), two_d(m), two_d(v))
        upd[nm] = (d_.reshape(w.shape), m_.reshape(w.shape), v_.reshape(w.shape))
    small_names = [nm for nm in names if nm not in big]
    pw, offs = _pack([weights[nm][0] for nm in small_names])
    pg, _ = _pack([grads[nm] for nm in small_names])
    pm, _ = _pack([weights[nm][1] for nm in small_names])
    pv, _ = _pack([weights[nm][2] for nm in small_names])
    d_, m_, v_ = _adamw("adamw_small", pw, pg, pm, pv)
    for nm, dd, mm, vv in zip(small_names, _unpack(d_, offs), _unpack(m_, offs), _unpack(v_, offs)):
        upd[nm] = (dd, mm, vv)

    grad_x = dcur[None]
    return (loss, grad_x, *[grads[nm].reshape(weights[nm][0].shape) for nm in names],
            *[upd[nm][0] for nm in names], *[upd[nm][1] for nm in names], *[upd[nm][2] for nm in names])
```

```python
import numpy as np
import jax
import jax.numpy as jnp
from jax import lax
from jax.experimental import pallas as pl
from jax.experimental.pallas import tpu as pltpu

F32 = jnp.float32
BF16 = jnp.bfloat16
MESH = pl.DeviceIdType.MESH

NDEV = 8
HEAD_DIM = 64
GQA_GROUP = 8
WINDOW = 128
N_BUCKETS = 32
MAX_DISTANCE = 128
POOL_WINDOWS = (2, 4, 8, 16)
POOL_HALO = 32
EPS = 1e-6
NEG_INF = -1e30
ADAM_LR = 0.001
ADAM_B1 = 0.9
ADAM_B2 = 0.999
ADAM_EPS = 1e-08
ADAM_WD = 0.01
ADAM_STEP = 10

V7X_VMEM_BYTES = 64 * 1024 * 1024
VMEM_LIMIT = V7X_VMEM_BYTES - 8 * 1024 * 1024
LANES = 128
SUBLANES = 8
PACK_ALIGN = SUBLANES * LANES

NN = (((1,), (0,)), ((), ()))
NT = (((1,), (1,)), ((), ()))
TN = (((0,), (0,)), ((), ()))


def _tile(dim, pref, align):
    t = min(pref, dim)
    t -= t % align
    while t >= align:
        if dim % t == 0:
            return t
        t -= align
    return dim


def _cparams(*sem):
    return pltpu.CompilerParams(dimension_semantics=sem, vmem_limit_bytes=VMEM_LIMIT)


def _bf(v):
    return v if v.dtype == BF16 else v.astype(BF16)


class _Exchange:
    def __init__(self, srcs, out_structs, plan, n_copies):
        self.srcs, self.out_structs, self.plan, self.n_copies = list(srcs), list(out_structs), plan, n_copies

    def in_specs(self):
        return [HBM_SPEC] * len(self.srcs)

    def out_specs(self):
        return [HBM_SPEC] * len(self.out_structs)

    def scratch(self):
        return [pltpu.SemaphoreType.DMA((self.n_copies,)), pltpu.SemaphoreType.DMA((self.n_copies,))]

    def run(self, in_refs, out_refs, send_sems, recv_sems, is_first, is_last):
        def copies():
            x, y, c = _coords()
            return [pltpu.make_async_remote_copy(src_ref=src, dst_ref=dst, send_sem=send_sems.at[k],
                                                 recv_sem=recv_sems.at[k], device_id=peer, device_id_type=MESH)
                    for k, (src, dst, peer) in enumerate(self.plan(x, y, c, in_refs, out_refs))]

        def start():
            for cp in copies():
                cp.start()

        def finish():
            cps = copies()
            for cp in cps:
                cp.wait_recv()
            for cp in cps:
                cp.wait_send()

        if is_first is True and is_last is True:
            start()
            finish()
        else:
            pl.when(is_first)(start)
            pl.when(is_last)(finish)


def _grid_edges(grid):
    first, last = True, True
    for ax, n in enumerate(grid):
        first = jnp.logical_and(first, pl.program_id(ax) == 0)
        last = jnp.logical_and(last, pl.program_id(ax) == n - 1)
    return first, last


def _mm(name, a, b, *, grid, a_spec, b_spec, o_spec, out_shape, contract, acc_shape, res=None, comm=None):
    nk = grid[2]
    n_main = 3 if res is not None else 2
    n_ci = len(comm.srcs) if comm else 0
    n_co = len(comm.out_structs) if comm else 0

    def body(*refs):
        a_ref, b_ref = refs[:2]
        r_ref = refs[2] if res is not None else None
        o_ref = refs[n_main + n_ci]
        scr = refs[n_main + n_ci + 1 + n_co:]
        if comm:
            first, last = _grid_edges(grid)
            comm.run(refs[n_main:n_main + n_ci], refs[n_main + n_ci + 1:n_main + n_ci + 1 + n_co],
                     scr[-2], scr[-1], first, last)
        part = lax.dot_general(_bf(a_ref[...]), _bf(b_ref[...]), contract, preferred_element_type=F32)

        def finish(acc):
            if r_ref is not None:
                acc = acc + r_ref[...]
            o_ref[...] = acc.astype(o_ref.dtype)

        if nk == 1:
            finish(part)
        else:
            acc_ref = scr[0]
            k = pl.program_id(2)

            @pl.when(k == 0)
            def _():
                acc_ref[...] = part

            @pl.when(k > 0)
            def _():
                acc_ref[...] += part

            @pl.when(k == nk - 1)
            def _():
                finish(acc_ref[...])

    in_specs = [a_spec, b_spec] + ([o_spec] if res is not None else [])
    args = (a, b) + ((res,) if res is not None else ())
    out_specs, out_shapes = o_spec, out_shape
    scratch = [pltpu.VMEM(acc_shape, F32)] if nk > 1 else []
    if comm:
        in_specs += comm.in_specs()
        args += tuple(comm.srcs)
        out_specs = [o_spec] + comm.out_specs()
        out_shapes = [out_shape] + comm.out_structs
        scratch += comm.scratch()
    return pl.pallas_call(
        body, name=name, grid=grid, in_specs=in_specs, out_specs=out_specs, out_shape=out_shapes,
        scratch_shapes=scratch,
        compiler_params=_cparams(*(("arbitrary",) * 3 if comm else ("parallel", "parallel", "arbitrary"))),
    )(*args)


def _mm_rows(name, a, b, contract, out_dtype, res=None, tm_pref=512, comm=None):
    S, K = a.shape
    N = b.shape[1] if contract == NN else b.shape[0]
    tm = _tile(S, tm_pref, 16)
    return _mm(name, a, b, grid=(1, S // tm, 1),
               a_spec=pl.BlockSpec((tm, K), lambda p, q, k: (q, 0)),
               b_spec=pl.BlockSpec(b.shape, lambda p, q, k: (0, 0)),
               o_spec=pl.BlockSpec((tm, N), lambda p, q, k: (q, 0)),
               out_shape=jax.ShapeDtypeStruct((S, N), out_dtype), contract=contract,
               acc_shape=(tm, N), res=res, comm=comm)


def _mm_tn(name, a, b, tm_pref=1024, tn_pref=1024, tk_pref=2048, comm=None):
    S, M = a.shape
    N = b.shape[1]
    tm, tn, tk = _tile(M, tm_pref, LANES), _tile(N, tn_pref, LANES), _tile(S, tk_pref, 16)
    return _mm(name, a, b, grid=(M // tm, N // tn, S // tk),
               a_spec=pl.BlockSpec((tk, tm), lambda p, q, k: (k, p)),
               b_spec=pl.BlockSpec((tk, tn), lambda p, q, k: (k, q)),
               o_spec=pl.BlockSpec((tm, tn), lambda p, q, k: (p, q)),
               out_shape=jax.ShapeDtypeStruct((M, N), F32), contract=TN, acc_shape=(tm, tn), comm=comm)


def _rmsnorm_fwd(name, x, gain):
    S, D = x.shape
    tm = _tile(S, 512, 16)

    def body(x_ref, g_ref, o_ref):
        xf = x_ref[...]
        r = lax.rsqrt(jnp.mean(xf * xf, axis=-1, keepdims=True) + EPS)
        o_ref[...] = (xf * r * g_ref[...]).astype(o_ref.dtype)

    return pl.pallas_call(
        body, name=name, grid=(S // tm,),
        in_specs=[pl.BlockSpec((tm, D), lambda i: (i, 0)), pl.BlockSpec((1, D), lambda i: (0, 0))],
        out_specs=pl.BlockSpec((tm, D), lambda i: (i, 0)),
        out_shape=jax.ShapeDtypeStruct((S, D), BF16), compiler_params=_cparams("parallel"),
    )(x, gain)


def _rms_bwd_math(dh, xf, gain):
    r = lax.rsqrt(jnp.mean(xf * xf, axis=-1, keepdims=True) + EPS)
    xhat = xf * r
    dxh = dh * gain
    dx = r * (dxh - xhat * jnp.mean(dxh * xhat, axis=-1, keepdims=True))
    return dx, jnp.sum(dh * xhat, axis=0, keepdims=True)


def _rmsnorm_bwd(name, dh, x, gain, dres):
    S, D = x.shape
    tm = _tile(S, 256, 16)

    def body(dh_ref, x_ref, g_ref, dr_ref, dx_ref, dxb_ref, dg_ref):
        dx, dg = _rms_bwd_math(dh_ref[...], x_ref[...], g_ref[...])
        dx = dr_ref[...] + dx
        dx_ref[...] = dx
        dxb_ref[...] = dx.astype(BF16)

        @pl.when(pl.program_id(0) == 0)
        def _():
            dg_ref[...] = dg

        @pl.when(pl.program_id(0) > 0)
        def _():
            dg_ref[...] += dg

    row = pl.BlockSpec((tm, D), lambda i: (i, 0))
    vec = pl.BlockSpec((1, D), lambda i: (0, 0))
    return pl.pallas_call(
        body, name=name, grid=(S // tm,), in_specs=[row, row, vec, row], out_specs=[row, row, vec],
        out_shape=[jax.ShapeDtypeStruct((S, D), F32), jax.ShapeDtypeStruct((S, D), BF16),
                   jax.ShapeDtypeStruct((1, D), F32)],
        compiler_params=_cparams("arbitrary"),
    )(dh, x, gain, dres)


def _loss_fwd_bwd(y, target):
    S, D = y.shape
    tm = _tile(S, 512, 16)

    def body(y_ref, t_ref, l_ref, dy_ref, dyb_ref):
        e = y_ref[...] - t_ref[...]
        dy = e * (1.0 / D)
        dy_ref[...] = dy
        dyb_ref[...] = dy.astype(BF16)
        part = 0.5 * jnp.sum(jnp.mean(e * e, axis=-1, keepdims=True), axis=0, keepdims=True)
        part = jnp.broadcast_to(part, (SUBLANES, LANES))

        @pl.when(pl.program_id(0) == 0)
        def _():
            l_ref[...] = part

        @pl.when(pl.program_id(0) > 0)
        def _():
            l_ref[...] += part

    row = pl.BlockSpec((tm, D), lambda i: (i, 0))
    return pl.pallas_call(
        body, name="loss", grid=(S // tm,), in_specs=[row, row],
        out_specs=[pl.BlockSpec((SUBLANES, LANES), lambda i: (0, 0)), row, row],
        out_shape=[jax.ShapeDtypeStruct((SUBLANES, LANES), F32), jax.ShapeDtypeStruct((S, D), F32),
                   jax.ShapeDtypeStruct((S, D), BF16)],
        compiler_params=_cparams("arbitrary"),
    )(y, target)


def _sigmoid(v):
    return 1.0 / (1.0 + jnp.exp(-v))


MXU_COLS = 256


def _col_chunks(n):
    return [slice(c, min(c + MXU_COLS, n)) for c in range(0, n, MXU_COLS)]


def _shift_rows(v, k, edge8, down):
    tm = v.shape[0]
    sub = lax.broadcasted_iota(jnp.int32, edge8.shape, 0)
    if down:
        r = pltpu.roll(v, k, axis=0)
        head = jnp.where(sub < k, pltpu.roll(edge8, k, axis=0), r[0:8, :])
        return jnp.concatenate([head, r[8:, :]], axis=0)
    r = pltpu.roll(v, tm - k, axis=0)
    tail = jnp.where(sub >= 8 - k, pltpu.roll(edge8, 8 - k, axis=0), r[tm - 8:tm, :])
    return jnp.concatenate([r[:tm - 8, :], tail], axis=0)


def _ffn_up_fwd(name, h2, wup, cw, cb, ag=None):
    S, D = h2.shape
    FF = wup.shape[1] // 2
    CB = _tile(FF, 1408, LANES)
    NJ = FF // CB
    tm = _tile(S, 512, 16)
    nI = S // tm
    n_steps = NJ * nI
    n_ag = len(ag.items) if ag else 0
    schedule = (0, (9 * n_steps) // 20, (7 * n_steps) // 10, n_steps - 1)

    def body(*refs):
        h_ref, wg_ref, wv_ref, cwg_ref, cwv_ref, cbg_ref, cbv_ref = refs[:7]
        u_ref, ab_ref, a_ref = refs[7 + n_ag:10 + n_ag]
        edge_g, edge_v = refs[10 + 2 * n_ag:12 + 2 * n_ag]
        if ag:
            ag.run_at(pl.program_id(0) * nI + pl.program_id(1), schedule, refs[7:7 + n_ag],
                      refs[10 + n_ag:10 + 2 * n_ag], *refs[12 + 2 * n_ag:])

        @pl.when(pl.program_id(1) == 0)
        def _():
            edge_g[...] = jnp.zeros((8, CB), F32)
            edge_v[...] = jnp.zeros((8, CB), F32)

        h = h_ref[...]

        def conv(w_ref, cw_ref, cb_ref, edge, slot, cs):
            u = jnp.dot(h, w_ref[:, cs], preferred_element_type=F32)
            u_ref[slot, :, cs] = u.astype(BF16)
            prev8 = edge[:, cs]
            uc = (cw_ref[0:1, cs] * _shift_rows(u, 2, prev8, True) + cw_ref[1:2, cs] * _shift_rows(u, 1, prev8, True)
                  + cw_ref[2:3, cs] * u + cb_ref[:, cs])
            edge[:, cs] = u[tm - 8:tm, :]
            return uc

        cs = slice(0, CB)
        gc = conv(wg_ref, cwg_ref, cbg_ref, edge_g, 0, cs)
        vc = conv(wv_ref, cwv_ref, cbv_ref, edge_v, 1, cs)
        sig = _sigmoid(gc)
        silu = gc * sig
        a_ref[...] = (silu * vc).astype(BF16)
        ab_ref[0] = (vc * (sig * (1.0 + gc * (1.0 - sig)))).astype(BF16)
        ab_ref[1] = silu.astype(BF16)

    def wspec(off):
        return pl.BlockSpec((D, CB), lambda j, i: (0, j + off))

    def cspec(rows, off):
        return pl.BlockSpec((rows, CB), lambda j, i: (0, j + off))

    pair = pl.BlockSpec((2, tm, CB), lambda j, i: (0, i, j))
    in_specs = [pl.BlockSpec((tm, D), lambda j, i: (i, 0)), wspec(0), wspec(NJ),
                cspec(3, 0), cspec(3, NJ), cspec(1, 0), cspec(1, NJ)]
    out_specs = [pair, pair, pl.BlockSpec((tm, CB), lambda j, i: (i, j))]
    out_shape = [jax.ShapeDtypeStruct((2, S, FF), BF16), jax.ShapeDtypeStruct((2, S, FF), BF16),
                 jax.ShapeDtypeStruct((S, FF), BF16)]
    scratch = [pltpu.VMEM((8, CB), F32), pltpu.VMEM((8, CB), F32)]
    args = (h2, wup, wup, cw, cw, cb, cb)
    if ag:
        in_specs += ag.in_specs()
        out_specs += ag.out_specs()
        out_shape += ag.out_structs()
        scratch += ag.scratch()
        args += tuple(ag.srcs())
    return pl.pallas_call(
        body, name=name, grid=(NJ, nI), in_specs=in_specs, out_specs=out_specs, out_shape=out_shape,
        scratch_shapes=scratch, compiler_params=_cparams("arbitrary", "arbitrary"),
    )(*args)


def _ffn_da_bwd(name, dyb, wd, u3, ab3, cw):
    S, D = dyb.shape
    FF = wd.shape[0]
    CB = _tile(FF, 1408, LANES)
    NJ = FF // CB
    tm = _tile(S, 256, 16)
    nI = S // tm

    def body(dy_ref, wd_ref, u_ref, ab_ref, cwg_ref, cwv_ref, du_ref, dc_ref, edge_g, edge_v):
        i = pl.program_id(1)

        @pl.when(i == 0)
        def _():
            edge_g[...] = jnp.zeros((8, CB), F32)
            edge_v[...] = jnp.zeros((8, CB), F32)
            dc_ref[...] = jnp.zeros(dc_ref.shape, F32)

        dy = dy_ref[...]

        def back(slot, d_uc, edge, cw_ref, cs):
            u = u_ref[slot, :, cs].astype(F32)
            next8 = edge[:, cs]
            dp1 = _shift_rows(d_uc, 1, next8, False)
            dp2 = _shift_rows(d_uc, 2, next8, False)
            du = cw_ref[2:3, cs] * d_uc + cw_ref[1:2, cs] * dp1 + cw_ref[0:1, cs] * dp2
            edge[:, cs] = d_uc[0:8, :]
            du_ref[slot, :, cs] = du.astype(BF16)
            dc_ref[slot, 0:1, cs] += jnp.sum(dp2 * u, axis=0, keepdims=True)
            dc_ref[slot, 1:2, cs] += jnp.sum(dp1 * u, axis=0, keepdims=True)
            dc_ref[slot, 2:3, cs] += jnp.sum(d_uc * u, axis=0, keepdims=True)
            dc_ref[slot, 3:4, cs] += jnp.sum(d_uc, axis=0, keepdims=True)

        for cs in _col_chunks(CB):
            da = lax.dot_general(dy, wd_ref[cs, :], NT, preferred_element_type=F32)
            back(0, da * ab_ref[0, :, cs].astype(F32), edge_g, cwg_ref, cs)
            back(1, da * ab_ref[1, :, cs].astype(F32), edge_v, cwv_ref, cs)

    def rev(i):
        return nI - 1 - i

    pair = pl.BlockSpec((2, tm, CB), lambda j, i: (0, rev(i), j))
    return pl.pallas_call(
        body, name=name, grid=(NJ, nI),
        in_specs=[pl.BlockSpec((tm, D), lambda j, i: (rev(i), 0)),
                  pl.BlockSpec((CB, D), lambda j, i: (j, 0)), pair, pair,
                  pl.BlockSpec((3, CB), lambda j, i: (0, j)), pl.BlockSpec((3, CB), lambda j, i: (0, j + NJ))],
        out_specs=[pair, pl.BlockSpec((2, 8, CB), lambda j, i: (0, 0, j))],
        out_shape=[jax.ShapeDtypeStruct((2, S, FF), BF16), jax.ShapeDtypeStruct((2, 8, FF), F32)],
        scratch_shapes=[pltpu.VMEM((8, CB), F32) for _ in range(2)],
        compiler_params=_cparams("parallel", "arbitrary"),
    )(dyb, wd, u3, ab3, cw, cw)


def _ffn_down_fwd(name, a, wd, res):
    S, FF = a.shape
    D = wd.shape[1]
    tm = _tile(S, 512, 16)
    tk = _tile(FF, 2816, LANES)
    return _mm(name, a, wd, grid=(1, S // tm, FF // tk),
               a_spec=pl.BlockSpec((tm, tk), lambda p, q, k: (q, k)),
               b_spec=pl.BlockSpec((tk, D), lambda p, q, k: (k, 0)),
               o_spec=pl.BlockSpec((tm, D), lambda p, q, k: (q, 0)),
               out_shape=jax.ShapeDtypeStruct((S, D), F32), contract=NN, acc_shape=(tm, D), res=res)


def _ffn_dh_bwd(name, du3, wup, comm=None):
    _, S, FF = du3.shape
    D = wup.shape[0]
    tm = _tile(S, 512, 16)
    tk = _tile(FF, 2816, LANES)
    nh = FF // tk
    return _mm(name, du3, wup, grid=(1, S // tm, 2 * nh),
               a_spec=pl.BlockSpec((None, tm, tk), lambda p, q, k: (k // nh, q, k % nh)),
               b_spec=pl.BlockSpec((D, tk), lambda p, q, k: (0, k)),
               o_spec=pl.BlockSpec((tm, D), lambda p, q, k: (q, 0)),
               out_shape=jax.ShapeDtypeStruct((S, D), F32), contract=NT, acc_shape=(tm, D), comm=comm)


def _ffn_dwup_bwd(name, h2, du3, comm=None):
    S, D = h2.shape
    FF = du3.shape[2]
    NJ = NDEV // 2
    CB = FF // NJ
    tm, tk = _tile(D, 1024, LANES), _tile(S, 2048, 16)
    return _mm(name, h2, du3, grid=(NDEV, D // tm, S // tk),
               a_spec=pl.BlockSpec((tk, tm), lambda p, q, k: (k, q)),
               b_spec=pl.BlockSpec((None, tk, CB), lambda p, q, k: (p // NJ, k, p % NJ)),
               o_spec=pl.BlockSpec((None, tm, CB), lambda p, q, k: (p, q, 0)),
               out_shape=jax.ShapeDtypeStruct((NDEV, D, CB), F32), contract=TN, acc_shape=(tm, CB), comm=comm)


def _t5_onehot():
    i = np.arange(WINDOW)[:, None]
    j = np.arange(2 * WINDOW)[None, :]
    n = np.maximum(WINDOW + i - j, 0)
    max_exact = N_BUCKETS // 2
    nf = np.maximum(n, 1).astype(np.float32)
    large = max_exact + (np.log(nf / max_exact) / np.log(MAX_DISTANCE / max_exact)
                         * (N_BUCKETS - max_exact)).astype(np.int32)
    large = np.minimum(large, N_BUCKETS - 1)
    bucket = np.where(n < max_exact, n, large).astype(np.int32).reshape(-1)
    return (np.arange(N_BUCKETS)[:, None] == bucket[None, :]).astype(np.float32)


def _bias_band(rel_bias, onehot):
    H = rel_bias.shape[0]
    n = onehot.shape[1]

    def body(r_ref, oh_ref, o_ref):
        o_ref[...] = jnp.dot(r_ref[...], oh_ref[...], preferred_element_type=F32,
                             precision=lax.Precision.HIGHEST)

    return pl.pallas_call(body, name="bias_band", out_shape=jax.ShapeDtypeStruct((H, n), F32),
                          compiler_params=pltpu.CompilerParams(vmem_limit_bytes=VMEM_LIMIT))(rel_bias, onehot)


def _bias_band_bwd(dband, onehot):
    H = dband.shape[0]

    def body(d_ref, oh_ref, o_ref):
        o_ref[...] = lax.dot_general(d_ref[...], oh_ref[...], NT, preferred_element_type=F32,
                                     precision=lax.Precision.HIGHEST)

    return pl.pallas_call(body, name="bias_band_bwd", out_shape=jax.ShapeDtypeStruct((H, N_BUCKETS), F32),
                          compiler_params=pltpu.CompilerParams(vmem_limit_bytes=VMEM_LIMIT))(dband, onehot)


def _band_valid(n):
    i = lax.broadcasted_iota(jnp.int32, (WINDOW, 2 * WINDOW), 0)
    j = lax.broadcasted_iota(jnp.int32, (WINDOW, 2 * WINDOW), 1)
    return (j > i) & (j <= i + WINDOW) & ((n > 0) | (j >= WINDOW))


def _head_norm(v, gain):
    r = lax.rsqrt(jnp.mean(v * v, axis=-1, keepdims=True) + EPS)
    vhat = v * r
    return r, vhat, vhat * gain


def _stack_heads(t, kh):
    return jnp.concatenate([t[:, (kh * GQA_GROUP + g) * HEAD_DIM:(kh * GQA_GROUP + g + 1) * HEAD_DIM]
                            for g in range(GQA_GROUP)], axis=0)


def _unstack_heads(t8):
    return jnp.concatenate([t8[g * WINDOW:(g + 1) * WINDOW, :] for g in range(GQA_GROUP)], axis=1)


def _band_scores(qn, kn, b_ref, kh, valid, scale):
    s = lax.dot_general(qn, kn, NT, preferred_element_type=F32) * scale
    s = s.reshape(GQA_GROUP, WINDOW, 2 * WINDOW) + b_ref[kh * GQA_GROUP:(kh + 1) * GQA_GROUP]
    s = jnp.where(valid[None], s, NEG_INF)
    return s.reshape(GQA_GROUP * WINDOW, 2 * WINDOW)


def _sink_rows(s_ref, kh):
    return jnp.concatenate([jnp.broadcast_to(s_ref[kh * GQA_GROUP + g:kh * GQA_GROUP + g + 1, :], (WINDOW, 1))
                            for g in range(GQA_GROUP)], axis=0)


def _attn_fwd(name, qkv, bias, qg, kg, sinks, ag=None):
    S, QW = qkv.shape
    H = bias.shape[0]
    D = H * HEAD_DIM
    KV = H // GQA_GROUP
    kvw = QW - D
    kvb = D // kvw
    nb = S // WINDOW
    scale = HEAD_DIM ** -0.5
    n_ag = len(ag.items) if ag else 0
    schedule = (0, (9 * nb) // 20, (7 * nb) // 10, nb - 1)

    def body(*refs):
        q_ref, kc_ref, kp_ref, b_ref, qg_ref, kg_ref, s_ref = refs[:7]
        o_ref, l_ref = refs[7 + n_ag:9 + n_ag]
        n = pl.program_id(0)
        if ag:
            ag.run_at(n, schedule, refs[7:7 + n_ag], refs[9 + n_ag:9 + 2 * n_ag], *refs[9 + 2 * n_ag:])
        valid = _band_valid(n)
        q = q_ref[...]
        kvc = kc_ref[...]
        kvp = kp_ref[...]
        lane = lax.broadcasted_iota(jnp.int32, (WINDOW, H), 1)
        lse_all = jnp.zeros((WINDOW, H), F32)
        outs = []
        for kh in range(KV):
            ks = slice(kh * HEAD_DIM, (kh + 1) * HEAD_DIM)
            vs = slice((KV + kh) * HEAD_DIM, (KV + kh + 1) * HEAD_DIM)
            kb = jnp.concatenate([kvp[:, ks], kvc[:, ks]], axis=0)
            vb = jnp.concatenate([kvp[:, vs], kvc[:, vs]], axis=0).astype(BF16)
            kn = _head_norm(kb, kg_ref[...])[2].astype(BF16)
            qn = _head_norm(_stack_heads(q, kh), qg_ref[...])[2].astype(BF16)
            s = _band_scores(qn, kn, b_ref, kh, valid, scale)
            sink = _sink_rows(s_ref, kh)
            m = jnp.maximum(jnp.max(s, axis=-1, keepdims=True), sink)
            p = jnp.exp(s - m)
            den = jnp.sum(p, axis=-1, keepdims=True) + jnp.exp(sink - m)
            p = p * (1.0 / den)
            outs.append(_unstack_heads(jnp.dot(p.astype(BF16), vb, preferred_element_type=F32)))
            lse = m + jnp.log(den)
            for g in range(GQA_GROUP):
                lse_all = jnp.where(lane == kh * GQA_GROUP + g, lse[g * WINDOW:(g + 1) * WINDOW, :], lse_all)
        o_ref[...] = jnp.concatenate(outs, axis=1).astype(BF16)
        l_ref[...] = lse_all

    const2 = lambda shape: pl.BlockSpec(shape, lambda n: (0, 0))
    in_specs = [pl.BlockSpec((WINDOW, D), lambda n: (n, 0)),
                pl.BlockSpec((WINDOW, kvw), lambda n: (n, kvb)),
                pl.BlockSpec((WINDOW, kvw), lambda n: (jnp.maximum(n - 1, 0), kvb)),
                pl.BlockSpec(bias.shape, lambda n: (0, 0, 0)),
                const2((1, HEAD_DIM)), const2((1, HEAD_DIM)), const2((H, 1))]
    out_specs = [pl.BlockSpec((WINDOW, D), lambda n: (n, 0)), pl.BlockSpec((WINDOW, H), lambda n: (n, 0))]
    out_shape = [jax.ShapeDtypeStruct((S, D), BF16), jax.ShapeDtypeStruct((S, H), F32)]
    args = (qkv, qkv, qkv, bias, qg, kg, sinks)
    scratch = []
    if ag:
        in_specs += ag.in_specs()
        out_specs += ag.out_specs()
        out_shape += ag.out_structs()
        scratch += ag.scratch()
        args += tuple(ag.srcs())
    return pl.pallas_call(
        body, name=name, grid=(nb,), in_specs=in_specs, out_specs=out_specs, out_shape=out_shape,
        scratch_shapes=scratch, compiler_params=_cparams("arbitrary" if ag else "parallel"),
    )(*args)


def _attn_bwd(name, qkv, do, lse, bias, qg, kg, sinks, comm=None):
    S, QW = qkv.shape
    H = bias.shape[0]
    D = H * HEAD_DIM
    KV = H // GQA_GROUP
    kvw = QW - D
    kvb = D // kvw
    nb = S // WINDOW
    scale = HEAD_DIM ** -0.5
    n_ci = len(comm.srcs) if comm else 0
    n_co = len(comm.out_structs) if comm else 0

    def body(*refs):
        q_ref, kc_ref, kp_ref, do_ref, l_ref, b_ref, qg_ref, kg_ref, s_ref = refs[:9]
        dq_ref, dkv_ref, db_ref, ds_ref, dqg_ref, dkg_ref = refs[9 + n_ci:15 + n_ci]
        carry = refs[15 + n_ci + n_co]
        n = pl.program_id(0)
        if comm:
            comm.run(refs[9:9 + n_ci], refs[15 + n_ci:15 + n_ci + n_co], refs[-2], refs[-1], n == 0, n == nb)

        @pl.when(n == 0)
        def _():
            db_ref[...] = jnp.zeros(db_ref.shape, F32)
            ds_ref[...] = jnp.zeros(ds_ref.shape, F32)
            dqg_ref[...] = jnp.zeros(dqg_ref.shape, F32)
            dkg_ref[...] = jnp.zeros(dkg_ref.shape, F32)
            carry[...] = jnp.zeros(carry.shape, F32)

        @pl.when(n == nb)
        def _():
            dkv_ref[...] = carry[...].astype(BF16)

        @pl.when(n < nb)
        def _():
            valid = _band_valid(n)
            q = q_ref[...]
            kvc = kc_ref[...]
            kvp = kp_ref[...]
            do_all = do_ref[...]
            lse = l_ref[...]
            qgain = qg_ref[...]
            kgain = kg_ref[...]
            dqs, dk_parts, dv_parts = [], [], []
            dqg = jnp.zeros((1, HEAD_DIM), F32)
            dkg = jnp.zeros((1, HEAD_DIM), F32)
            for kh in range(KV):
                ks = slice(kh * HEAD_DIM, (kh + 1) * HEAD_DIM)
                vs = slice((KV + kh) * HEAD_DIM, (KV + kh + 1) * HEAD_DIM)
                heads = slice(kh * GQA_GROUP, (kh + 1) * GQA_GROUP)
                kb = jnp.concatenate([kvp[:, ks], kvc[:, ks]], axis=0)
                vb = jnp.concatenate([kvp[:, vs], kvc[:, vs]], axis=0).astype(BF16)
                rk, khat, kn32 = _head_norm(kb, kgain)
                kn = kn32.astype(BF16)
                rq, qhat, qn32 = _head_norm(_stack_heads(q, kh), qgain)
                qn = qn32.astype(BF16)
                s = _band_scores(qn, kn, b_ref, kh, valid, scale)
                lse8 = jnp.concatenate([lse[:, kh * GQA_GROUP + g:kh * GQA_GROUP + g + 1]
                                        for g in range(GQA_GROUP)], axis=0)
                p = jnp.exp(s - lse8)
                do8 = _stack_heads(do_all, kh)
                dp = lax.dot_general(do8, vb, NT, preferred_element_type=F32)
                delta = jnp.sum(p * dp, axis=-1, keepdims=True)
                ds = p * (dp - delta)
                db_ref[heads] += ds.reshape(GQA_GROUP, WINDOW, 2 * WINDOW)
                psink = jnp.exp(_sink_rows(s_ref, kh) - lse8)
                ds_ref[heads, :] += -jnp.sum((psink * delta).reshape(GQA_GROUP, WINDOW, 1), axis=1)
                dsb = (ds * scale).astype(BF16)
                dqn = jnp.dot(dsb, kn, preferred_element_type=F32)
                dkn = lax.dot_general(dsb, qn, TN, preferred_element_type=F32)
                dv_parts.append(lax.dot_general(p.astype(BF16), do8, TN, preferred_element_type=F32))
                dqg = dqg + jnp.sum(dqn * qhat, axis=0, keepdims=True)
                dqh = dqn * qgain
                dqs.append(_unstack_heads(rq * (dqh - qhat * jnp.mean(dqh * qhat, axis=-1, keepdims=True))))
                dkg = dkg + jnp.sum(dkn * khat, axis=0, keepdims=True)
                dkh = dkn * kgain
                dk_parts.append(rk * (dkh - khat * jnp.mean(dkh * khat, axis=-1, keepdims=True)))
            dq_ref[...] = jnp.concatenate(dqs, axis=1).astype(BF16)
            dqg_ref[...] += dqg
            dkg_ref[...] += dkg
            dkv = jnp.concatenate(dk_parts + dv_parts, axis=1)
            dkv_ref[...] = (carry[...] + dkv[0:WINDOW, :]).astype(BF16)
            carry[...] = dkv[WINDOW:2 * WINDOW, :]

    cur = lambda n: jnp.minimum(n, nb - 1)
    const2 = lambda shape: pl.BlockSpec(shape, lambda n: (0, 0))
    in_specs = [pl.BlockSpec((WINDOW, D), lambda n: (cur(n), 0)),
                pl.BlockSpec((WINDOW, kvw), lambda n: (cur(n), kvb)),
                pl.BlockSpec((WINDOW, kvw), lambda n: (jnp.maximum(cur(n) - 1, 0), kvb)),
                pl.BlockSpec((WINDOW, D), lambda n: (cur(n), 0)),
                pl.BlockSpec((WINDOW, H), lambda n: (cur(n), 0)),
                pl.BlockSpec(bias.shape, lambda n: (0, 0, 0)),
                const2((1, HEAD_DIM)), const2((1, HEAD_DIM)), const2((H, 1))]
    out_specs = [pl.BlockSpec((WINDOW, D), lambda n: (cur(n), 0)),
                 pl.BlockSpec((WINDOW, kvw), lambda n: (jnp.maximum(n - 1, 0), 0)),
                 pl.BlockSpec(bias.shape, lambda n: (0, 0, 0)),
                 const2((H, 1)), const2((1, HEAD_DIM)), const2((1, HEAD_DIM))]
    out_shape = [jax.ShapeDtypeStruct((S, D), BF16), jax.ShapeDtypeStruct((S, kvw), BF16),
                 jax.ShapeDtypeStruct(bias.shape, F32), jax.ShapeDtypeStruct((H, 1), F32),
                 jax.ShapeDtypeStruct((1, HEAD_DIM), F32), jax.ShapeDtypeStruct((1, HEAD_DIM), F32)]
    scratch = [pltpu.VMEM((WINDOW, kvw), F32)]
    args = (qkv, qkv, qkv, do, lse, bias, qg, kg, sinks)
    if comm:
        in_specs += comm.in_specs()
        out_specs += comm.out_specs()
        out_shape += comm.out_structs
        scratch += comm.scratch()
        args += tuple(comm.srcs)
    return pl.pallas_call(
        body, name=name, grid=(nb + 1,), in_specs=in_specs, out_specs=out_specs, out_shape=out_shape,
        scratch_shapes=scratch, compiler_params=_cparams("arbitrary"),
    )(*args)


def _window_sums(src, bufs, lo, n_rows, step_sign, col_groups):
    out = []
    for g, cols in enumerate(col_groups):
        prev = src
        for level in range(g + 1):
            k = step_sign * (1 << level)
            cur = bufs[level]
            cur[pl.ds(lo, n_rows), cols] = prev[pl.ds(lo, n_rows), cols] + prev[pl.ds(lo + k, n_rows), cols]
            prev = cur
        out.append(prev)
    return out


def _pool_fwd(name, x, gain, wp, scale):
    S, D = x.shape
    G, C = wp.shape[0], wp.shape[1]
    tm = _tile(S, 256, POOL_HALO)
    hb = tm // POOL_HALO
    HL = POOL_HALO
    groups = [slice(g * C, (g + 1) * C) for g in range(G)]

    def body(x_ref, xh_ref, g_ref, w_ref, sc_ref, o_ref, d_ref, ext, p2, p4, p8, p16):
        i = pl.program_id(0)
        gain_v = g_ref[...]
        xt = x_ref[...]
        h = _head_norm(xt, gain_v)[2]
        hh = _head_norm(xh_ref[...], gain_v)[2]
        ext[pl.ds(0, HL), :] = jnp.where(i > 0, hh, 0.0)
        ext[pl.ds(HL, tm), :] = h
        bufs = (p2, p4, p8, p16)
        for b in bufs:
            b[pl.ds(0, 8), :] = jnp.zeros((8, D), F32)
        sums = _window_sums(ext, bufs, 8, tm + HL - 8, -1, groups)
        t = i * tm + lax.broadcasted_iota(jnp.int32, (tm, 1), 0)
        for g, cols in enumerate(groups):
            cnt = jnp.minimum(t + 1, POOL_WINDOWS[g]).astype(F32)
            d = sums[g][pl.ds(HL, tm), cols] / cnt - h[:, cols]
            db = d.astype(BF16)
            d_ref[:, cols] = db
            y = jnp.dot(db, w_ref[g], preferred_element_type=F32)
            o_ref[:, cols] = xt[:, cols] + y * sc_ref[:, cols]

    row = pl.BlockSpec((tm, D), lambda i: (i, 0))
    vec = pl.BlockSpec((1, D), lambda i: (0, 0))
    return pl.pallas_call(
        body, name=name, grid=(S // tm,),
        in_specs=[row, pl.BlockSpec((HL, D), lambda i: (jnp.maximum(i * hb - 1, 0), 0)), vec,
                  pl.BlockSpec(wp.shape, lambda i: (0, 0, 0)), vec],
        out_specs=[row, row],
        out_shape=[jax.ShapeDtypeStruct((S, D), F32), jax.ShapeDtypeStruct((S, D), BF16)],
        scratch_shapes=[pltpu.VMEM((tm + HL, D), F32) for _ in range(5)],
        compiler_params=_cparams("parallel"),
    )(x, x, gain, wp, scale)


def _pool_bwd(name, dx1, x, gain, wp, scale, dsave):
    S, D = x.shape
    G, C = wp.shape[0], wp.shape[1]
    tm = _tile(S, 256, POOL_HALO)
    hb = tm // POOL_HALO
    HL = POOL_HALO
    nI = S // tm
    groups = [slice(g * C, (g + 1) * C) for g in range(G)]

    def body(dx_ref, dxh_ref, x_ref, g_ref, w_ref, sc_ref, ds_ref, o_ref, ob_ref, dw_ref, dsc_ref, dg_ref,
             ext, p2, p4, p8, p16):
        i = pl.program_id(0)

        @pl.when(i == 0)
        def _():
            dw_ref[...] = jnp.zeros(dw_ref.shape, F32)
            dsc_ref[...] = jnp.zeros(dsc_ref.shape, F32)
            dg_ref[...] = jnp.zeros(dg_ref.shape, F32)

        dx1t = dx_ref[...]
        sc = sc_ref[...]
        dys = (dx1t * sc).astype(BF16)
        dys_h = (dxh_ref[...] * sc).astype(BF16)
        t = i * tm + lax.broadcasted_iota(jnp.int32, (tm, 1), 0)
        th = (i + 1) * tm + lax.broadcasted_iota(jnp.int32, (HL, 1), 0)
        dds = []
        for g, cols in enumerate(groups):
            dsv = ds_ref[:, cols]
            y = jnp.dot(dsv, w_ref[g], preferred_element_type=F32)
            dsc_ref[:, cols] += jnp.sum(dx1t[:, cols] * y, axis=0, keepdims=True)
            dw_ref[g] += lax.dot_general(dsv, dys[:, cols], TN, preferred_element_type=F32)
            dd = lax.dot_general(dys[:, cols], w_ref[g], NT, preferred_element_type=F32)
            dd_h = lax.dot_general(dys_h[:, cols], w_ref[g], NT, preferred_element_type=F32)
            dds.append(dd)
            w = POOL_WINDOWS[g]
            ext[pl.ds(0, tm), cols] = dd / jnp.minimum(t + 1, w).astype(F32)
            e_h = dd_h / jnp.minimum(th + 1, w).astype(F32)
            ext[pl.ds(tm, HL), cols] = jnp.where(i < nI - 1, e_h, 0.0)
        bufs = (p2, p4, p8, p16)
        for b in bufs:
            b[pl.ds(tm + HL - 8, 8), :] = jnp.zeros((8, D), F32)
        sums = _window_sums(ext, bufs, 0, tm + HL - 8, 1, groups)
        dh = jnp.concatenate([sums[g][pl.ds(0, tm), cols] - dds[g] for g, cols in enumerate(groups)], axis=1)
        dx, dg = _rms_bwd_math(dh, x_ref[...], g_ref[...])
        dx = dx1t + dx
        o_ref[...] = dx
        ob_ref[...] = dx.astype(BF16)
        dg_ref[...] += dg

    row = pl.BlockSpec((tm, D), lambda i: (i, 0))
    vec = pl.BlockSpec((1, D), lambda i: (0, 0))
    last_h = S // HL - 1
    return pl.pallas_call(
        body, name=name, grid=(nI,),
        in_specs=[row, pl.BlockSpec((HL, D), lambda i: (jnp.minimum((i + 1) * hb, last_h), 0)), row, vec,
                  pl.BlockSpec(wp.shape, lambda i: (0, 0, 0)), vec, row],
        out_specs=[row, row, pl.BlockSpec(wp.shape, lambda i: (0, 0, 0)), vec, vec],
        out_shape=[jax.ShapeDtypeStruct((S, D), F32), jax.ShapeDtypeStruct((S, D), BF16),
                   jax.ShapeDtypeStruct(wp.shape, F32),
                   jax.ShapeDtypeStruct((1, D), F32), jax.ShapeDtypeStruct((1, D), F32)],
        scratch_shapes=[pltpu.VMEM((tm + HL, D), F32) for _ in range(5)],
        compiler_params=_cparams("arbitrary"),
    )(dx1, dx1, x, gain, wp, scale, dsave)


HBM_SPEC = pl.BlockSpec(memory_space=pltpu.HBM)


def _coords():
    return lax.axis_index("x"), lax.axis_index("y"), lax.axis_index("c")


class _AgItem:
    def __init__(self, src, out_struct, slot, half):
        self.src, self.out_struct, self.slot, self.half = src, out_struct, slot, half


def _rows_item(src, n_rows_total):
    r, ncol = src.shape
    return _AgItem(src, jax.ShapeDtypeStruct((n_rows_total, ncol), src.dtype),
                   lambda out, d: out.at[pl.ds(pl.multiple_of(d * r, 16), r), :],
                   lambda ref, h: ref.at[pl.ds(h * (r // 2), r // 2), :])


def _cols_item(src, n_cols_total):
    nrow, cb = src.shape
    return _AgItem(src, jax.ShapeDtypeStruct((nrow, n_cols_total), src.dtype),
                   lambda out, d: out.at[:, pl.ds(pl.multiple_of(d * cb, LANES), cb)],
                   lambda ref, h: ref.at[pl.ds(h * (nrow // 2), nrow // 2), :])


def _lead_item(src):
    return _AgItem(src, jax.ShapeDtypeStruct((NDEV,) + src.shape, src.dtype),
                   lambda out, d: out.at[d],
                   lambda ref, h: ref.at[pl.ds(h * (src.shape[0] // 2), src.shape[0] // 2)])


def _pool_item(src, c_total):
    g, pc, c = src.shape
    return _AgItem(src, jax.ShapeDtypeStruct((g, c_total, c), src.dtype),
                   lambda out, d: out.at[:, pl.ds(pl.multiple_of(d * pc, 16), pc), :],
                   lambda ref, h: ref.at[pl.ds(h * (g // 2), g // 2)])


class _AllGather:
    N_PHASES = 4

    def __init__(self, items):
        self.items = list(items)

    def srcs(self):
        return [it.src for it in self.items]

    def out_structs(self):
        return [it.out_struct for it in self.items]

    def in_specs(self):
        return [HBM_SPEC] * len(self.items)

    def out_specs(self):
        return [HBM_SPEC] * len(self.items)

    def scratch(self):
        n = len(self.items)
        return [pltpu.SemaphoreType.DMA((8 * n,)), pltpu.SemaphoreType.DMA((8 * n,)), pltpu.SemaphoreType.DMA((n,))]

    def phase(self, ph, ins, outs, send_sems, recv_sems, local_sems):
        x, y, c = _coords()
        me, xn = 4 * x + 2 * y + c, 4 * (1 - x) + 2 * y + c
        yn, dg = 4 * x + 2 * (1 - y) + c, 4 * (1 - x) + 2 * (1 - y) + c
        XN, YN, SB = (1 - x, y, c), (x, 1 - y, c), (x, y, 1 - c)
        sib = lambda blk: blk + 1 - 2 * c
        for o, it in enumerate(self.items):
            slot = lambda d, it=it, o=o: it.slot(outs[o], d)
            half = it.half
            table = [
                (ins[o], slot(me), XN, slot(xn)),
                (ins[o], slot(me), YN, slot(yn)),
                (half(slot(xn), 0), half(slot(xn), 0), YN, half(slot(dg), 0)),
                (half(slot(yn), 1), half(slot(yn), 1), XN, half(slot(dg), 1)),
                (ins[o], slot(me), SB, slot(sib(me))),
                (slot(xn), slot(xn), SB, slot(sib(xn))),
                (slot(yn), slot(yn), SB, slot(sib(yn))),
                (slot(dg), slot(dg), SB, slot(sib(dg))),
            ]

            def send(k, table=table, o=o):
                src, dst, peer, _ = table[k]
                return pltpu.make_async_remote_copy(src_ref=src, dst_ref=dst, send_sem=send_sems.at[8 * o + k],
                                                    recv_sem=recv_sems.at[8 * o + k], device_id=peer, device_id_type=MESH)

            def arrived(k, table=table, o=o):
                land = table[k][3]
                pltpu.make_async_remote_copy(src_ref=land, dst_ref=land, send_sem=send_sems.at[8 * o + k],
                                             recv_sem=recv_sems.at[8 * o + k], device_id=table[k][2],
                                             device_id_type=MESH).wait_recv()

            local = pltpu.make_async_copy(ins[o], slot(me), local_sems.at[o])
            if ph == 0:
                local.start()
                for k in (0, 1, 4):
                    send(k).start()
            elif ph == 1:
                arrived(0)
                send(2).start()
                send(5).start()
                arrived(1)
                send(3).start()
                send(6).start()
            elif ph == 2:
                arrived(2)
                arrived(3)
                send(7).start()
            else:
                for k in (4, 5, 6, 7):
                    arrived(k)
                for k in range(8):
                    send(k).wait_send()
                local.wait()

    def run_at(self, step, schedule, ins, outs, send_sems, recv_sems, local_sems):
        for ph in range(self.N_PHASES):
            @pl.when(step == schedule[ph])
            def _(ph=ph):
                self.phase(ph, ins, outs, send_sems, recv_sems, local_sems)


def _allgather_now(name, ag):
    n = len(ag.items)

    def body(*refs):
        for ph in range(ag.N_PHASES):
            ag.phase(ph, refs[:n], refs[n:2 * n], *refs[2 * n:])

    return pl.pallas_call(body, name=name, in_specs=ag.in_specs(), out_specs=ag.out_specs(),
                          out_shape=ag.out_structs(), scratch_shapes=ag.scratch())(*ag.srcs())


def _allgather_small(name, block):
    m_per, ncol = block.shape

    def body(x_ref, out_ref, send_sems, recv_sems, local_sem):
        x, y, c = _coords()
        me, sibling = (x, y, c), (x, y, 1 - c)
        chips = [(1 - x, y), (x, 1 - y), (1 - x, 1 - y)]

        def rows(px, py, pc):
            return out_ref.at[pl.ds((4 * px + 2 * py + pc) * m_per, m_per), :]

        def copy(k, block_of, to, src=None):
            return pltpu.make_async_remote_copy(
                src_ref=rows(*block_of) if src is None else src, dst_ref=rows(*block_of),
                send_sem=send_sems.at[k], recv_sem=recv_sems.at[k], device_id=to, device_id_type=MESH)

        mine = pltpu.make_async_copy(x_ref, rows(*me), local_sem)
        mine.start()
        first = [copy(0, me, sibling, src=x_ref)]
        first += [copy(1 + j, me, (*chip, c), src=x_ref) for j, chip in enumerate(chips)]
        for cp in first:
            cp.start()
        passed = [copy(4 + j, (*chip, c), sibling) for j, chip in enumerate(chips)]
        for j, chip in enumerate(chips):
            copy(1 + j, (*chip, c), me).wait_recv()
            passed[j].start()
        copy(0, sibling, me).wait_recv()
        for j, chip in enumerate(chips):
            copy(4 + j, (*chip, 1 - c), me).wait_recv()
        for cp in first + passed:
            cp.wait_send()
        mine.wait()

    return pl.pallas_call(
        body, name=name, out_shape=jax.ShapeDtypeStruct((NDEV * m_per, ncol), block.dtype),
        in_specs=[pl.BlockSpec(memory_space=pltpu.VMEM)], out_specs=pl.BlockSpec(memory_space=pltpu.VMEM),
        scratch_shapes=[pltpu.SemaphoreType.DMA((7,)), pltpu.SemaphoreType.DMA((7,)), pltpu.SemaphoreType.DMA],
        compiler_params=pltpu.CompilerParams(vmem_limit_bytes=VMEM_LIMIT),
    )(block)


def _exchange_now(name, ex):
    n_in, n_out = len(ex.srcs), len(ex.out_structs)

    def body(*refs):
        ex.run(refs[:n_in], refs[n_in:n_in + n_out], refs[n_in + n_out], refs[n_in + n_out + 1], True, True)

    return pl.pallas_call(body, name=name, in_specs=ex.in_specs(), out_specs=ex.out_specs(),
                          out_shape=ex.out_structs, scratch_shapes=ex.scratch())(*ex.srcs)


def _rs_stage_c(gs):
    def plan(x, y, c, ins, outs):
        sib = (x, y, 1 - c)
        return [(g.at[q, 1 - c], r.at[q], sib) for g, r in zip(ins, outs) for q in range(4)]

    outs = [jax.ShapeDtypeStruct((4,) + g.shape[2:], g.dtype) for g in gs]
    return _Exchange(gs, outs, plan, 4 * len(gs))


def _rs_stage_ici(sends, first):
    def plan(x, y, c, ins, outs):
        XN, YN = (1 - x, y, c), (x, 1 - y, c)
        peers = (YN, XN) if first else (XN, YN)
        return [(s.at[h], r.at[h], peers[h]) for s, r in zip(ins, outs) for h in range(2)]

    outs = [jax.ShapeDtypeStruct(s.shape, s.dtype) for s in sends]
    return _Exchange(sends, outs, plan, 2 * len(sends))


def _coord_vec():
    x, y, c = _coords()
    return jnp.stack([x, y, c]).astype(jnp.int32)


def _rs_add1(name, g, r1, coords):
    R, L = g.shape[3], g.shape[4]
    tr = _tile(R, 512, 16)

    def qk(h, idx, cr):
        return jnp.where(h == 0, 2 * idx + cr[1], 2 * cr[0] + idx)

    def qs(h, idx, cr):
        return jnp.where(h == 0, 2 * idx + 1 - cr[1], 2 * (1 - cr[0]) + idx)

    def body(cr, gk, rk, gsd, rsd, keep, send):
        keep[...] = gk[...] + rk[...]
        send[...] = (gsd[...] + rsd[...]).astype(BF16)

    gspec = lambda qf: pl.BlockSpec((None, None, None, tr, L), lambda h, idx, r, cr: (qf(h, idx, cr), cr[2], h, r, 0))
    rspec = lambda qf: pl.BlockSpec((None, None, tr, L), lambda h, idx, r, cr: (qf(h, idx, cr), h, r, 0))
    ospec = pl.BlockSpec((None, None, tr, L), lambda h, idx, r, cr: (h, idx, r, 0))
    return pl.pallas_call(
        body, name=name,
        grid_spec=pltpu.PrefetchScalarGridSpec(
            num_scalar_prefetch=1, grid=(2, 2, R // tr),
            in_specs=[gspec(qk), rspec(qk), gspec(qs), rspec(qs)], out_specs=[ospec, ospec]),
        out_shape=[jax.ShapeDtypeStruct((2, 2, R, L), F32), jax.ShapeDtypeStruct((2, 2, R, L), BF16)],
        compiler_params=_cparams("parallel", "parallel", "parallel"),
    )(coords, g, r1, g, r1)


def _rs_add2(name, keep2, recv2, coords):
    R, L = keep2.shape[2], keep2.shape[3]
    tr = _tile(R, 512, 16)

    def mine(h, cr):
        return jnp.where(h == 0, cr[0], cr[1])

    def body(cr, kk, rk, ks, rs, keep, send):
        keep[...] = kk[...] + rk[...].astype(F32)
        send[...] = (ks[...] + rs[...].astype(F32)).astype(BF16)

    sel = lambda f: pl.BlockSpec((None, None, tr, L), lambda h, r, cr: (h, f(h, cr), r, 0))
    ospec = pl.BlockSpec((None, tr, L), lambda h, r, cr: (h, r, 0))
    other = lambda h, cr: 1 - mine(h, cr)
    return pl.pallas_call(
        body, name=name,
        grid_spec=pltpu.PrefetchScalarGridSpec(
            num_scalar_prefetch=1, grid=(2, R // tr),
            in_specs=[sel(mine), sel(mine), sel(other), sel(other)], out_specs=[ospec, ospec]),
        out_shape=[jax.ShapeDtypeStruct((2, R, L), F32), jax.ShapeDtypeStruct((2, R, L), BF16)],
        compiler_params=_cparams("parallel", "parallel"),
    )(coords, keep2, recv2, keep2, recv2)


def _rs_add3(name, keep3, recv3):
    R, L = keep3.shape[1], keep3.shape[2]
    tr = _tile(R, 512, 16)

    def body(k, r, o):
        o[...] = k[...] + r[...].astype(F32)

    spec = pl.BlockSpec((None, tr, L), lambda h, r: (h, r, 0))
    return pl.pallas_call(body, name=name, grid=(2, R // tr), in_specs=[spec, spec], out_specs=spec,
                          out_shape=jax.ShapeDtypeStruct((2, R, L), F32),
                          compiler_params=_cparams("parallel", "parallel"))(keep3, recv3)


class _ReduceScatter:
    N_STAGES = 3

    def __init__(self, tag, gs, coords):
        self.tag, self.coords, self.stage, self.result = tag, coords, 0, None
        self.full = []
        for g in gs:
            per = int(np.prod(g.shape[1:]))
            L = g.shape[-1]
            self.full.append(g.reshape(4, 2, 2, per // (2 * L), L))
        self.keep, self.send = None, None

    def exchange(self):
        if self.stage == 0:
            return _rs_stage_c(self.full)
        return _rs_stage_ici(list(self.send), self.stage == 1)

    def absorb(self, recv):
        names = [f"rs_add{self.stage + 1}_{self.tag}_{k}" for k in range(len(self.full))]
        if self.stage == 0:
            pairs = [_rs_add1(nm, g, r, self.coords) for nm, g, r in zip(names, self.full, recv)]
            self.keep, self.send = zip(*pairs)
        elif self.stage == 1:
            pairs = [_rs_add2(nm, kp, r, self.coords) for nm, kp, r in zip(names, self.keep, recv)]
            self.keep, self.send = zip(*pairs)
        else:
            self.result = [_rs_add3(nm, kp, r) for nm, kp, r in zip(names, self.keep, recv)]
        self.stage += 1

    def finish_now(self):
        while self.stage < self.N_STAGES:
            self.absorb(_exchange_now(f"rs_x{self.stage}_{self.tag}", self.exchange()))
        return self.result


def _adamw(name, w, g, m, v):
    R, L = w.shape
    tr = _tile(R, 256, 8)

    def body(w_ref, g_ref, m_ref, v_ref, d_ref, nm_ref, nv_ref):
        gv = g_ref[...]
        nm = ADAM_B1 * m_ref[...] + (1.0 - ADAM_B1) * gv
        nv = ADAM_B2 * v_ref[...] + (1.0 - ADAM_B2) * (gv * gv)
        m_hat = nm / (1.0 - ADAM_B1 ** ADAM_STEP)
        v_hat = nv / (1.0 - ADAM_B2 ** ADAM_STEP)
        d_ref[...] = -ADAM_LR * (m_hat / (jnp.sqrt(v_hat) + ADAM_EPS) + ADAM_WD * w_ref[...])
        nm_ref[...] = nm
        nv_ref[...] = nv

    spec = pl.BlockSpec((tr, L), lambda i: (i, 0))
    out = jax.ShapeDtypeStruct((R, L), F32)
    return pl.pallas_call(body, name=name, grid=(R // tr,), in_specs=[spec] * 4, out_specs=[spec] * 3,
                          out_shape=[out, out, out], compiler_params=_cparams("parallel"))(w, g, m, v)


def _sum_devices(name, gathered):
    _, R, L = gathered.shape
    tr = _tile(R, 512, 8)

    def body(g_ref, o_ref):
        acc = g_ref[0]
        for d in range(1, NDEV):
            acc = acc + g_ref[d]
        o_ref[...] = acc

    return pl.pallas_call(body, name=name, grid=(R // tr,),
                          in_specs=[pl.BlockSpec((NDEV, tr, L), lambda i: (0, i, 0))],
                          out_specs=pl.BlockSpec((tr, L), lambda i: (i, 0)),
                          out_shape=jax.ShapeDtypeStruct((R, L), F32),
                          compiler_params=_cparams("parallel"))(gathered)


def _pack(parts):
    flat, offs, pos = [], [], 0
    for p in parts:
        n = int(np.prod(p.shape))
        padded = -(-n // PACK_ALIGN) * PACK_ALIGN
        flat.append(jnp.pad(p.reshape(-1).astype(F32), (0, padded - n)))
        offs.append((pos, n, p.shape))
        pos += padded
    return jnp.concatenate(flat).reshape(-1, LANES), offs


def _unpack(packed, offs):
    flat = packed.reshape(-1)
    return [flat[pos:pos + n].reshape(shape) for pos, n, shape in offs]


def kernel(x, norm_mix, norm_ffn, rel_bias, attn_w_qkv, attn_q_gain, attn_k_gain, attn_sinks, attn_w_o, pool_w, pool_scale, ffn_w_up, ffn_conv_w, ffn_conv_b, ffn_w_down, loss_target, m_norm_mix, m_norm_ffn, m_rel_bias, m_attn_w_qkv, m_attn_q_gain, m_attn_k_gain, m_attn_sinks, m_attn_w_o, m_pool_w, m_pool_scale, m_ffn_w_up, m_ffn_conv_w, m_ffn_conv_b, m_ffn_w_down, v_norm_mix, v_norm_ffn, v_rel_bias, v_attn_w_qkv, v_attn_q_gain, v_attn_k_gain, v_attn_sinks, v_attn_w_o, v_pool_w, v_pool_scale, v_ffn_w_up, v_ffn_conv_w, v_ffn_conv_b, v_ffn_w_down):
    xs = x[0]
    target = loss_target[0]
    S, D = xs.shape
    depth = norm_mix.shape[0]
    H = D // HEAD_DIM
    n_attn, n_pool = attn_w_qkv.shape[0], pool_w.shape[0]
    QS = attn_w_qkv.shape[2]
    CB = ffn_w_up.shape[2]
    FB = ffn_w_down.shape[1]
    FF = FB * NDEV
    G, PC, C = pool_w.shape[1], pool_w.shape[2], pool_w.shape[3]
    xi, yi, ci = _coords()
    me = 4 * xi + 2 * yi + ci
    coords = _coord_vec()

    def ffn_allgather(i):
        return _AllGather([_cols_item(ffn_w_up[i].astype(BF16), NDEV * CB),
                           _rows_item(ffn_w_down[i].astype(BF16), FF)])

    def mixer_allgather(i):
        j = i // 2
        if i % 2 == 0:
            return _AllGather([_lead_item(attn_w_qkv[j].astype(BF16)), _rows_item(attn_w_o[j].astype(BF16), D)])
        return _AllGather([_pool_item(pool_w[j].astype(BF16), C)])

    def layer_allgather(i):
        return _AllGather(ffn_allgather(i).items + mixer_allgather(i).items)

    def unpack_ffn(outs):
        return {"up": outs[0], "down": outs[1]}

    def unpack_mixer(i, outs):
        if i % 2 == 0:
            return {"qkv": outs[0].transpose(1, 0, 2).reshape(D, NDEV * QS), "o": outs[1]}
        return {"pool": outs[0]}

    weights_of = [None] * depth
    weights_of[0] = unpack_mixer(0, _allgather_now("allgather_mixer_0", mixer_allgather(0)))

    small_in, small_in_offs = _pack([ffn_conv_w, pool_scale])
    gathered_in = _allgather_small("allgather_small_params", small_in).reshape(NDEV, -1)
    per_dev = [_unpack(gathered_in[d], small_in_offs) for d in range(NDEV)]
    conv_w_full = jnp.concatenate([p[0] for p in per_dev], axis=2)
    pool_scale_full = jnp.concatenate([p[1] for p in per_dev], axis=1)

    onehot = jnp.asarray(_t5_onehot())
    bias = _bias_band(rel_bias, onehot).reshape(H, WINDOW, 2 * WINDOW)

    saved = []
    cur = xs
    for i in range(depth):
        j = i // 2
        w = weights_of[i]
        st = {"x0": cur}
        if i % 2 == 0:
            h = _rmsnorm_fwd(f"norm_mix_{i}", cur, norm_mix[i:i + 1])
            qkv = _mm_rows(f"qkv_{i}", h, w["qkv"], NN, F32)
            qg, kg, sk = attn_q_gain[j:j + 1], attn_k_gain[j:j + 1], attn_sinks[j].reshape(H, 1)
            ag0 = ffn_allgather(0) if i == 0 else None
            o, lse, *gathered_ffn = _attn_fwd(f"attn_fwd_{i}", qkv, bias, qg, kg, sk, ag0)
            if ag0:
                w.update(unpack_ffn(gathered_ffn))
            x1 = _mm_rows(f"attn_out_{i}", o, w["o"], NN, F32, res=cur)
            st.update(h=h, qkv=qkv, o=o, lse=lse)
        else:
            x1, dsave = _pool_fwd(f"pool_fwd_{i}", cur, norm_mix[i:i + 1], w["pool"], pool_scale_full[j:j + 1])
            st.update(dsave=dsave)
        h2 = _rmsnorm_fwd(f"norm_ffn_{i}", x1, norm_ffn[i:i + 1])
        ag = layer_allgather(i + 1) if i + 1 < depth else None
        u3, ab3, a, *gathered_next = _ffn_up_fwd(f"ffn_up_{i}", h2, w["up"], conv_w_full[i], ffn_conv_b[i:i + 1], ag)
        if ag:
            weights_of[i + 1] = {**unpack_ffn(gathered_next[:2]), **unpack_mixer(i + 1, gathered_next[2:])}
        cur = _ffn_down_fwd(f"ffn_down_{i}", a, w["down"], x1)
        st.update(x1=x1, h2=h2, u3=u3, ab3=ab3, a=a)
        saved.append(st)

    loss_tile, dcur, dcur_b = _loss_fwd_bwd(cur, target)

    g_up_l, g_down_l = [None] * depth, [None] * depth
    g_qkv_l, g_o_l, g_pool_l = [None] * n_attn, [None] * n_attn, [None] * n_pool
    d_norm_mix, d_norm_ffn = [None] * depth, [None] * depth
    d_conv_w, d_conv_b = [None] * depth, [None] * depth
    d_qg, d_kg, d_sinks, d_pscale = [None] * n_attn, [None] * n_attn, [None] * n_attn, [None] * n_pool
    d_band = None

    def store_ffn(i, red):
        g_up_l[i] = red[0].reshape(D, CB)
        g_down_l[i] = red[1].reshape(FB, D)

    def store_mixer(i, red):
        if i % 2 == 0:
            g_qkv_l[i // 2] = red[0].reshape(D, QS)
            g_o_l[i // 2] = red[1].reshape(D // NDEV, D)
        else:
            g_pool_l[i // 2] = red[0].reshape(G, PC, C)

    def carrying(rs, call):
        if rs is None or rs.stage >= rs.N_STAGES:
            return call(None)
        out, *recv = call(rs.exchange())
        rs.absorb(recv)
        return out

    pending = None
    for i in reversed(range(depth)):
        j = i // 2
        st = saved[i]
        w = weights_of[i]
        du3, dc = _ffn_da_bwd(f"ffn_da_{i}", dcur_b, w["down"], st["u3"], st["ab3"], conv_w_full[i])
        d_conv_w[i] = jnp.concatenate([dc[0, 0:3], dc[1, 0:3]], axis=1)
        d_conv_b[i] = jnp.concatenate([dc[0, 3], dc[1, 3]], axis=0)
        dwdown = carrying(pending, lambda ex: _mm_tn(f"ffn_dwdown_{i}", st["a"], dcur_b,
                                                     tm_pref=CB if CB % LANES == 0 else 1024, comm=ex))
        dwdown = dwdown.reshape(NDEV, FB, D)
        dwup = carrying(pending, lambda ex: _ffn_dwup_bwd(f"ffn_dwup_{i}", st["h2"], du3, comm=ex))
        dh2 = carrying(pending, lambda ex: _ffn_dh_bwd(f"ffn_dh_{i}", du3, w["up"], comm=ex))
        if pending is not None:
            red = pending.finish_now()
            store_ffn(i + 1, red[:2])
            store_mixer(i + 1, red[2:])
        dx1, dx1_b, dg = _rmsnorm_bwd(f"norm_ffn_bwd_{i}", dh2, st["x1"], norm_ffn[i:i + 1], dcur)
        d_norm_ffn[i] = dg[0]
        own = _ReduceScatter(f"l{i}f", [dwup, dwdown], coords) if i == 0 else None
        if i % 2 == 0:
            do = _mm_rows(f"attn_do_{i}", dx1_b, w["o"], NT, BF16)
            dwo = carrying(own, lambda ex: _mm_tn(f"attn_dwo_{i}", st["o"], dx1_b, comm=ex)).reshape(NDEV, D // NDEV, D)
            qg, kg, sk = attn_q_gain[j:j + 1], attn_k_gain[j:j + 1], attn_sinks[j].reshape(H, 1)
            ex = own.exchange() if own is not None and own.stage < own.N_STAGES else None
            dq, dkv, db, dsk, dqg, dkg, *recv = _attn_bwd(f"attn_bwd_{i}", st["qkv"], do, st["lse"], bias, qg, kg, sk,
                                                          comm=ex)
            if ex is not None:
                own.absorb(recv)
            d_band = db if d_band is None else d_band + db
            d_sinks[j], d_qg[j], d_kg[j] = dsk[:, 0], dqg[0], dkg[0]
            dqkv = jnp.concatenate([dq, dkv], axis=1)
            dwqkv = carrying(own, lambda ex: _mm_tn(f"attn_dwqkv_{i}", st["h"], dqkv, tn_pref=1280, comm=ex))
            dwqkv = dwqkv.reshape(D, NDEV, QS).transpose(1, 0, 2)
            dh = _mm_rows(f"attn_dh_{i}", dqkv, w["qkv"], NT, F32)
            dcur, dcur_b, dg = _rmsnorm_bwd(f"norm_mix_bwd_{i}", dh, st["x0"], norm_mix[i:i + 1], dx1)
            mixer_grads = [dwqkv, dwo]
        else:
            dcur, dcur_b, dwp, dps, dg = _pool_bwd(f"pool_bwd_{i}", dx1, st["x0"], norm_mix[i:i + 1], w["pool"],
                                                   pool_scale_full[j:j + 1], st["dsave"])
            d_pscale[j] = dps[0]
            mixer_grads = [dwp.reshape(G, NDEV, PC, C).transpose(1, 0, 2, 3)]
        d_norm_mix[i] = dg[0]
        if own is not None:
            store_ffn(i, own.finish_now())
            store_mixer(i, _ReduceScatter(f"l{i}m", mixer_grads, coords).finish_now())
        else:
            pending = _ReduceScatter(f"l{i}", [dwup, dwdown] + mixer_grads, coords)

    d_rel = _bias_band_bwd(d_band.reshape(H, -1), onehot)

    small_parts = [loss_tile, jnp.stack(d_norm_mix), jnp.stack(d_norm_ffn), d_rel, jnp.stack(d_qg),
                   jnp.stack(d_kg), jnp.stack(d_sinks), jnp.stack(d_pscale), jnp.stack(d_conv_w),
                   jnp.stack(d_conv_b)]
    small, small_offs = _pack(small_parts)
    gathered = _allgather_small("allgather_small_grads", small).reshape(NDEV, -1, LANES)
    summed = _unpack(_sum_devices("sum_small_grads", gathered), small_offs)
    loss = summed[0][0, 0]
    (g_norm_mix, g_norm_ffn, g_rel, g_qg, g_kg, g_sinks, g_pscale_full, g_conv_w_full, g_conv_b) = summed[1:]
    g_pscale = lax.dynamic_slice_in_dim(g_pscale_full, me * (D // NDEV), D // NDEV, axis=1)
    g_conv_w = lax.dynamic_slice_in_dim(g_conv_w_full, me * CB, CB, axis=2)

    grads = {
        "norm_mix": g_norm_mix, "norm_ffn": g_norm_ffn, "rel_bias": g_rel, "attn_w_qkv": jnp.stack(g_qkv_l),
        "attn_q_gain": g_qg, "attn_k_gain": g_kg, "attn_sinks": g_sinks, "attn_w_o": jnp.stack(g_o_l),
        "pool_w": jnp.stack(g_pool_l), "pool_scale": g_pscale, "ffn_w_up": jnp.stack(g_up_l),
        "ffn_conv_w": g_conv_w, "ffn_conv_b": g_conv_b, "ffn_w_down": jnp.stack(g_down_l),
    }
    weights = {
        "norm_mix": (norm_mix, m_norm_mix, v_norm_mix), "norm_ffn": (norm_ffn, m_norm_ffn, v_norm_ffn),
        "rel_bias": (rel_bias, m_rel_bias, v_rel_bias), "attn_w_qkv": (attn_w_qkv, m_attn_w_qkv, v_attn_w_qkv),
        "attn_q_gain": (attn_q_gain, m_attn_q_gain, v_attn_q_gain),
        "attn_k_gain": (attn_k_gain, m_attn_k_gain, v_attn_k_gain),
        "attn_sinks": (attn_sinks, m_attn_sinks, v_attn_sinks), "attn_w_o": (attn_w_o, m_attn_w_o, v_attn_w_o),
        "pool_w": (pool_w, m_pool_w, v_pool_w), "pool_scale": (pool_scale, m_pool_scale, v_pool_scale),
        "ffn_w_up": (ffn_w_up, m_ffn_w_up, v_ffn_w_up), "ffn_conv_w": (ffn_conv_w, m_ffn_conv_w, v_ffn_conv_w),
        "ffn_conv_b": (ffn_conv_b, m_ffn_conv_b, v_ffn_conv_b), "ffn_w_down": (ffn_w_down, m_ffn_w_down, v_ffn_w_down),
    }
    names = list(weights)
    big = ("attn_w_qkv", "attn_w_o", "pool_w", "ffn_w_up", "ffn_w_down")
    upd = {}
    for nm in big:
        w, m, v = weights[nm]
        two_d = lambda t: t.reshape(-1, w.shape[-1])
        d_, m_, v_ = _adamw(f"adamw_{nm}", two_d(w), two_d(grads[nm]), two_d(m), two_d(v))
        upd[nm] = (d_.reshape(w.shape), m_.reshape(w.shape), v_.reshape(w.shape))
    small_names = [nm for nm in names if nm not in big]
    pw, offs = _pack([weights[nm][0] for nm in small_names])
    pg, _ = _pack([grads[nm] for nm in small_names])
    pm, _ = _pack([weights[nm][1] for nm in small_names])
    pv, _ = _pack([weights[nm][2] for nm in small_names])
    d_, m_, v_ = _adamw("adamw_small", pw, pg, pm, pv)
    for nm, dd, mm, vv in zip(small_names, _unpack(d_, offs), _unpack(m_, offs), _unpack(v_, offs)):
        upd[nm] = (dd, mm, vv)

    grad_x = dcur[None]
    return (loss, grad_x, *[grads[nm].reshape(weights[nm][0].shape) for nm in names],
            *[upd[nm][0] for nm in names], *[upd[nm][1] for nm in names], *[upd[nm][2] for nm in names])
```

```python
import numpy as np
import jax
import jax.numpy as jnp
from jax import lax
from jax.experimental import pallas as pl
from jax.experimental.pallas import tpu as pltpu

F32 = jnp.float32
BF16 = jnp.bfloat16
MESH = pl.DeviceIdType.MESH

NDEV = 8
HEAD_DIM = 64
GQA_GROUP = 8
WINDOW = 128
N_BUCKETS = 32
MAX_DISTANCE = 128
POOL_WINDOWS = (2, 4, 8, 16)
POOL_HALO = 32
EPS = 1e-6
NEG_INF = -1e30
ADAM_LR = 0.001
ADAM_B1 = 0.9
ADAM_B2 = 0.999
ADAM_EPS = 1e-08
ADAM_WD = 0.01
ADAM_STEP = 10

V7X_VMEM_BYTES = 64 * 1024 * 1024
VMEM_LIMIT = V7X_VMEM_BYTES - 8 * 1024 * 1024
LANES = 128
SUBLANES = 8
PACK_ALIGN = SUBLANES * LANES

NN = (((1,), (0,)), ((), ()))
NT = (((1,), (1,)), ((), ()))
TN = (((0,), (0,)), ((), ()))


def _tile(dim, pref, align):
    t = min(pref, dim)
    t -= t % align
    while t >= align:
        if dim % t == 0:
            return t
        t -= align
    return dim


def _cparams(*sem):
    return pltpu.CompilerParams(dimension_semantics=sem, vmem_limit_bytes=VMEM_LIMIT)


def _bf(v):
    return v if v.dtype == BF16 else v.astype(BF16)


class _Exchange:
    def __init__(self, srcs, out_structs, plan, n_copies):
        self.srcs, self.out_structs, self.plan, self.n_copies = list(srcs), list(out_structs), plan, n_copies

    def in_specs(self):
        return [HBM_SPEC] * len(self.srcs)

    def out_specs(self):
        return [HBM_SPEC] * len(self.out_structs)

    def scratch(self):
        return [pltpu.SemaphoreType.DMA((self.n_copies,)), pltpu.SemaphoreType.DMA((self.n_copies,))]

    def run(self, in_refs, out_refs, send_sems, recv_sems, is_first, is_last):
        def copies():
            x, y, c = _coords()
            return [pltpu.make_async_remote_copy(src_ref=src, dst_ref=dst, send_sem=send_sems.at[k],
                                                 recv_sem=recv_sems.at[k], device_id=peer, device_id_type=MESH)
                    for k, (src, dst, peer) in enumerate(self.plan(x, y, c, in_refs, out_refs))]

        def start():
            for cp in copies():
                cp.start()

        def finish():
            cps = copies()
            for cp in cps:
                cp.wait_recv()
            for cp in cps:
                cp.wait_send()

        if is_first is True and is_last is True:
            start()
            finish()
        else:
            pl.when(is_first)(start)
            pl.when(is_last)(finish)


def _grid_edges(grid):
    first, last = True, True
    for ax, n in enumerate(grid):
        first = jnp.logical_and(first, pl.program_id(ax) == 0)
        last = jnp.logical_and(last, pl.program_id(ax) == n - 1)
    return first, last


def _mm(name, a, b, *, grid, a_spec, b_spec, o_spec, out_shape, contract, acc_shape, res=None, comm=None):
    nk = grid[2]
    n_main = 3 if res is not None else 2
    n_ci = len(comm.srcs) if comm else 0
    n_co = len(comm.out_structs) if comm else 0

    def body(*refs):
        a_ref, b_ref = refs[:2]
        r_ref = refs[2] if res is not None else None
        o_ref = refs[n_main + n_ci]
        scr = refs[n_main + n_ci + 1 + n_co:]
        if comm:
            first, last = _grid_edges(grid)
            comm.run(refs[n_main:n_main + n_ci], refs[n_main + n_ci + 1:n_main + n_ci + 1 + n_co],
                     scr[-2], scr[-1], first, last)
        part = lax.dot_general(_bf(a_ref[...]), _bf(b_ref[...]), contract, preferred_element_type=F32)

        def finish(acc):
            if r_ref is not None:
                acc = acc + r_ref[...]
            o_ref[...] = acc.astype(o_ref.dtype)

        if nk == 1:
            finish(part)
        else:
            acc_ref = scr[0]
            k = pl.program_id(2)

            @pl.when(k == 0)
            def _():
                acc_ref[...] = part

            @pl.when(k > 0)
            def _():
                acc_ref[...] += part

            @pl.when(k == nk - 1)
            def _():
                finish(acc_ref[...])

    in_specs = [a_spec, b_spec] + ([o_spec] if res is not None else [])
    args = (a, b) + ((res,) if res is not None else ())
    out_specs, out_shapes = o_spec, out_shape
    scratch = [pltpu.VMEM(acc_shape, F32)] if nk > 1 else []
    if comm:
        in_specs += comm.in_specs()
        args += tuple(comm.srcs)
        out_specs = [o_spec] + comm.out_specs()
        out_shapes = [out_shape] + comm.out_structs
        scratch += comm.scratch()
    return pl.pallas_call(
        body, name=name, grid=grid, in_specs=in_specs, out_specs=out_specs, out_shape=out_shapes,
        scratch_shapes=scratch,
        compiler_params=_cparams(*(("arbitrary",) * 3 if comm else ("parallel", "parallel", "arbitrary"))),
    )(*args)


def _mm_rows(name, a, b, contract, out_dtype, res=None, tm_pref=512, comm=None):
    S, K = a.shape
    N = b.shape[1] if contract == NN else b.shape[0]
    tm = _tile(S, tm_pref, 16)
    return _mm(name, a, b, grid=(1, S // tm, 1),
               a_spec=pl.BlockSpec((tm, K), lambda p, q, k: (q, 0)),
               b_spec=pl.BlockSpec(b.shape, lambda p, q, k: (0, 0)),
               o_spec=pl.BlockSpec((tm, N), lambda p, q, k: (q, 0)),
               out_shape=jax.ShapeDtypeStruct((S, N), out_dtype), contract=contract,
               acc_shape=(tm, N), res=res, comm=comm)


def _mm_tn(name, a, b, tm_pref=1024, tn_pref=1024, tk_pref=2048, comm=None):
    S, M = a.shape
    N = b.shape[1]
    tm, tn, tk = _tile(M, tm_pref, LANES), _tile(N, tn_pref, LANES), _tile(S, tk_pref, 16)
    return _mm(name, a, b, grid=(M // tm, N // tn, S // tk),
               a_spec=pl.BlockSpec((tk, tm), lambda p, q, k: (k, p)),
               b_spec=pl.BlockSpec((tk, tn), lambda p, q, k: (k, q)),
               o_spec=pl.BlockSpec((tm, tn), lambda p, q, k: (p, q)),
               out_shape=jax.ShapeDtypeStruct((M, N), F32), contract=TN, acc_shape=(tm, tn), comm=comm)


def _rmsnorm_fwd(name, x, gain):
    S, D = x.shape
    tm = _tile(S, 512, 16)

    def body(x_ref, g_ref, o_ref):
        xf = x_ref[...]
        r = lax.rsqrt(jnp.mean(xf * xf, axis=-1, keepdims=True) + EPS)
        o_ref[...] = (xf * r * g_ref[...]).astype(o_ref.dtype)

    return pl.pallas_call(
        body, name=name, grid=(S // tm,),
        in_specs=[pl.BlockSpec((tm, D), lambda i: (i, 0)), pl.BlockSpec((1, D), lambda i: (0, 0))],
        out_specs=pl.BlockSpec((tm, D), lambda i: (i, 0)),
        out_shape=jax.ShapeDtypeStruct((S, D), BF16), compiler_params=_cparams("parallel"),
    )(x, gain)


def _rms_bwd_math(dh, xf, gain):
    r = lax.rsqrt(jnp.mean(xf * xf, axis=-1, keepdims=True) + EPS)
    xhat = xf * r
    dxh = dh * gain
    dx = r * (dxh - xhat * jnp.mean(dxh * xhat, axis=-1, keepdims=True))
    return dx, jnp.sum(dh * xhat, axis=0, keepdims=True)


def _rmsnorm_bwd(name, dh, x, gain, dres):
    S, D = x.shape
    tm = _tile(S, 256, 16)

    def body(dh_ref, x_ref, g_ref, dr_ref, dx_ref, dxb_ref, dg_ref):
        dx, dg = _rms_bwd_math(dh_ref[...], x_ref[...], g_ref[...])
        dx = dr_ref[...] + dx
        dx_ref[...] = dx
        dxb_ref[...] = dx.astype(BF16)

        @pl.when(pl.program_id(0) == 0)
        def _():
            dg_ref[...] = dg

        @pl.when(pl.program_id(0) > 0)
        def _():
            dg_ref[...] += dg

    row = pl.BlockSpec((tm, D), lambda i: (i, 0))
    vec = pl.BlockSpec((1, D), lambda i: (0, 0))
    return pl.pallas_call(
        body, name=name, grid=(S // tm,), in_specs=[row, row, vec, row], out_specs=[row, row, vec],
        out_shape=[jax.ShapeDtypeStruct((S, D), F32), jax.ShapeDtypeStruct((S, D), BF16),
                   jax.ShapeDtypeStruct((1, D), F32)],
        compiler_params=_cparams("arbitrary"),
    )(dh, x, gain, dres)


def _loss_fwd_bwd(y, target):
    S, D = y.shape
    tm = _tile(S, 512, 16)

    def body(y_ref, t_ref, l_ref, dy_ref, dyb_ref):
        e = y_ref[...] - t_ref[...]
        dy = e * (1.0 / D)
        dy_ref[...] = dy
        dyb_ref[...] = dy.astype(BF16)
        part = 0.5 * jnp.sum(jnp.mean(e * e, axis=-1, keepdims=True), axis=0, keepdims=True)
        part = jnp.broadcast_to(part, (SUBLANES, LANES))

        @pl.when(pl.program_id(0) == 0)
        def _():
            l_ref[...] = part

        @pl.when(pl.program_id(0) > 0)
        def _():
            l_ref[...] += part

    row = pl.BlockSpec((tm, D), lambda i: (i, 0))
    return pl.pallas_call(
        body, name="loss", grid=(S // tm,), in_specs=[row, row],
        out_specs=[pl.BlockSpec((SUBLANES, LANES), lambda i: (0, 0)), row, row],
        out_shape=[jax.ShapeDtypeStruct((SUBLANES, LANES), F32), jax.ShapeDtypeStruct((S, D), F32),
                   jax.ShapeDtypeStruct((S, D), BF16)],
        compiler_params=_cparams("arbitrary"),
    )(y, target)


def _sigmoid(v):
    return 1.0 / (1.0 + jnp.exp(-v))


MXU_COLS = 256


def _col_chunks(n):
    return [slice(c, min(c + MXU_COLS, n)) for c in range(0, n, MXU_COLS)]


def _shift_rows(v, k, edge8, down):
    tm = v.shape[0]
    sub = lax.broadcasted_iota(jnp.int32, edge8.shape, 0)
    if down:
        r = pltpu.roll(v, k, axis=0)
        head = jnp.where(sub < k, pltpu.roll(edge8, k, axis=0), r[0:8, :])
        return jnp.concatenate([head, r[8:, :]], axis=0)
    r = pltpu.roll(v, tm - k, axis=0)
    tail = jnp.where(sub >= 8 - k, pltpu.roll(edge8, 8 - k, axis=0), r[tm - 8:tm, :])
    return jnp.concatenate([r[:tm - 8, :], tail], axis=0)


def _ffn_up_fwd(name, h2, wup, cw, cb, ag=None):
    S, D = h2.shape
    FF = wup.shape[1] // 2
    CB = _tile(FF, 1408, LANES)
    NJ = FF // CB
    tm = _tile(S, 512, 16)
    nI = S // tm
    n_steps = NJ * nI
    n_ag = len(ag.items) if ag else 0
    schedule = (0, (9 * n_steps) // 20, (7 * n_steps) // 10, n_steps - 1)

    def body(*refs):
        h_ref, wg_ref, wv_ref, cwg_ref, cwv_ref, cbg_ref, cbv_ref = refs[:7]
        u_ref, ab_ref, a_ref = refs[7 + n_ag:10 + n_ag]
        edge_g, edge_v = refs[10 + 2 * n_ag:12 + 2 * n_ag]
        if ag:
            ag.run_at(pl.program_id(0) * nI + pl.program_id(1), schedule, refs[7:7 + n_ag],
                      refs[10 + n_ag:10 + 2 * n_ag], *refs[12 + 2 * n_ag:])

        @pl.when(pl.program_id(1) == 0)
        def _():
            edge_g[...] = jnp.zeros((8, CB), F32)
            edge_v[...] = jnp.zeros((8, CB), F32)

        h = h_ref[...]

        def conv(w_ref, cw_ref, cb_ref, edge, slot, cs):
            u = jnp.dot(h, w_ref[:, cs], preferred_element_type=F32)
            u_ref[slot, :, cs] = u.astype(BF16)
            prev8 = edge[:, cs]
            uc = (cw_ref[0:1, cs] * _shift_rows(u, 2, prev8, True) + cw_ref[1:2, cs] * _shift_rows(u, 1, prev8, True)
                  + cw_ref[2:3, cs] * u + cb_ref[:, cs])
            edge[:, cs] = u[tm - 8:tm, :]
            return uc

        cs = slice(0, CB)
        gc = conv(wg_ref, cwg_ref, cbg_ref, edge_g, 0, cs)
        vc = conv(wv_ref, cwv_ref, cbv_ref, edge_v, 1, cs)
        sig = _sigmoid(gc)
        silu = gc * sig
        a_ref[...] = (silu * vc).astype(BF16)
        ab_ref[0] = (vc * (sig * (1.0 + gc * (1.0 - sig)))).astype(BF16)
        ab_ref[1] = silu.astype(BF16)

    def wspec(off):
        return pl.BlockSpec((D, CB), lambda j, i: (0, j + off))

    def cspec(rows, off):
        return pl.BlockSpec((rows, CB), lambda j, i: (0, j + off))

    pair = pl.BlockSpec((2, tm, CB), lambda j, i: (0, i, j))
    in_specs = [pl.BlockSpec((tm, D), lambda j, i: (i, 0)), wspec(0), wspec(NJ),
                cspec(3, 0), cspec(3, NJ), cspec(1, 0), cspec(1, NJ)]
    out_specs = [pair, pair, pl.BlockSpec((tm, CB), lambda j, i: (i, j))]
    out_shape = [jax.ShapeDtypeStruct((2, S, FF), BF16), jax.ShapeDtypeStruct((2, S, FF), BF16),
                 jax.ShapeDtypeStruct((S, FF), BF16)]
    scratch = [pltpu.VMEM((8, CB), F32), pltpu.VMEM((8, CB), F32)]
    args = (h2, wup, wup, cw, cw, cb, cb)
    if ag:
        in_specs += ag.in_specs()
        out_specs += ag.out_specs()
        out_shape += ag.out_structs()
        scratch += ag.scratch()
        args += tuple(ag.srcs())
    return pl.pallas_call(
        body, name=name, grid=(NJ, nI), in_specs=in_specs, out_specs=out_specs, out_shape=out_shape,
        scratch_shapes=scratch, compiler_params=_cparams("arbitrary", "arbitrary"),
    )(*args)


def _ffn_da_bwd(name, dyb, wd, u3, ab3, cw):
    S, D = dyb.shape
    FF = wd.shape[0]
    CB = _tile(FF, 1408, LANES)
    NJ = FF // CB
    tm = _tile(S, 256, 16)
    nI = S // tm

    def body(dy_ref, wd_ref, u_ref, ab_ref, cwg_ref, cwv_ref, du_ref, dc_ref, edge_g, edge_v):
        i = pl.program_id(1)

        @pl.when(i == 0)
        def _():
            edge_g[...] = jnp.zeros((8, CB), F32)
            edge_v[...] = jnp.zeros((8, CB), F32)
            dc_ref[...] = jnp.zeros(dc_ref.shape, F32)

        dy = dy_ref[...]

        def back(slot, d_uc, edge, cw_ref, cs):
            u = u_ref[slot, :, cs].astype(F32)
            next8 = edge[:, cs]
            dp1 = _shift_rows(d_uc, 1, next8, False)
            dp2 = _shift_rows(d_uc, 2, next8, False)
            du = cw_ref[2:3, cs] * d_uc + cw_ref[1:2, cs] * dp1 + cw_ref[0:1, cs] * dp2
            edge[:, cs] = d_uc[0:8, :]
            du_ref[slot, :, cs] = du.astype(BF16)
            dc_ref[slot, 0:1, cs] += jnp.sum(dp2 * u, axis=0, keepdims=True)
            dc_ref[slot, 1:2, cs] += jnp.sum(dp1 * u, axis=0, keepdims=True)
            dc_ref[slot, 2:3, cs] += jnp.sum(d_uc * u, axis=0, keepdims=True)
            dc_ref[slot, 3:4, cs] += jnp.sum(d_uc, axis=0, keepdims=True)

        for cs in _col_chunks(CB):
            da = lax.dot_general(dy, wd_ref[cs, :], NT, preferred_element_type=F32)
            back(0, da * ab_ref[0, :, cs].astype(F32), edge_g, cwg_ref, cs)
            back(1, da * ab_ref[1, :, cs].astype(F32), edge_v, cwv_ref, cs)

    def rev(i):
        return nI - 1 - i

    pair = pl.BlockSpec((2, tm, CB), lambda j, i: (0, rev(i), j))
    return pl.pallas_call(
        body, name=name, grid=(NJ, nI),
        in_specs=[pl.BlockSpec((tm, D), lambda j, i: (rev(i), 0)),
                  pl.BlockSpec((CB, D), lambda j, i: (j, 0)), pair, pair,
                  pl.BlockSpec((3, CB), lambda j, i: (0, j)), pl.BlockSpec((3, CB), lambda j, i: (0, j + NJ))],
        out_specs=[pair, pl.BlockSpec((2, 8, CB), lambda j, i: (0, 0, j))],
        out_shape=[jax.ShapeDtypeStruct((2, S, FF), BF16), jax.ShapeDtypeStruct((2, 8, FF), F32)],
        scratch_shapes=[pltpu.VMEM((8, CB), F32) for _ in range(2)],
        compiler_params=_cparams("parallel", "arbitrary"),
    )(dyb, wd, u3, ab3, cw, cw)


def _ffn_down_fwd(name, a, wd, res):
    S, FF = a.shape
    D = wd.shape[1]
    tm = _tile(S, 512, 16)
    tk = _tile(FF, 2816, LANES)
    return _mm(name, a, wd, grid=(1, S // tm, FF // tk),
               a_spec=pl.BlockSpec((tm, tk), lambda p, q, k: (q, k)),
               b_spec=pl.BlockSpec((tk, D), lambda p, q, k: (k, 0)),
               o_spec=pl.BlockSpec((tm, D), lambda p, q, k: (q, 0)),
               out_shape=jax.ShapeDtypeStruct((S, D), F32), contract=NN, acc_shape=(tm, D), res=res)


def _ffn_dh_bwd(name, du3, wup, comm=None):
    _, S, FF = du3.shape
    D = wup.shape[0]
    tm = _tile(S, 512, 16)
    tk = _tile(FF, 2816, LANES)
    nh = FF // tk
    return _mm(name, du3, wup, grid=(1, S // tm, 2 * nh),
               a_spec=pl.BlockSpec((None, tm, tk), lambda p, q, k: (k // nh, q, k % nh)),
               b_spec=pl.BlockSpec((D, tk), lambda p, q, k: (0, k)),
               o_spec=pl.BlockSpec((tm, D), lambda p, q, k: (q, 0)),
               out_shape=jax.ShapeDtypeStruct((S, D), F32), contract=NT, acc_shape=(tm, D), comm=comm)


def _ffn_dwup_bwd(name, h2, du3, comm=None):
    S, D = h2.shape
    FF = du3.shape[2]
    NJ = NDEV // 2
    CB = FF // NJ
    tm, tk = _tile(D, 1024, LANES), _tile(S, 2048, 16)
    return _mm(name, h2, du3, grid=(NDEV, D // tm, S // tk),
               a_spec=pl.BlockSpec((tk, tm), lambda p, q, k: (k, q)),
               b_spec=pl.BlockSpec((None, tk, CB), lambda p, q, k: (p // NJ, k, p % NJ)),
               o_spec=pl.BlockSpec((None, tm, CB), lambda p, q, k: (p, q, 0)),
               out_shape=jax.ShapeDtypeStruct((NDEV, D, CB), F32), contract=TN, acc_shape=(tm, CB), comm=comm)


def _t5_onehot():
    i = np.arange(WINDOW)[:, None]
    j = np.arange(2 * WINDOW)[None, :]
    n = np.maximum(WINDOW + i - j, 0)
    max_exact = N_BUCKETS // 2
    nf = np.maximum(n, 1).astype(np.float32)
    large = max_exact + (np.log(nf / max_exact) / np.log(MAX_DISTANCE / max_exact)
                         * (N_BUCKETS - max_exact)).astype(np.int32)
    large = np.minimum(large, N_BUCKETS - 1)
    bucket = np.where(n < max_exact, n, large).astype(np.int32).reshape(-1)
    return (np.arange(N_BUCKETS)[:, None] == bucket[None, :]).astype(np.float32)


def _bias_band(rel_bias, onehot):
    H = rel_bias.shape[0]
    n = onehot.shape[1]

    def body(r_ref, oh_ref, o_ref):
        o_ref[...] = jnp.dot(r_ref[...], oh_ref[...], preferred_element_type=F32,
                             precision=lax.Precision.HIGHEST)

    return pl.pallas_call(body, name="bias_band", out_shape=jax.ShapeDtypeStruct((H, n), F32),
                          compiler_params=pltpu.CompilerParams(vmem_limit_bytes=VMEM_LIMIT))(rel_bias, onehot)


def _bias_band_bwd(dband, onehot):
    H = dband.shape[0]

    def body(d_ref, oh_ref, o_ref):
        o_ref[...] = lax.dot_general(d_ref[...], oh_ref[...], NT, preferred_element_type=F32,
                                     precision=lax.Precision.HIGHEST)

    return pl.pallas_call(body, name="bias_band_bwd", out_shape=jax.ShapeDtypeStruct((H, N_BUCKETS), F32),
                          compiler_params=pltpu.CompilerParams(vmem_limit_bytes=VMEM_LIMIT))(dband, onehot)


def _band_valid(n):
    i = lax.broadcasted_iota(jnp.int32, (WINDOW, 2 * WINDOW), 0)
    j = lax.broadcasted_iota(jnp.int32, (WINDOW, 2 * WINDOW), 1)
    return (j > i) & (j <= i + WINDOW) & ((n > 0) | (j >= WINDOW))


def _head_norm(v, gain):
    r = lax.rsqrt(jnp.mean(v * v, axis=-1, keepdims=True) + EPS)
    vhat = v * r
    return r, vhat, vhat * gain


HEAD_SUM_COLS = 256


def _head_sum_matrices():
    blk = np.arange(HEAD_SUM_COLS) // HEAD_DIM
    bd = (blk[:, None] == blk[None, :]).astype(np.float32)
    return jnp.asarray(bd, BF16), jnp.ones((2 * WINDOW, LANES), BF16)


def _head_sums(v, bd_ref):
    hi = v.astype(BF16)
    lo = (v - hi.astype(F32)).astype(BF16)
    bd = bd_ref[...]
    parts = []
    for c in range(0, v.shape[1], HEAD_SUM_COLS):
        cs = slice(c, c + HEAD_SUM_COLS)
        parts.append(jnp.dot(hi[:, cs], bd, preferred_element_type=F32) + jnp.dot(lo[:, cs], bd, preferred_element_type=F32))
    return jnp.concatenate(parts, axis=1)


def _stack_heads(t, kh):
    return jnp.concatenate([t[:, (kh * GQA_GROUP + g) * HEAD_DIM:(kh * GQA_GROUP + g + 1) * HEAD_DIM]
                            for g in range(GQA_GROUP)], axis=0)


def _unstack_heads(t8):
    return jnp.concatenate([t8[g * WINDOW:(g + 1) * WINDOW, :] for g in range(GQA_GROUP)], axis=1)


def _band_scores(qn, kn, b_ref, kh, valid, scale):
    s = lax.dot_general(qn, kn, NT, preferred_element_type=F32) * scale
    s = s.reshape(GQA_GROUP, WINDOW, 2 * WINDOW) + b_ref[kh * GQA_GROUP:(kh + 1) * GQA_GROUP]
    s = jnp.where(valid[None], s, NEG_INF)
    return s.reshape(GQA_GROUP * WINDOW, 2 * WINDOW)


def _sink_rows(s_ref, kh):
    return jnp.concatenate([jnp.broadcast_to(s_ref[kh * GQA_GROUP + g:kh * GQA_GROUP + g + 1, :], (WINDOW, 1))
                            for g in range(GQA_GROUP)], axis=0)


def _attn_fwd(name, qkv, bias, qg, kg, sinks, ag=None):
    S, QW = qkv.shape
    H = bias.shape[0]
    D = H * HEAD_DIM
    KV = H // GQA_GROUP
    kvw = QW - D
    kvb = D // kvw
    nb = S // WINDOW
    scale = HEAD_DIM ** -0.5
    n_ag = len(ag.items) if ag else 0
    schedule = (0, (9 * nb) // 20, (7 * nb) // 10, nb - 1)
    N_IN = 9
    bd, ones = _head_sum_matrices()

    def body(*refs):
        q_ref, kc_ref, kp_ref, b_ref, qg_ref, kg_ref, s_ref, bd_ref, ones_ref = refs[:N_IN]
        o_ref, l_ref = refs[N_IN + n_ag:N_IN + 2 + n_ag]
        n = pl.program_id(0)
        if ag:
            ag.run_at(n, schedule, refs[N_IN:N_IN + n_ag], refs[N_IN + 2 + n_ag:N_IN + 2 + 2 * n_ag],
                      *refs[N_IN + 2 + 2 * n_ag:])
        valid = _band_valid(n)
        q = q_ref[...]
        kvc = kc_ref[...]
        kvp = kp_ref[...]
        rq = lax.rsqrt(_head_sums(q * q, bd_ref) * (1.0 / HEAD_DIM) + EPS)
        qn_all = (q * rq * qg_ref[...]).astype(BF16)
        lane = lax.broadcasted_iota(jnp.int32, (WINDOW, H), 1)
        lse_all = jnp.zeros((WINDOW, H), F32)
        scores, vbs = [], []
        for kh in range(KV):
            ks = slice(kh * HEAD_DIM, (kh + 1) * HEAD_DIM)
            vs = slice((KV + kh) * HEAD_DIM, (KV + kh + 1) * HEAD_DIM)
            kb = jnp.concatenate([kvp[:, ks], kvc[:, ks]], axis=0)
            vb = jnp.concatenate([kvp[:, vs], kvc[:, vs]], axis=0).astype(BF16)
            kn = _head_norm(kb, kg_ref[...])[2].astype(BF16)
            vbs.append(vb)
            scores.append(_band_scores(_stack_heads(qn_all, kh), kn, b_ref, kh, valid, scale))
        sinks_ = [_sink_rows(s_ref, kh) for kh in range(KV)]
        ms = [jnp.maximum(jnp.max(s, axis=-1, keepdims=True), sk) for s, sk in zip(scores, sinks_)]
        ps = [jnp.exp(s - m).astype(BF16) for s, m in zip(scores, ms)]
        dens = [jnp.dot(p, ones_ref[...], preferred_element_type=F32)[:, :HEAD_DIM] + jnp.exp(sk - m)
                for p, sk, m in zip(ps, sinks_, ms)]
        outs = [_unstack_heads(jnp.dot(p, vb, preferred_element_type=F32) * (1.0 / den))
                for p, vb, den in zip(ps, vbs, dens)]
        for kh in range(KV):
            lse = ms[kh] + jnp.log(dens[kh][:, 0:1])
            for g in range(GQA_GROUP):
                lse_all = jnp.where(lane == kh * GQA_GROUP + g, lse[g * WINDOW:(g + 1) * WINDOW, :], lse_all)
        o_ref[...] = jnp.concatenate(outs, axis=1).astype(BF16)
        l_ref[...] = lse_all

    const2 = lambda shape: pl.BlockSpec(shape, lambda n: (0, 0))
    in_specs = [pl.BlockSpec((WINDOW, D), lambda n: (n, 0)),
                pl.BlockSpec((WINDOW, kvw), lambda n: (n, kvb)),
                pl.BlockSpec((WINDOW, kvw), lambda n: (jnp.maximum(n - 1, 0), kvb)),
                pl.BlockSpec(bias.shape, lambda n: (0, 0, 0)),
                const2((1, D)), const2((1, HEAD_DIM)), const2((H, 1)), const2(bd.shape), const2(ones.shape)]
    out_specs = [pl.BlockSpec((WINDOW, D), lambda n: (n, 0)), pl.BlockSpec((WINDOW, H), lambda n: (n, 0))]
    out_shape = [jax.ShapeDtypeStruct((S, D), BF16), jax.ShapeDtypeStruct((S, H), F32)]
    args = (qkv, qkv, qkv, bias, jnp.tile(qg, (1, H)), kg, sinks, bd, ones)
    scratch = []
    if ag:
        in_specs += ag.in_specs()
        out_specs += ag.out_specs()
        out_shape += ag.out_structs()
        scratch += ag.scratch()
        args += tuple(ag.srcs())
    return pl.pallas_call(
        body, name=name, grid=(nb,), in_specs=in_specs, out_specs=out_specs, out_shape=out_shape,
        scratch_shapes=scratch, compiler_params=_cparams("arbitrary" if ag else "parallel"),
    )(*args)


def _attn_bwd(name, qkv, do, lse, bias, qg, kg, sinks, comm=None):
    S, QW = qkv.shape
    H = bias.shape[0]
    D = H * HEAD_DIM
    KV = H // GQA_GROUP
    kvw = QW - D
    kvb = D // kvw
    nb = S // WINDOW
    scale = HEAD_DIM ** -0.5
    n_ci = len(comm.srcs) if comm else 0
    n_co = len(comm.out_structs) if comm else 0
    N_IN = 10
    GROUPS_TOGETHER = 2
    bd, _ = _head_sum_matrices()

    def body(*refs):
        q_ref, kc_ref, kp_ref, do_ref, l_ref, b_ref, qg_ref, kg_ref, s_ref, bd_ref = refs[:N_IN]
        dq_ref, dkv_ref, db_ref, ds_ref, dqg_ref, dkg_ref = refs[N_IN + n_ci:N_IN + 6 + n_ci]
        carry = refs[N_IN + 6 + n_ci + n_co]
        n = pl.program_id(0)
        if comm:
            comm.run(refs[N_IN:N_IN + n_ci], refs[N_IN + 6 + n_ci:N_IN + 6 + n_ci + n_co], refs[-2], refs[-1],
                     n == 0, n == nb)

        @pl.when(n == 0)
        def _():
            db_ref[...] = jnp.zeros(db_ref.shape, F32)
            ds_ref[...] = jnp.zeros(ds_ref.shape, F32)
            dqg_ref[...] = jnp.zeros(dqg_ref.shape, F32)
            dkg_ref[...] = jnp.zeros(dkg_ref.shape, F32)
            carry[...] = jnp.zeros(carry.shape, F32)

        @pl.when(n == nb)
        def _():
            dkv_ref[...] = carry[...].astype(BF16)

        @pl.when(n < nb)
        def _():
            valid = _band_valid(n)
            q = q_ref[...]
            kvc = kc_ref[...]
            kvp = kp_ref[...]
            do_all = do_ref[...]
            lse = l_ref[...]
            qgain = qg_ref[...]
            kgain = kg_ref[...]
            rq = lax.rsqrt(_head_sums(q * q, bd_ref) * (1.0 / HEAD_DIM) + EPS)
            qhat = q * rq
            qn_all = (qhat * qgain).astype(BF16)
            def run_groups(groups):
                idx = range(len(groups))
                heads = [slice(kh * GQA_GROUP, (kh + 1) * GQA_GROUP) for kh in groups]
                kbs = [jnp.concatenate([kvp[:, kh * HEAD_DIM:(kh + 1) * HEAD_DIM],
                                        kvc[:, kh * HEAD_DIM:(kh + 1) * HEAD_DIM]], axis=0) for kh in groups]
                vbs = [jnp.concatenate([kvp[:, (KV + kh) * HEAD_DIM:(KV + kh + 1) * HEAD_DIM],
                                        kvc[:, (KV + kh) * HEAD_DIM:(KV + kh + 1) * HEAD_DIM]], axis=0).astype(BF16)
                       for kh in groups]
                knorm = [_head_norm(kb, kgain) for kb in kbs]
                kns = [t[2].astype(BF16) for t in knorm]
                qns = [_stack_heads(qn_all, kh) for kh in groups]
                ss = [_band_scores(qns[i], kns[i], b_ref, groups[i], valid, scale) for i in idx]
                lse8 = [jnp.concatenate([lse[:, kh * GQA_GROUP + g:kh * GQA_GROUP + g + 1]
                                         for g in range(GQA_GROUP)], axis=0) for kh in groups]
                ps = [jnp.exp(s - l) for s, l in zip(ss, lse8)]
                do8 = [_stack_heads(do_all, kh) for kh in groups]
                dps = [lax.dot_general(d, vb, NT, preferred_element_type=F32) for d, vb in zip(do8, vbs)]
                deltas = [jnp.sum(p * dp, axis=-1, keepdims=True) for p, dp in zip(ps, dps)]
                dss = [p * (dp - dl) for p, dp, dl in zip(ps, dps, deltas)]
                for i in idx:
                    db_ref[heads[i]] += dss[i].reshape(GQA_GROUP, WINDOW, 2 * WINDOW)
                    psink = jnp.exp(_sink_rows(s_ref, groups[i]) - lse8[i])
                    ds_ref[heads[i], :] += -jnp.sum((psink * deltas[i]).reshape(GQA_GROUP, WINDOW, 1), axis=1)
                dsbs = [(ds * scale).astype(BF16) for ds in dss]
                dqn_p = [_unstack_heads(jnp.dot(dsb, kn, preferred_element_type=F32)) for dsb, kn in zip(dsbs, kns)]
                dkns = [lax.dot_general(dsb, qn, TN, preferred_element_type=F32) for dsb, qn in zip(dsbs, qns)]
                dv_p = [lax.dot_general(p.astype(BF16), d, TN, preferred_element_type=F32) for p, d in zip(ps, do8)]
                dkg_p = jnp.zeros((1, HEAD_DIM), F32)
                dk_p = []
                for i in idx:
                    rk, khat, _ = knorm[i]
                    dkg_p = dkg_p + jnp.sum(dkns[i] * khat, axis=0, keepdims=True)
                    dkh = dkns[i] * kgain
                    dk_p.append(rk * (dkh - khat * jnp.mean(dkh * khat, axis=-1, keepdims=True)))
                return dqn_p, dk_p, dv_p, dkg_p

            dqn_parts, dk_parts, dv_parts = [], [], []
            dkg = jnp.zeros((1, HEAD_DIM), F32)
            for g0 in range(0, KV, GROUPS_TOGETHER):
                dqn_p, dk_p, dv_p, dkg_p = run_groups(list(range(g0, min(g0 + GROUPS_TOGETHER, KV))))
                dqn_parts += dqn_p
                dk_parts += dk_p
                dv_parts += dv_p
                dkg = dkg + dkg_p
            dqn = jnp.concatenate(dqn_parts, axis=1)
            dqh = dqn * qgain
            dq = rq * (dqh - qhat * (_head_sums(dqh * qhat, bd_ref) * (1.0 / HEAD_DIM)))
            dq_ref[...] = dq.astype(BF16)
            dqg_ref[...] += jnp.sum(dqn * qhat, axis=0, keepdims=True)
            dkg_ref[...] += dkg
            dkv = jnp.concatenate(dk_parts + dv_parts, axis=1)
            dkv_ref[...] = (carry[...] + dkv[0:WINDOW, :]).astype(BF16)
            carry[...] = dkv[WINDOW:2 * WINDOW, :]

    cur = lambda n: jnp.minimum(n, nb - 1)
    const2 = lambda shape: pl.BlockSpec(shape, lambda n: (0, 0))
    in_specs = [pl.BlockSpec((WINDOW, D), lambda n: (cur(n), 0)),
                pl.BlockSpec((WINDOW, kvw), lambda n: (cur(n), kvb)),
                pl.BlockSpec((WINDOW, kvw), lambda n: (jnp.maximum(cur(n) - 1, 0), kvb)),
                pl.BlockSpec((WINDOW, D), lambda n: (cur(n), 0)),
                pl.BlockSpec((WINDOW, H), lambda n: (cur(n), 0)),
                pl.BlockSpec(bias.shape, lambda n: (0, 0, 0)),
                const2((1, D)), const2((1, HEAD_DIM)), const2((H, 1)), const2(bd.shape)]
    out_specs = [pl.BlockSpec((WINDOW, D), lambda n: (cur(n), 0)),
                 pl.BlockSpec((WINDOW, kvw), lambda n: (jnp.maximum(n - 1, 0), 0)),
                 pl.BlockSpec(bias.shape, lambda n: (0, 0, 0)),
                 const2((H, 1)), const2((1, D)), const2((1, HEAD_DIM))]
    out_shape = [jax.ShapeDtypeStruct((S, D), BF16), jax.ShapeDtypeStruct((S, kvw), BF16),
                 jax.ShapeDtypeStruct(bias.shape, F32), jax.ShapeDtypeStruct((H, 1), F32),
                 jax.ShapeDtypeStruct((1, D), F32), jax.ShapeDtypeStruct((1, HEAD_DIM), F32)]
    scratch = [pltpu.VMEM((WINDOW, kvw), F32)]
    args = (qkv, qkv, qkv, do, lse, bias, jnp.tile(qg, (1, H)), kg, sinks, bd)
    if comm:
        in_specs += comm.in_specs()
        out_specs += comm.out_specs()
        out_shape += comm.out_structs
        scratch += comm.scratch()
        args += tuple(comm.srcs)
    return pl.pallas_call(
        body, name=name, grid=(nb + 1,), in_specs=in_specs, out_specs=out_specs, out_shape=out_shape,
        scratch_shapes=scratch, compiler_params=_cparams("arbitrary"),
    )(*args)


def _window_sums(src, bufs, lo, n_rows, step_sign, col_groups):
    out = []
    for g, cols in enumerate(col_groups):
        prev = src
        for level in range(g + 1):
            k = step_sign * (1 << level)
            cur = bufs[level]
            cur[pl.ds(lo, n_rows), cols] = prev[pl.ds(lo, n_rows), cols] + prev[pl.ds(lo + k, n_rows), cols]
            prev = cur
        out.append(prev)
    return out


def _pool_fwd(name, x, gain, wp, scale):
    S, D = x.shape
    G, C = wp.shape[0], wp.shape[1]
    tm = _tile(S, 256, POOL_HALO)
    hb = tm // POOL_HALO
    HL = POOL_HALO
    groups = [slice(g * C, (g + 1) * C) for g in range(G)]

    def body(x_ref, xh_ref, g_ref, w_ref, sc_ref, o_ref, d_ref, ext, p2, p4, p8, p16):
        i = pl.program_id(0)
        gain_v = g_ref[...]
        xt = x_ref[...]
        h = _head_norm(xt, gain_v)[2]
        hh = _head_norm(xh_ref[...], gain_v)[2]
        ext[pl.ds(0, HL), :] = jnp.where(i > 0, hh, 0.0)
        ext[pl.ds(HL, tm), :] = h
        bufs = (p2, p4, p8, p16)
        for b in bufs:
            b[pl.ds(0, 8), :] = jnp.zeros((8, D), F32)
        sums = _window_sums(ext, bufs, 8, tm + HL - 8, -1, groups)
        t = i * tm + lax.broadcasted_iota(jnp.int32, (tm, 1), 0)
        for g, cols in enumerate(groups):
            cnt = jnp.minimum(t + 1, POOL_WINDOWS[g]).astype(F32)
            d = sums[g][pl.ds(HL, tm), cols] / cnt - h[:, cols]
            db = d.astype(BF16)
            d_ref[:, cols] = db
            y = jnp.dot(db, w_ref[g], preferred_element_type=F32)
            o_ref[:, cols] = xt[:, cols] + y * sc_ref[:, cols]

    row = pl.BlockSpec((tm, D), lambda i: (i, 0))
    vec = pl.BlockSpec((1, D), lambda i: (0, 0))
    return pl.pallas_call(
        body, name=name, grid=(S // tm,),
        in_specs=[row, pl.BlockSpec((HL, D), lambda i: (jnp.maximum(i * hb - 1, 0), 0)), vec,
                  pl.BlockSpec(wp.shape, lambda i: (0, 0, 0)), vec],
        out_specs=[row, row],
        out_shape=[jax.ShapeDtypeStruct((S, D), F32), jax.ShapeDtypeStruct((S, D), BF16)],
        scratch_shapes=[pltpu.VMEM((tm + HL, D), F32) for _ in range(5)],
        compiler_params=_cparams("parallel"),
    )(x, x, gain, wp, scale)


def _pool_bwd(name, dx1, x, gain, wp, scale, dsave):
    S, D = x.shape
    G, C = wp.shape[0], wp.shape[1]
    tm = _tile(S, 256, POOL_HALO)
    hb = tm // POOL_HALO
    HL = POOL_HALO
    nI = S // tm
    groups = [slice(g * C, (g + 1) * C) for g in range(G)]

    def body(dx_ref, dxh_ref, x_ref, g_ref, w_ref, sc_ref, ds_ref, o_ref, ob_ref, dw_ref, dsc_ref, dg_ref,
             ext, p2, p4, p8, p16):
        i = pl.program_id(0)

        @pl.when(i == 0)
        def _():
            dw_ref[...] = jnp.zeros(dw_ref.shape, F32)
            dsc_ref[...] = jnp.zeros(dsc_ref.shape, F32)
            dg_ref[...] = jnp.zeros(dg_ref.shape, F32)

        dx1t = dx_ref[...]
        sc = sc_ref[...]
        dys = (dx1t * sc).astype(BF16)
        dys_h = (dxh_ref[...] * sc).astype(BF16)
        t = i * tm + lax.broadcasted_iota(jnp.int32, (tm, 1), 0)
        th = (i + 1) * tm + lax.broadcasted_iota(jnp.int32, (HL, 1), 0)
        dds = []
        for g, cols in enumerate(groups):
            dsv = ds_ref[:, cols]
            y = jnp.dot(dsv, w_ref[g], preferred_element_type=F32)
            dsc_ref[:, cols] += jnp.sum(dx1t[:, cols] * y, axis=0, keepdims=True)
            dw_ref[g] += lax.dot_general(dsv, dys[:, cols], TN, preferred_element_type=F32)
            dd = lax.dot_general(dys[:, cols], w_ref[g], NT, preferred_element_type=F32)
            dd_h = lax.dot_general(dys_h[:, cols], w_ref[g], NT, preferred_element_type=F32)
            dds.append(dd)
            w = POOL_WINDOWS[g]
            ext[pl.ds(0, tm), cols] = dd / jnp.minimum(t + 1, w).astype(F32)
            e_h = dd_h / jnp.minimum(th + 1, w).astype(F32)
            ext[pl.ds(tm, HL), cols] = jnp.where(i < nI - 1, e_h, 0.0)
        bufs = (p2, p4, p8, p16)
        for b in bufs:
            b[pl.ds(tm + HL - 8, 8), :] = jnp.zeros((8, D), F32)
        sums = _window_sums(ext, bufs, 0, tm + HL - 8, 1, groups)
        dh = jnp.concatenate([sums[g][pl.ds(0, tm), cols] - dds[g] for g, cols in enumerate(groups)], axis=1)
        dx, dg = _rms_bwd_math(dh, x_ref[...], g_ref[...])
        dx = dx1t + dx
        o_ref[...] = dx
        ob_ref[...] = dx.astype(BF16)
        dg_ref[...] += dg

    row = pl.BlockSpec((tm, D), lambda i: (i, 0))
    vec = pl.BlockSpec((1, D), lambda i: (0, 0))
    last_h = S // HL - 1
    return pl.pallas_call(
        body, name=name, grid=(nI,),
        in_specs=[row, pl.BlockSpec((HL, D), lambda i: (jnp.minimum((i + 1) * hb, last_h), 0)), row, vec,
                  pl.BlockSpec(wp.shape, lambda i: (0, 0, 0)), vec, row],
        out_specs=[row, row, pl.BlockSpec(wp.shape, lambda i: (0, 0, 0)), vec, vec],
        out_shape=[jax.ShapeDtypeStruct((S, D), F32), jax.ShapeDtypeStruct((S, D), BF16),
                   jax.ShapeDtypeStruct(wp.shape, F32),
                   jax.ShapeDtypeStruct((1, D), F32), jax.ShapeDtypeStruct((1, D), F32)],
        scratch_shapes=[pltpu.VMEM((tm + HL, D), F32) for _ in range(5)],
        compiler_params=_cparams("arbitrary"),
    )(dx1, dx1, x, gain, wp, scale, dsave)


HBM_SPEC = pl.BlockSpec(memory_space=pltpu.HBM)


def _coords():
    return lax.axis_index("x"), lax.axis_index("y"), lax.axis_index("c")


class _AgItem:
    def __init__(self, src, out_struct, slot, half):
        self.src, self.out_struct, self.slot, self.half = src, out_struct, slot, half


def _rows_item(src, n_rows_total):
    r, ncol = src.shape
    return _AgItem(src, jax.ShapeDtypeStruct((n_rows_total, ncol), src.dtype),
                   lambda out, d: out.at[pl.ds(pl.multiple_of(d * r, 16), r), :],
                   lambda ref, h: ref.at[pl.ds(h * (r // 2), r // 2), :])


def _cols_item(src, n_cols_total):
    nrow, cb = src.shape
    return _AgItem(src, jax.ShapeDtypeStruct((nrow, n_cols_total), src.dtype),
                   lambda out, d: out.at[:, pl.ds(pl.multiple_of(d * cb, LANES), cb)],
                   lambda ref, h: ref.at[pl.ds(h * (nrow // 2), nrow // 2), :])


def _lead_item(src):
    return _AgItem(src, jax.ShapeDtypeStruct((NDEV,) + src.shape, src.dtype),
                   lambda out, d: out.at[d],
                   lambda ref, h: ref.at[pl.ds(h * (src.shape[0] // 2), src.shape[0] // 2)])


def _pool_item(src, c_total):
    g, pc, c = src.shape
    return _AgItem(src, jax.ShapeDtypeStruct((g, c_total, c), src.dtype),
                   lambda out, d: out.at[:, pl.ds(pl.multiple_of(d * pc, 16), pc), :],
                   lambda ref, h: ref.at[pl.ds(h * (g // 2), g // 2)])


class _AllGather:
    N_PHASES = 4

    def __init__(self, items):
        self.items = list(items)

    def srcs(self):
        return [it.src for it in self.items]

    def out_structs(self):
        return [it.out_struct for it in self.items]

    def in_specs(self):
        return [HBM_SPEC] * len(self.items)

    def out_specs(self):
        return [HBM_SPEC] * len(self.items)

    def scratch(self):
        n = len(self.items)
        return [pltpu.SemaphoreType.DMA((8 * n,)), pltpu.SemaphoreType.DMA((8 * n,)), pltpu.SemaphoreType.DMA((n,))]

    def phase(self, ph, ins, outs, send_sems, recv_sems, local_sems):
        x, y, c = _coords()
        me, xn = 4 * x + 2 * y + c, 4 * (1 - x) + 2 * y + c
        yn, dg = 4 * x + 2 * (1 - y) + c, 4 * (1 - x) + 2 * (1 - y) + c
        XN, YN, SB = (1 - x, y, c), (x, 1 - y, c), (x, y, 1 - c)
        sib = lambda blk: blk + 1 - 2 * c
        for o, it in enumerate(self.items):
            slot = lambda d, it=it, o=o: it.slot(outs[o], d)
            half = it.half
            table = [
                (ins[o], slot(me), XN, slot(xn)),
                (ins[o], slot(me), YN, slot(yn)),
                (half(slot(xn), 0), half(slot(xn), 0), YN, half(slot(dg), 0)),
                (half(slot(yn), 1), half(slot(yn), 1), XN, half(slot(dg), 1)),
                (ins[o], slot(me), SB, slot(sib(me))),
                (slot(xn), slot(xn), SB, slot(sib(xn))),
                (slot(yn), slot(yn), SB, slot(sib(yn))),
                (slot(dg), slot(dg), SB, slot(sib(dg))),
            ]

            def send(k, table=table, o=o):
                src, dst, peer, _ = table[k]
                return pltpu.make_async_remote_copy(src_ref=src, dst_ref=dst, send_sem=send_sems.at[8 * o + k],
                                                    recv_sem=recv_sems.at[8 * o + k], device_id=peer, device_id_type=MESH)

            def arrived(k, table=table, o=o):
                land = table[k][3]
                pltpu.make_async_remote_copy(src_ref=land, dst_ref=land, send_sem=send_sems.at[8 * o + k],
                                             recv_sem=recv_sems.at[8 * o + k], device_id=table[k][2],
                                             device_id_type=MESH).wait_recv()

            local = pltpu.make_async_copy(ins[o], slot(me), local_sems.at[o])
            if ph == 0:
                local.start()
                for k in (0, 1, 4):
                    send(k).start()
            elif ph == 1:
                arrived(0)
                send(2).start()
                send(5).start()
                arrived(1)
                send(3).start()
                send(6).start()
            elif ph == 2:
                arrived(2)
                arrived(3)
                send(7).start()
            else:
                for k in (4, 5, 6, 7):
                    arrived(k)
                for k in range(8):
                    send(k).wait_send()
                local.wait()

    def run_at(self, step, schedule, ins, outs, send_sems, recv_sems, local_sems):
        for ph in range(self.N_PHASES):
            @pl.when(step == schedule[ph])
            def _(ph=ph):
                self.phase(ph, ins, outs, send_sems, recv_sems, local_sems)


def _allgather_now(name, ag):
    n = len(ag.items)

    def body(*refs):
        for ph in range(ag.N_PHASES):
            ag.phase(ph, refs[:n], refs[n:2 * n], *refs[2 * n:])

    return pl.pallas_call(body, name=name, in_specs=ag.in_specs(), out_specs=ag.out_specs(),
                          out_shape=ag.out_structs(), scratch_shapes=ag.scratch())(*ag.srcs())


def _allgather_small(name, block):
    m_per, ncol = block.shape

    def body(x_ref, out_ref, send_sems, recv_sems, local_sem):
        x, y, c = _coords()
        me, sibling = (x, y, c), (x, y, 1 - c)
        chips = [(1 - x, y), (x, 1 - y), (1 - x, 1 - y)]

        def rows(px, py, pc):
            return out_ref.at[pl.ds((4 * px + 2 * py + pc) * m_per, m_per), :]

        def copy(k, block_of, to, src=None):
            return pltpu.make_async_remote_copy(
                src_ref=rows(*block_of) if src is None else src, dst_ref=rows(*block_of),
                send_sem=send_sems.at[k], recv_sem=recv_sems.at[k], device_id=to, device_id_type=MESH)

        mine = pltpu.make_async_copy(x_ref, rows(*me), local_sem)
        mine.start()
        first = [copy(0, me, sibling, src=x_ref)]
        first += [copy(1 + j, me, (*chip, c), src=x_ref) for j, chip in enumerate(chips)]
        for cp in first:
            cp.start()
        passed = [copy(4 + j, (*chip, c), sibling) for j, chip in enumerate(chips)]
        for j, chip in enumerate(chips):
            copy(1 + j, (*chip, c), me).wait_recv()
            passed[j].start()
        copy(0, sibling, me).wait_recv()
        for j, chip in enumerate(chips):
            copy(4 + j, (*chip, 1 - c), me).wait_recv()
        for cp in first + passed:
            cp.wait_send()
        mine.wait()

    return pl.pallas_call(
        body, name=name, out_shape=jax.ShapeDtypeStruct((NDEV * m_per, ncol), block.dtype),
        in_specs=[pl.BlockSpec(memory_space=pltpu.VMEM)], out_specs=pl.BlockSpec(memory_space=pltpu.VMEM),
        scratch_shapes=[pltpu.SemaphoreType.DMA((7,)), pltpu.SemaphoreType.DMA((7,)), pltpu.SemaphoreType.DMA],
        compiler_params=pltpu.CompilerParams(vmem_limit_bytes=VMEM_LIMIT),
    )(block)


def _exchange_now(name, ex):
    n_in, n_out = len(ex.srcs), len(ex.out_structs)

    def body(*refs):
        ex.run(refs[:n_in], refs[n_in:n_in + n_out], refs[n_in + n_out], refs[n_in + n_out + 1], True, True)

    return pl.pallas_call(body, name=name, in_specs=ex.in_specs(), out_specs=ex.out_specs(),
                          out_shape=ex.out_structs, scratch_shapes=ex.scratch())(*ex.srcs)


def _rs_stage_c(gs):
    def plan(x, y, c, ins, outs):
        sib = (x, y, 1 - c)
        return [(g.at[q, 1 - c], r.at[q], sib) for g, r in zip(ins, outs) for q in range(4)]

    outs = [jax.ShapeDtypeStruct((4,) + g.shape[2:], g.dtype) for g in gs]
    return _Exchange(gs, outs, plan, 4 * len(gs))


def _rs_stage_ici(sends, first):
    def plan(x, y, c, ins, outs):
        XN, YN = (1 - x, y, c), (x, 1 - y, c)
        peers = (YN, XN) if first else (XN, YN)
        return [(s.at[h], r.at[h], peers[h]) for s, r in zip(ins, outs) for h in range(2)]

    outs = [jax.ShapeDtypeStruct(s.shape, s.dtype) for s in sends]
    return _Exchange(sends, outs, plan, 2 * len(sends))


def _coord_vec():
    x, y, c = _coords()
    return jnp.stack([x, y, c]).astype(jnp.int32)


def _rs_add1(name, g, r1, coords):
    R, L = g.shape[3], g.shape[4]
    tr = _tile(R, 512, 16)

    def qk(h, idx, cr):
        return jnp.where(h == 0, 2 * idx + cr[1], 2 * cr[0] + idx)

    def qs(h, idx, cr):
        return jnp.where(h == 0, 2 * idx + 1 - cr[1], 2 * (1 - cr[0]) + idx)

    def body(cr, gk, rk, gsd, rsd, keep, send):
        keep[...] = gk[...] + rk[...]
        send[...] = (gsd[...] + rsd[...]).astype(BF16)

    gspec = lambda qf: pl.BlockSpec((None, None, None, tr, L), lambda h, idx, r, cr: (qf(h, idx, cr), cr[2], h, r, 0))
    rspec = lambda qf: pl.BlockSpec((None, None, tr, L), lambda h, idx, r, cr: (qf(h, idx, cr), h, r, 0))
    ospec = pl.BlockSpec((None, None, tr, L), lambda h, idx, r, cr: (h, idx, r, 0))
    return pl.pallas_call(
        body, name=name,
        grid_spec=pltpu.PrefetchScalarGridSpec(
            num_scalar_prefetch=1, grid=(2, 2, R // tr),
            in_specs=[gspec(qk), rspec(qk), gspec(qs), rspec(qs)], out_specs=[ospec, ospec]),
        out_shape=[jax.ShapeDtypeStruct((2, 2, R, L), F32), jax.ShapeDtypeStruct((2, 2, R, L), BF16)],
        compiler_params=_cparams("parallel", "parallel", "parallel"),
    )(coords, g, r1, g, r1)


def _rs_add2(name, keep2, recv2, coords):
    R, L = keep2.shape[2], keep2.shape[3]
    tr = _tile(R, 512, 16)

    def mine(h, cr):
        return jnp.where(h == 0, cr[0], cr[1])

    def body(cr, kk, rk, ks, rs, keep, send):
        keep[...] = kk[...] + rk[...].astype(F32)
        send[...] = (ks[...] + rs[...].astype(F32)).astype(BF16)

    sel = lambda f: pl.BlockSpec((None, None, tr, L), lambda h, r, cr: (h, f(h, cr), r, 0))
    ospec = pl.BlockSpec((None, tr, L), lambda h, r, cr: (h, r, 0))
    other = lambda h, cr: 1 - mine(h, cr)
    return pl.pallas_call(
        body, name=name,
        grid_spec=pltpu.PrefetchScalarGridSpec(
            num_scalar_prefetch=1, grid=(2, R // tr),
            in_specs=[sel(mine), sel(mine), sel(other), sel(other)], out_specs=[ospec, ospec]),
        out_shape=[jax.ShapeDtypeStruct((2, R, L), F32), jax.ShapeDtypeStruct((2, R, L), BF16)],
        compiler_params=_cparams("parallel", "parallel"),
    )(coords, keep2, recv2, keep2, recv2)


def _rs_add3(name, keep3, recv3):
    R, L = keep3.shape[1], keep3.shape[2]
    tr = _tile(R, 512, 16)

    def body(k, r, o):
        o[...] = k[...] + r[...].astype(F32)

    spec = pl.BlockSpec((None, tr, L), lambda h, r: (h, r, 0))
    return pl.pallas_call(body, name=name, grid=(2, R // tr), in_specs=[spec, spec], out_specs=spec,
                          out_shape=jax.ShapeDtypeStruct((2, R, L), F32),
                          compiler_params=_cparams("parallel", "parallel"))(keep3, recv3)


class _ReduceScatter:
    N_STAGES = 3

    def __init__(self, tag, gs, coords):
        self.tag, self.coords, self.stage, self.result = tag, coords, 0, None
        self.full = []
        for g in gs:
            per = int(np.prod(g.shape[1:]))
            L = g.shape[-1]
            self.full.append(g.reshape(4, 2, 2, per // (2 * L), L))
        self.keep, self.send = None, None

    def exchange(self):
        if self.stage == 0:
            return _rs_stage_c(self.full)
        return _rs_stage_ici(list(self.send), self.stage == 1)

    def absorb(self, recv):
        names = [f"rs_add{self.stage + 1}_{self.tag}_{k}" for k in range(len(self.full))]
        if self.stage == 0:
            pairs = [_rs_add1(nm, g, r, self.coords) for nm, g, r in zip(names, self.full, recv)]
            self.keep, self.send = zip(*pairs)
        elif self.stage == 1:
            pairs = [_rs_add2(nm, kp, r, self.coords) for nm, kp, r in zip(names, self.keep, recv)]
            self.keep, self.send = zip(*pairs)
        else:
            self.result = [_rs_add3(nm, kp, r) for nm, kp, r in zip(names, self.keep, recv)]
        self.stage += 1

    def finish_now(self):
        while self.stage < self.N_STAGES:
            self.absorb(_exchange_now(f"rs_x{self.stage}_{self.tag}", self.exchange()))
        return self.result


def _adamw(name, w, g, m, v):
    R, L = w.shape
    tr = _tile(R, 256, 8)

    def body(w_ref, g_ref, m_ref, v_ref, d_ref, nm_ref, nv_ref):
        gv = g_ref[...]
        nm = ADAM_B1 * m_ref[...] + (1.0 - ADAM_B1) * gv
        nv = ADAM_B2 * v_ref[...] + (1.0 - ADAM_B2) * (gv * gv)
        m_hat = nm / (1.0 - ADAM_B1 ** ADAM_STEP)
        v_hat = nv / (1.0 - ADAM_B2 ** ADAM_STEP)
        d_ref[...] = -ADAM_LR * (m_hat / (jnp.sqrt(v_hat) + ADAM_EPS) + ADAM_WD * w_ref[...])
        nm_ref[...] = nm
        nv_ref[...] = nv

    spec = pl.BlockSpec((tr, L), lambda i: (i, 0))
    out = jax.ShapeDtypeStruct((R, L), F32)
    return pl.pallas_call(body, name=name, grid=(R // tr,), in_specs=[spec] * 4, out_specs=[spec] * 3,
                          out_shape=[out, out, out], compiler_params=_cparams("parallel"))(w, g, m, v)


def _sum_devices(name, gathered):
    _, R, L = gathered.shape
    tr = _tile(R, 512, 8)

    def body(g_ref, o_ref):
        acc = g_ref[0]
        for d in range(1, NDEV):
            acc = acc + g_ref[d]
        o_ref[...] = acc

    return pl.pallas_call(body, name=name, grid=(R // tr,),
                          in_specs=[pl.BlockSpec((NDEV, tr, L), lambda i: (0, i, 0))],
                          out_specs=pl.BlockSpec((tr, L), lambda i: (i, 0)),
                          out_shape=jax.ShapeDtypeStruct((R, L), F32),
                          compiler_params=_cparams("parallel"))(gathered)


def _pack(parts):
    flat, offs, pos = [], [], 0
    for p in parts:
        n = int(np.prod(p.shape))
        padded = -(-n // PACK_ALIGN) * PACK_ALIGN
        flat.append(jnp.pad(p.reshape(-1).astype(F32), (0, padded - n)))
        offs.append((pos, n, p.shape))
        pos += padded
    return jnp.concatenate(flat).reshape(-1, LANES), offs


def _unpack(packed, offs):
    flat = packed.reshape(-1)
    return [flat[pos:pos + n].reshape(shape) for pos, n, shape in offs]


def kernel(x, norm_mix, norm_ffn, rel_bias, attn_w_qkv, attn_q_gain, attn_k_gain, attn_sinks, attn_w_o, pool_w, pool_scale, ffn_w_up, ffn_conv_w, ffn_conv_b, ffn_w_down, loss_target, m_norm_mix, m_norm_ffn, m_rel_bias, m_attn_w_qkv, m_attn_q_gain, m_attn_k_gain, m_attn_sinks, m_attn_w_o, m_pool_w, m_pool_scale, m_ffn_w_up, m_ffn_conv_w, m_ffn_conv_b, m_ffn_w_down, v_norm_mix, v_norm_ffn, v_rel_bias, v_attn_w_qkv, v_attn_q_gain, v_attn_k_gain, v_attn_sinks, v_attn_w_o, v_pool_w, v_pool_scale, v_ffn_w_up, v_ffn_conv_w, v_ffn_conv_b, v_ffn_w_down):
    xs = x[0]
    target = loss_target[0]
    S, D = xs.shape
    depth = norm_mix.shape[0]
    H = D // HEAD_DIM
    n_attn, n_pool = attn_w_qkv.shape[0], pool_w.shape[0]
    QS = attn_w_qkv.shape[2]
    CB = ffn_w_up.shape[2]
    FB = ffn_w_down.shape[1]
    FF = FB * NDEV
    G, PC, C = pool_w.shape[1], pool_w.shape[2], pool_w.shape[3]
    xi, yi, ci = _coords()
    me = 4 * xi + 2 * yi + ci
    coords = _coord_vec()

    def ffn_allgather(i):
        return _AllGather([_cols_item(ffn_w_up[i].astype(BF16), NDEV * CB),
                           _rows_item(ffn_w_down[i].astype(BF16), FF)])

    def mixer_allgather(i):
        j = i // 2
        if i % 2 == 0:
            return _AllGather([_lead_item(attn_w_qkv[j].astype(BF16)), _rows_item(attn_w_o[j].astype(BF16), D)])
        return _AllGather([_pool_item(pool_w[j].astype(BF16), C)])

    def layer_allgather(i):
        return _AllGather(ffn_allgather(i).items + mixer_allgather(i).items)

    def unpack_ffn(outs):
        return {"up": outs[0], "down": outs[1]}

    def unpack_mixer(i, outs):
        if i % 2 == 0:
            return {"qkv": outs[0].transpose(1, 0, 2).reshape(D, NDEV * QS), "o": outs[1]}
        return {"pool": outs[0]}

    weights_of = [None] * depth
    weights_of[0] = unpack_mixer(0, _allgather_now("allgather_mixer_0", mixer_allgather(0)))

    small_in, small_in_offs = _pack([ffn_conv_w, pool_scale])
    gathered_in = _allgather_small("allgather_small_params", small_in).reshape(NDEV, -1)
    per_dev = [_unpack(gathered_in[d], small_in_offs) for d in range(NDEV)]
    conv_w_full = jnp.concatenate([p[0] for p in per_dev], axis=2)
    pool_scale_full = jnp.concatenate([p[1] for p in per_dev], axis=1)

    onehot = jnp.asarray(_t5_onehot())
    bias = _bias_band(rel_bias, onehot).reshape(H, WINDOW, 2 * WINDOW)

    saved = []
    cur = xs
    for i in range(depth):
        j = i // 2
        w = weights_of[i]
        st = {"x0": cur}
        if i % 2 == 0:
            h = _rmsnorm_fwd(f"norm_mix_{i}", cur, norm_mix[i:i + 1])
            qkv = _mm_rows(f"qkv_{i}", h, w["qkv"], NN, F32)
            qg, kg, sk = attn_q_gain[j:j + 1], attn_k_gain[j:j + 1], attn_sinks[j].reshape(H, 1)
            ag0 = ffn_allgather(0) if i == 0 else None
            o, lse, *gathered_ffn = _attn_fwd(f"attn_fwd_{i}", qkv, bias, qg, kg, sk, ag0)
            if ag0:
                w.update(unpack_ffn(gathered_ffn))
            x1 = _mm_rows(f"attn_out_{i}", o, w["o"], NN, F32, res=cur)
            st.update(h=h, qkv=qkv, o=o, lse=lse)
        else:
            x1, dsave = _pool_fwd(f"pool_fwd_{i}", cur, norm_mix[i:i + 1], w["pool"], pool_scale_full[j:j + 1])
            st.update(dsave=dsave)
        h2 = _rmsnorm_fwd(f"norm_ffn_{i}", x1, norm_ffn[i:i + 1])
        ag = layer_allgather(i + 1) if i + 1 < depth else None
        u3, ab3, a, *gathered_next = _ffn_up_fwd(f"ffn_up_{i}", h2, w["up"], conv_w_full[i], ffn_conv_b[i:i + 1], ag)
        if ag:
            weights_of[i + 1] = {**unpack_ffn(gathered_next[:2]), **unpack_mixer(i + 1, gathered_next[2:])}
        cur = _ffn_down_fwd(f"ffn_down_{i}", a, w["down"], x1)
        st.update(x1=x1, h2=h2, u3=u3, ab3=ab3, a=a)
        saved.append(st)

    loss_tile, dcur, dcur_b = _loss_fwd_bwd(cur, target)

    g_up_l, g_down_l = [None] * depth, [None] * depth
    g_qkv_l, g_o_l, g_pool_l = [None] * n_attn, [None] * n_attn, [None] * n_pool
    d_norm_mix, d_norm_ffn = [None] * depth, [None] * depth
    d_conv_w, d_conv_b = [None] * depth, [None] * depth
    d_qg, d_kg, d_sinks, d_pscale = [None] * n_attn, [None] * n_attn, [None] * n_attn, [None] * n_pool
    d_band = None

    def store_ffn(i, red):
        g_up_l[i] = red[0].reshape(D, CB)
        g_down_l[i] = red[1].reshape(FB, D)

    def store_mixer(i, red):
        if i % 2 == 0:
            g_qkv_l[i // 2] = red[0].reshape(D, QS)
            g_o_l[i // 2] = red[1].reshape(D // NDEV, D)
        else:
            g_pool_l[i // 2] = red[0].reshape(G, PC, C)

    def carrying(rs, call):
        if rs is None or rs.stage >= rs.N_STAGES:
            return call(None)
        out, *recv = call(rs.exchange())
        rs.absorb(recv)
        return out

    pending = None
    for i in reversed(range(depth)):
        j = i // 2
        st = saved[i]
        w = weights_of[i]
        du3, dc = _ffn_da_bwd(f"ffn_da_{i}", dcur_b, w["down"], st["u3"], st["ab3"], conv_w_full[i])
        d_conv_w[i] = jnp.concatenate([dc[0, 0:3], dc[1, 0:3]], axis=1)
        d_conv_b[i] = jnp.concatenate([dc[0, 3], dc[1, 3]], axis=0)
        dwdown = carrying(pending, lambda ex: _mm_tn(f"ffn_dwdown_{i}", st["a"], dcur_b,
                                                     tm_pref=CB if CB % LANES == 0 else 1024, comm=ex))
        dwdown = dwdown.reshape(NDEV, FB, D)
        dwup = carrying(pending, lambda ex: _ffn_dwup_bwd(f"ffn_dwup_{i}", st["h2"], du3, comm=ex))
        dh2 = carrying(pending, lambda ex: _ffn_dh_bwd(f"ffn_dh_{i}", du3, w["up"], comm=ex))
        if pending is not None:
            red = pending.finish_now()
            store_ffn(i + 1, red[:2])
            store_mixer(i + 1, red[2:])
        dx1, dx1_b, dg = _rmsnorm_bwd(f"norm_ffn_bwd_{i}", dh2, st["x1"], norm_ffn[i:i + 1], dcur)
        d_norm_ffn[i] = dg[0]
        own = _ReduceScatter(f"l{i}f", [dwup, dwdown], coords) if i == 0 else None
        if i % 2 == 0:
            do = _mm_rows(f"attn_do_{i}", dx1_b, w["o"], NT, BF16)
            dwo = carrying(own, lambda ex: _mm_tn(f"attn_dwo_{i}", st["o"], dx1_b, comm=ex)).reshape(NDEV, D // NDEV, D)
            qg, kg, sk = attn_q_gain[j:j + 1], attn_k_gain[j:j + 1], attn_sinks[j].reshape(H, 1)
            ex = own.exchange() if own is not None and own.stage < own.N_STAGES else None
            dq, dkv, db, dsk, dqg, dkg, *recv = _attn_bwd(f"attn_bwd_{i}", st["qkv"], do, st["lse"], bias, qg, kg, sk,
                                                          comm=ex)
            if ex is not None:
                own.absorb(recv)
            d_band = db if d_band is None else d_band + db
            d_sinks[j], d_qg[j], d_kg[j] = dsk[:, 0], dqg.reshape(H, HEAD_DIM).sum(axis=0), dkg[0]
            dqkv = jnp.concatenate([dq, dkv], axis=1)
            dwqkv = carrying(own, lambda ex: _mm_tn(f"attn_dwqkv_{i}", st["h"], dqkv, tn_pref=1280, comm=ex))
            dwqkv = dwqkv.reshape(D, NDEV, QS).transpose(1, 0, 2)
            dh = _mm_rows(f"attn_dh_{i}", dqkv, w["qkv"], NT, F32)
            dcur, dcur_b, dg = _rmsnorm_bwd(f"norm_mix_bwd_{i}", dh, st["x0"], norm_mix[i:i + 1], dx1)
            mixer_grads = [dwqkv, dwo]
        else:
            dcur, dcur_b, dwp, dps, dg = _pool_bwd(f"pool_bwd_{i}", dx1, st["x0"], norm_mix[i:i + 1], w["pool"],
                                                   pool_scale_full[j:j + 1], st["dsave"])
            d_pscale[j] = dps[0]
            mixer_grads = [dwp.reshape(G, NDEV, PC, C).transpose(1, 0, 2, 3)]
        d_norm_mix[i] = dg[0]
        if own is not None:
            store_ffn(i, own.finish_now())
            store_mixer(i, _ReduceScatter(f"l{i}m", mixer_grads, coords).finish_now())
        else:
            pending = _ReduceScatter(f"l{i}", [dwup, dwdown] + mixer_grads, coords)

    d_rel = _bias_band_bwd(d_band.reshape(H, -1), onehot)

    small_parts = [loss_tile, jnp.stack(d_norm_mix), jnp.stack(d_norm_ffn), d_rel, jnp.stack(d_qg),
                   jnp.stack(d_kg), jnp.stack(d_sinks), jnp.stack(d_pscale), jnp.stack(d_conv_w),
                   jnp.stack(d_conv_b)]
    small, small_offs = _pack(small_parts)
    gathered = _allgather_small("allgather_small_grads", small).reshape(NDEV, -1, LANES)
    summed = _unpack(_sum_devices("sum_small_grads", gathered), small_offs)
    loss = summed[0][0, 0]
    (g_norm_mix, g_norm_ffn, g_rel, g_qg, g_kg, g_sinks, g_pscale_full, g_conv_w_full, g_conv_b) = summed[1:]
    g_pscale = lax.dynamic_slice_in_dim(g_pscale_full, me * (D // NDEV), D // NDEV, axis=1)
    g_conv_w = lax.dynamic_slice_in_dim(g_conv_w_full, me * CB, CB, axis=2)

    grads = {
        "norm_mix": g_norm_mix, "norm_ffn": g_norm_ffn, "rel_bias": g_rel, "attn_w_qkv": jnp.stack(g_qkv_l),
        "attn_q_gain": g_qg, "attn_k_gain": g_kg, "attn_sinks": g_sinks, "attn_w_o": jnp.stack(g_o_l),
        "pool_w": jnp.stack(g_pool_l), "pool_scale": g_pscale, "ffn_w_up": jnp.stack(g_up_l),
        "ffn_conv_w": g_conv_w, "ffn_conv_b": g_conv_b, "ffn_w_down": jnp.stack(g_down_l),
    }
    weights = {
        "norm_mix": (norm_mix, m_norm_mix, v_norm_mix), "norm_ffn": (norm_ffn, m_norm_ffn, v_norm_ffn),
        "rel_bias": (rel_bias, m_rel_bias, v_rel_bias), "attn_w_qkv": (attn_w_qkv, m_attn_w_qkv, v_attn_w_qkv),
        "attn_q_gain": (attn_q_gain, m_attn_q_gain, v_attn_q_gain),
        "attn_k_gain": (attn_k_gain, m_attn_k_gain, v_attn_k_gain),
        "attn_sinks": (attn_sinks, m_attn_sinks, v_attn_sinks), "attn_w_o": (attn_w_o, m_attn_w_o, v_attn_w_o),
        "pool_w": (pool_w, m_pool_w, v_pool_w), "pool_scale": (pool_scale, m_pool_scale, v_pool_scale),
        "ffn_w_up": (ffn_w_up, m_ffn_w_up, v_ffn_w_up), "ffn_conv_w": (ffn_conv_w, m_ffn_conv_w, v_ffn_conv_w),
        "ffn_conv_b": (ffn_conv_b, m_ffn_conv_b, v_ffn_conv_b), "ffn_w_down": (ffn_w_down, m_ffn_w_down, v_ffn_w_down),
    }
    names = list(weights)
    big = ("attn_w_qkv", "attn_w_o", "pool_w", "ffn_w_up", "ffn_w_down")
    upd = {}
    for nm in big:
        w, m, v = weights[nm]
        two_d = lambda t: t.reshape(-1, w.shape[-1])
        d_, m_, v_ = _adamw(f"adamw_{nm}", two_d(w), two_d(grads[nm]), two_d(m), two_d(v))
        upd[nm] = (d_.reshape(w.shape), m_.reshape(w.shape), v_.reshape(w.shape))
    small_names = [nm for nm in names if nm not in big]
    pw, offs = _pack([weights[nm][0] for nm in small_names])
    pg, _ = _pack([grads[nm] for nm in small_names])
    pm, _ = _pack([weights[nm][1] for nm in small_names])
    pv, _ = _pack([weights[nm][2] for nm in small_names])
    d_, m_, v_ = _adamw("adamw_small", pw, pg, pm, pv)
    for nm, dd, mm, vv in zip(small_names, _unpack(d_, offs), _unpack(m_, offs), _unpack(v_, offs)):
        upd[nm] = (dd, mm, vv)

    grad_x = dcur[None]
    return (loss, grad_x, *[grads[nm].reshape(weights[nm][0].shape) for nm in names],
            *[upd[nm][0] for nm in names], *[upd[nm][1] for nm in names], *[upd[nm][2] for nm in names])
```

```python
import numpy as np
import jax
import jax.numpy as jnp
from jax import lax
from jax.experimental import pallas as pl
from jax.experimental.pallas import tpu as pltpu

F32 = jnp.float32
BF16 = jnp.bfloat16
MESH = pl.DeviceIdType.MESH

NDEV = 8
HEAD_DIM = 64
GQA_GROUP = 8
WINDOW = 128
N_BUCKETS = 32
MAX_DISTANCE = 128
POOL_WINDOWS = (2, 4, 8, 16)
POOL_HALO = 32
EPS = 1e-6
NEG_INF = -1e30
ADAM_LR = 0.001
ADAM_B1 = 0.9
ADAM_B2 = 0.999
ADAM_EPS = 1e-08
ADAM_WD = 0.01
ADAM_STEP = 10

V7X_VMEM_BYTES = 64 * 1024 * 1024
VMEM_LIMIT = V7X_VMEM_BYTES - 8 * 1024 * 1024
LANES = 128
SUBLANES = 8
PACK_ALIGN = SUBLANES * LANES

NN = (((1,), (0,)), ((), ()))
NT = (((1,), (1,)), ((), ()))
TN = (((0,), (0,)), ((), ()))


def _tile(dim, pref, align):
    t = min(pref, dim)
    t -= t % align
    while t >= align:
        if dim % t == 0:
            return t
        t -= align
    return dim


def _cparams(*sem):
    return pltpu.CompilerParams(dimension_semantics=sem, vmem_limit_bytes=VMEM_LIMIT)


def _bf(v):
    return v if v.dtype == BF16 else v.astype(BF16)


class _Exchange:
    def __init__(self, srcs, out_structs, plan, n_copies):
        self.srcs, self.out_structs, self.plan, self.n_copies = list(srcs), list(out_structs), plan, n_copies

    def in_specs(self):
        return [HBM_SPEC] * len(self.srcs)

    def out_specs(self):
        return [HBM_SPEC] * len(self.out_structs)

    def scratch(self):
        return [pltpu.SemaphoreType.DMA((self.n_copies,)), pltpu.SemaphoreType.DMA((self.n_copies,))]

    def run(self, in_refs, out_refs, send_sems, recv_sems, is_first, is_last):
        def copies():
            x, y, c = _coords()
            return [pltpu.make_async_remote_copy(src_ref=src, dst_ref=dst, send_sem=send_sems.at[k],
                                                 recv_sem=recv_sems.at[k], device_id=peer, device_id_type=MESH)
                    for k, (src, dst, peer) in enumerate(self.plan(x, y, c, in_refs, out_refs))]

        def start():
            for cp in copies():
                cp.start()

        def finish():
            cps = copies()
            for cp in cps:
                cp.wait_recv()
            for cp in cps:
                cp.wait_send()

        if is_first is True and is_last is True:
            start()
            finish()
        else:
            pl.when(is_first)(start)
            pl.when(is_last)(finish)


def _grid_edges(grid):
    first, last = True, True
    for ax, n in enumerate(grid):
        first = jnp.logical_and(first, pl.program_id(ax) == 0)
        last = jnp.logical_and(last, pl.program_id(ax) == n - 1)
    return first, last


def _mm(name, a, b, *, grid, a_spec, b_spec, o_spec, out_shape, contract, acc_shape, res=None, comm=None):
    nk = grid[2]
    n_main = 3 if res is not None else 2
    n_ci = len(comm.srcs) if comm else 0
    n_co = len(comm.out_structs) if comm else 0

    def body(*refs):
        a_ref, b_ref = refs[:2]
        r_ref = refs[2] if res is not None else None
        o_ref = refs[n_main + n_ci]
        scr = refs[n_main + n_ci + 1 + n_co:]
        if comm:
            first, last = _grid_edges(grid)
            comm.run(refs[n_main:n_main + n_ci], refs[n_main + n_ci + 1:n_main + n_ci + 1 + n_co],
                     scr[-2], scr[-1], first, last)
        part = lax.dot_general(_bf(a_ref[...]), _bf(b_ref[...]), contract, preferred_element_type=F32)

        def finish(acc):
            if r_ref is not None:
                acc = acc + r_ref[...]
            o_ref[...] = acc.astype(o_ref.dtype)

        if nk == 1:
            finish(part)
        else:
            acc_ref = scr[0]
            k = pl.program_id(2)

            @pl.when(k == 0)
            def _():
                acc_ref[...] = part

            @pl.when(k > 0)
            def _():
                acc_ref[...] += part

            @pl.when(k == nk - 1)
            def _():
                finish(acc_ref[...])

    in_specs = [a_spec, b_spec] + ([o_spec] if res is not None else [])
    args = (a, b) + ((res,) if res is not None else ())
    out_specs, out_shapes = o_spec, out_shape
    scratch = [pltpu.VMEM(acc_shape, F32)] if nk > 1 else []
    if comm:
        in_specs += comm.in_specs()
        args += tuple(comm.srcs)
        out_specs = [o_spec] + comm.out_specs()
        out_shapes = [out_shape] + comm.out_structs
        scratch += comm.scratch()
    return pl.pallas_call(
        body, name=name, grid=grid, in_specs=in_specs, out_specs=out_specs, out_shape=out_shapes,
        scratch_shapes=scratch,
        compiler_params=_cparams(*(("arbitrary",) * 3 if comm else ("parallel", "parallel", "arbitrary"))),
    )(*args)


def _mm_rows(name, a, b, contract, out_dtype, res=None, tm_pref=512, comm=None):
    S, K = a.shape
    N = b.shape[1] if contract == NN else b.shape[0]
    tm = _tile(S, tm_pref, 16)
    return _mm(name, a, b, grid=(1, S // tm, 1),
               a_spec=pl.BlockSpec((tm, K), lambda p, q, k: (q, 0)),
               b_spec=pl.BlockSpec(b.shape, lambda p, q, k: (0, 0)),
               o_spec=pl.BlockSpec((tm, N), lambda p, q, k: (q, 0)),
               out_shape=jax.ShapeDtypeStruct((S, N), out_dtype), contract=contract,
               acc_shape=(tm, N), res=res, comm=comm)


def _mm_tn(name, a, b, tm_pref=1024, tn_pref=1024, tk_pref=2048, comm=None):
    S, M = a.shape
    N = b.shape[1]
    tm, tn, tk = _tile(M, tm_pref, LANES), _tile(N, tn_pref, LANES), _tile(S, tk_pref, 16)
    return _mm(name, a, b, grid=(M // tm, N // tn, S // tk),
               a_spec=pl.BlockSpec((tk, tm), lambda p, q, k: (k, p)),
               b_spec=pl.BlockSpec((tk, tn), lambda p, q, k: (k, q)),
               o_spec=pl.BlockSpec((tm, tn), lambda p, q, k: (p, q)),
               out_shape=jax.ShapeDtypeStruct((M, N), F32), contract=TN, acc_shape=(tm, tn), comm=comm)


def _rmsnorm_fwd(name, x, gain):
    S, D = x.shape
    tm = _tile(S, 512, 16)

    def body(x_ref, g_ref, o_ref):
        xf = x_ref[...]
        r = lax.rsqrt(jnp.mean(xf * xf, axis=-1, keepdims=True) + EPS)
        o_ref[...] = (xf * r * g_ref[...]).astype(o_ref.dtype)

    return pl.pallas_call(
        body, name=name, grid=(S // tm,),
        in_specs=[pl.BlockSpec((tm, D), lambda i: (i, 0)), pl.BlockSpec((1, D), lambda i: (0, 0))],
        out_specs=pl.BlockSpec((tm, D), lambda i: (i, 0)),
        out_shape=jax.ShapeDtypeStruct((S, D), BF16), compiler_params=_cparams("parallel"),
    )(x, gain)


def _rms_bwd_math(dh, xf, gain):
    r = lax.rsqrt(jnp.mean(xf * xf, axis=-1, keepdims=True) + EPS)
    xhat = xf * r
    dxh = dh * gain
    dx = r * (dxh - xhat * jnp.mean(dxh * xhat, axis=-1, keepdims=True))
    return dx, jnp.sum(dh * xhat, axis=0, keepdims=True)


def _rmsnorm_bwd(name, dh, x, gain, dres):
    S, D = x.shape
    tm = _tile(S, 256, 16)

    def body(dh_ref, x_ref, g_ref, dr_ref, dx_ref, dxb_ref, dg_ref):
        dx, dg = _rms_bwd_math(dh_ref[...], x_ref[...], g_ref[...])
        dx = dr_ref[...] + dx
        dx_ref[...] = dx
        dxb_ref[...] = dx.astype(BF16)

        @pl.when(pl.program_id(0) == 0)
        def _():
            dg_ref[...] = dg

        @pl.when(pl.program_id(0) > 0)
        def _():
            dg_ref[...] += dg

    row = pl.BlockSpec((tm, D), lambda i: (i, 0))
    vec = pl.BlockSpec((1, D), lambda i: (0, 0))
    return pl.pallas_call(
        body, name=name, grid=(S // tm,), in_specs=[row, row, vec, row], out_specs=[row, row, vec],
        out_shape=[jax.ShapeDtypeStruct((S, D), F32), jax.ShapeDtypeStruct((S, D), BF16),
                   jax.ShapeDtypeStruct((1, D), F32)],
        compiler_params=_cparams("arbitrary"),
    )(dh, x, gain, dres)


def _loss_fwd_bwd(y, target):
    S, D = y.shape
    tm = _tile(S, 512, 16)

    def body(y_ref, t_ref, l_ref, dy_ref, dyb_ref):
        e = y_ref[...] - t_ref[...]
        dy = e * (1.0 / D)
        dy_ref[...] = dy
        dyb_ref[...] = dy.astype(BF16)
        part = 0.5 * jnp.sum(jnp.mean(e * e, axis=-1, keepdims=True), axis=0, keepdims=True)
        part = jnp.broadcast_to(part, (SUBLANES, LANES))

        @pl.when(pl.program_id(0) == 0)
        def _():
            l_ref[...] = part

        @pl.when(pl.program_id(0) > 0)
        def _():
            l_ref[...] += part

    row = pl.BlockSpec((tm, D), lambda i: (i, 0))
    return pl.pallas_call(
        body, name="loss", grid=(S // tm,), in_specs=[row, row],
        out_specs=[pl.BlockSpec((SUBLANES, LANES), lambda i: (0, 0)), row, row],
        out_shape=[jax.ShapeDtypeStruct((SUBLANES, LANES), F32), jax.ShapeDtypeStruct((S, D), F32),
                   jax.ShapeDtypeStruct((S, D), BF16)],
        compiler_params=_cparams("arbitrary"),
    )(y, target)


def _sigmoid(v):
    return 1.0 / (1.0 + jnp.exp(-v))


MXU_COLS = 256


def _col_chunks(n):
    return [slice(c, min(c + MXU_COLS, n)) for c in range(0, n, MXU_COLS)]


def _shift_rows(v, k, edge8, down):
    tm = v.shape[0]
    sub = lax.broadcasted_iota(jnp.int32, edge8.shape, 0)
    if down:
        r = pltpu.roll(v, k, axis=0)
        head = jnp.where(sub < k, pltpu.roll(edge8, k, axis=0), r[0:8, :])
        return jnp.concatenate([head, r[8:, :]], axis=0)
    r = pltpu.roll(v, tm - k, axis=0)
    tail = jnp.where(sub >= 8 - k, pltpu.roll(edge8, 8 - k, axis=0), r[tm - 8:tm, :])
    return jnp.concatenate([r[:tm - 8, :], tail], axis=0)


def _ffn_up_fwd(name, h2, wup, cw, cb, ag=None):
    S, D = h2.shape
    FF = wup.shape[1] // 2
    CB = _tile(FF, 1408, LANES)
    NJ = FF // CB
    tm = _tile(S, 512, 16)
    nI = S // tm
    n_steps = NJ * nI
    n_ag = len(ag.items) if ag else 0
    schedule = (0, (9 * n_steps) // 20, (7 * n_steps) // 10, n_steps - 1)

    def body(*refs):
        h_ref, wg_ref, wv_ref, cwg_ref, cwv_ref, cbg_ref, cbv_ref = refs[:7]
        u_ref, ab_ref, a_ref = refs[7 + n_ag:10 + n_ag]
        edge_g, edge_v = refs[10 + 2 * n_ag:12 + 2 * n_ag]
        if ag:
            ag.run_at(pl.program_id(0) * nI + pl.program_id(1), schedule, refs[7:7 + n_ag],
                      refs[10 + n_ag:10 + 2 * n_ag], *refs[12 + 2 * n_ag:])

        @pl.when(pl.program_id(1) == 0)
        def _():
            edge_g[...] = jnp.zeros((8, CB), F32)
            edge_v[...] = jnp.zeros((8, CB), F32)

        h = h_ref[...]

        def conv(w_ref, cw_ref, cb_ref, edge, slot, cs):
            u = jnp.dot(h, w_ref[:, cs], preferred_element_type=F32)
            u_ref[slot, :, cs] = u.astype(BF16)
            prev8 = edge[:, cs]
            uc = (cw_ref[0:1, cs] * _shift_rows(u, 2, prev8, True) + cw_ref[1:2, cs] * _shift_rows(u, 1, prev8, True)
                  + cw_ref[2:3, cs] * u + cb_ref[:, cs])
            edge[:, cs] = u[tm - 8:tm, :]
            return uc

        cs = slice(0, CB)
        gc = conv(wg_ref, cwg_ref, cbg_ref, edge_g, 0, cs)
        vc = conv(wv_ref, cwv_ref, cbv_ref, edge_v, 1, cs)
        sig = _sigmoid(gc)
        silu = gc * sig
        a_ref[...] = (silu * vc).astype(BF16)
        ab_ref[0] = (vc * (sig * (1.0 + gc * (1.0 - sig)))).astype(BF16)
        ab_ref[1] = silu.astype(BF16)

    def wspec(off):
        return pl.BlockSpec((D, CB), lambda j, i: (0, j + off))

    def cspec(rows, off):
        return pl.BlockSpec((rows, CB), lambda j, i: (0, j + off))

    pair = pl.BlockSpec((2, tm, CB), lambda j, i: (0, i, j))
    in_specs = [pl.BlockSpec((tm, D), lambda j, i: (i, 0)), wspec(0), wspec(NJ),
                cspec(3, 0), cspec(3, NJ), cspec(1, 0), cspec(1, NJ)]
    out_specs = [pair, pair, pl.BlockSpec((tm, CB), lambda j, i: (i, j))]
    out_shape = [jax.ShapeDtypeStruct((2, S, FF), BF16), jax.ShapeDtypeStruct((2, S, FF), BF16),
                 jax.ShapeDtypeStruct((S, FF), BF16)]
    scratch = [pltpu.VMEM((8, CB), F32), pltpu.VMEM((8, CB), F32)]
    args = (h2, wup, wup, cw, cw, cb, cb)
    if ag:
        in_specs += ag.in_specs()
        out_specs += ag.out_specs()
        out_shape += ag.out_structs()
        scratch += ag.scratch()
        args += tuple(ag.srcs())
    return pl.pallas_call(
        body, name=name, grid=(NJ, nI), in_specs=in_specs, out_specs=out_specs, out_shape=out_shape,
        scratch_shapes=scratch, compiler_params=_cparams("arbitrary", "arbitrary"),
    )(*args)


def _ffn_da_bwd(name, dyb, wd, u3, ab3, cw):
    S, D = dyb.shape
    FF = wd.shape[0]
    CB = _tile(FF, 1408, LANES)
    NJ = FF // CB
    tm = _tile(S, 512, 16)
    nI = S // tm

    def body(dy_ref, wd_ref, u_ref, ab_ref, cwg_ref, cwv_ref, du_ref, dc_ref, edge_g, edge_v):
        i = pl.program_id(1)

        @pl.when(i == 0)
        def _():
            edge_g[...] = jnp.zeros((8, CB), F32)
            edge_v[...] = jnp.zeros((8, CB), F32)
            dc_ref[...] = jnp.zeros(dc_ref.shape, F32)

        dy = dy_ref[...]

        def back(slot, d_uc, edge, cw_ref, cs):
            u = u_ref[slot, :, cs].astype(F32)
            next8 = edge[:, cs]
            dp1 = _shift_rows(d_uc, 1, next8, False)
            dp2 = _shift_rows(d_uc, 2, next8, False)
            du = cw_ref[2:3, cs] * d_uc + cw_ref[1:2, cs] * dp1 + cw_ref[0:1, cs] * dp2
            edge[:, cs] = d_uc[0:8, :]
            du_ref[slot, :, cs] = du.astype(BF16)
            dc_ref[slot, 0:1, cs] += jnp.sum(dp2 * u, axis=0, keepdims=True)
            dc_ref[slot, 1:2, cs] += jnp.sum(dp1 * u, axis=0, keepdims=True)
            dc_ref[slot, 2:3, cs] += jnp.sum(d_uc * u, axis=0, keepdims=True)
            dc_ref[slot, 3:4, cs] += jnp.sum(d_uc, axis=0, keepdims=True)

        for cs in _col_chunks(CB):
            da = lax.dot_general(dy, wd_ref[cs, :], NT, preferred_element_type=F32)
            back(0, da * ab_ref[0, :, cs].astype(F32), edge_g, cwg_ref, cs)
            back(1, da * ab_ref[1, :, cs].astype(F32), edge_v, cwv_ref, cs)

    def rev(i):
        return nI - 1 - i

    pair = pl.BlockSpec((2, tm, CB), lambda j, i: (0, rev(i), j))
    return pl.pallas_call(
        body, name=name, grid=(NJ, nI),
        in_specs=[pl.BlockSpec((tm, D), lambda j, i: (rev(i), 0)),
                  pl.BlockSpec((CB, D), lambda j, i: (j, 0)), pair, pair,
                  pl.BlockSpec((3, CB), lambda j, i: (0, j)), pl.BlockSpec((3, CB), lambda j, i: (0, j + NJ))],
        out_specs=[pair, pl.BlockSpec((2, 8, CB), lambda j, i: (0, 0, j))],
        out_shape=[jax.ShapeDtypeStruct((2, S, FF), BF16), jax.ShapeDtypeStruct((2, 8, FF), F32)],
        scratch_shapes=[pltpu.VMEM((8, CB), F32) for _ in range(2)],
        compiler_params=_cparams("parallel", "arbitrary"),
    )(dyb, wd, u3, ab3, cw, cw)


def _ffn_down_fwd(name, a, wd, res):
    S, FF = a.shape
    D = wd.shape[1]
    tm = _tile(S, 512, 16)
    tk = _tile(FF, 2816, LANES)
    return _mm(name, a, wd, grid=(1, S // tm, FF // tk),
               a_spec=pl.BlockSpec((tm, tk), lambda p, q, k: (q, k)),
               b_spec=pl.BlockSpec((tk, D), lambda p, q, k: (k, 0)),
               o_spec=pl.BlockSpec((tm, D), lambda p, q, k: (q, 0)),
               out_shape=jax.ShapeDtypeStruct((S, D), F32), contract=NN, acc_shape=(tm, D), res=res)


def _ffn_dh_bwd(name, du3, wup, comm=None):
    _, S, FF = du3.shape
    D = wup.shape[0]
    tm = _tile(S, 512, 16)
    tk = _tile(FF, 2816, LANES)
    nh = FF // tk
    return _mm(name, du3, wup, grid=(1, S // tm, 2 * nh),
               a_spec=pl.BlockSpec((None, tm, tk), lambda p, q, k: (k // nh, q, k % nh)),
               b_spec=pl.BlockSpec((D, tk), lambda p, q, k: (0, k)),
               o_spec=pl.BlockSpec((tm, D), lambda p, q, k: (q, 0)),
               out_shape=jax.ShapeDtypeStruct((S, D), F32), contract=NT, acc_shape=(tm, D), comm=comm)


def _ffn_dwup_bwd(name, h2, du3, comm=None):
    S, D = h2.shape
    FF = du3.shape[2]
    NJ = NDEV // 2
    CB = FF // NJ
    tm, tk = _tile(D, 1024, LANES), _tile(S, 2048, 16)
    return _mm(name, h2, du3, grid=(NDEV, D // tm, S // tk),
               a_spec=pl.BlockSpec((tk, tm), lambda p, q, k: (k, q)),
               b_spec=pl.BlockSpec((None, tk, CB), lambda p, q, k: (p // NJ, k, p % NJ)),
               o_spec=pl.BlockSpec((None, tm, CB), lambda p, q, k: (p, q, 0)),
               out_shape=jax.ShapeDtypeStruct((NDEV, D, CB), F32), contract=TN, acc_shape=(tm, CB), comm=comm)


def _t5_onehot():
    i = np.arange(WINDOW)[:, None]
    j = np.arange(2 * WINDOW)[None, :]
    n = np.maximum(WINDOW + i - j, 0)
    max_exact = N_BUCKETS // 2
    nf = np.maximum(n, 1).astype(np.float32)
    large = max_exact + (np.log(nf / max_exact) / np.log(MAX_DISTANCE / max_exact)
                         * (N_BUCKETS - max_exact)).astype(np.int32)
    large = np.minimum(large, N_BUCKETS - 1)
    bucket = np.where(n < max_exact, n, large).astype(np.int32).reshape(-1)
    return (np.arange(N_BUCKETS)[:, None] == bucket[None, :]).astype(np.float32)


def _bias_band(rel_bias, onehot):
    H = rel_bias.shape[0]
    n = onehot.shape[1]

    def body(r_ref, oh_ref, o_ref):
        o_ref[...] = jnp.dot(r_ref[...], oh_ref[...], preferred_element_type=F32,
                             precision=lax.Precision.HIGHEST)

    return pl.pallas_call(body, name="bias_band", out_shape=jax.ShapeDtypeStruct((H, n), F32),
                          compiler_params=pltpu.CompilerParams(vmem_limit_bytes=VMEM_LIMIT))(rel_bias, onehot)


def _bias_band_bwd(dband, onehot):
    H = dband.shape[0]

    def body(d_ref, oh_ref, o_ref):
        o_ref[...] = lax.dot_general(d_ref[...], oh_ref[...], NT, preferred_element_type=F32,
                                     precision=lax.Precision.HIGHEST)

    return pl.pallas_call(body, name="bias_band_bwd", out_shape=jax.ShapeDtypeStruct((H, N_BUCKETS), F32),
                          compiler_params=pltpu.CompilerParams(vmem_limit_bytes=VMEM_LIMIT))(dband, onehot)


def _band_valid(n):
    i = lax.broadcasted_iota(jnp.int32, (WINDOW, 2 * WINDOW), 0)
    j = lax.broadcasted_iota(jnp.int32, (WINDOW, 2 * WINDOW), 1)
    return (j > i) & (j <= i + WINDOW) & ((n > 0) | (j >= WINDOW))


def _head_norm(v, gain):
    r = lax.rsqrt(jnp.mean(v * v, axis=-1, keepdims=True) + EPS)
    vhat = v * r
    return r, vhat, vhat * gain


HEAD_SUM_COLS = 256


def _head_sum_matrices():
    blk = np.arange(HEAD_SUM_COLS) // HEAD_DIM
    bd = (blk[:, None] == blk[None, :]).astype(np.float32)
    return jnp.asarray(bd, BF16), jnp.ones((2 * WINDOW, LANES), BF16)


def _head_sums(v, bd_ref):
    hi = v.astype(BF16)
    lo = (v - hi.astype(F32)).astype(BF16)
    bd = bd_ref[...]
    parts = []
    for c in range(0, v.shape[1], HEAD_SUM_COLS):
        cs = slice(c, c + HEAD_SUM_COLS)
        parts.append(jnp.dot(hi[:, cs], bd, preferred_element_type=F32) + jnp.dot(lo[:, cs], bd, preferred_element_type=F32))
    return jnp.concatenate(parts, axis=1)


def _stack_heads(t, kh):
    return jnp.concatenate([t[:, (kh * GQA_GROUP + g) * HEAD_DIM:(kh * GQA_GROUP + g + 1) * HEAD_DIM]
                            for g in range(GQA_GROUP)], axis=0)


def _unstack_heads(t8):
    return jnp.concatenate([t8[g * WINDOW:(g + 1) * WINDOW, :] for g in range(GQA_GROUP)], axis=1)


def _band_scores(qn, kn, b_ref, kh, valid, scale):
    s = lax.dot_general(qn, kn, NT, preferred_element_type=F32) * scale
    s = s.reshape(GQA_GROUP, WINDOW, 2 * WINDOW) + b_ref[kh * GQA_GROUP:(kh + 1) * GQA_GROUP]
    s = jnp.where(valid[None], s, NEG_INF)
    return s.reshape(GQA_GROUP * WINDOW, 2 * WINDOW)


def _sink_rows(s_ref, kh):
    return jnp.concatenate([jnp.broadcast_to(s_ref[kh * GQA_GROUP + g:kh * GQA_GROUP + g + 1, :], (WINDOW, 1))
                            for g in range(GQA_GROUP)], axis=0)


def _attn_fwd(name, qkv, bias, qg, kg, sinks, ag=None):
    S, QW = qkv.shape
    H = bias.shape[0]
    D = H * HEAD_DIM
    KV = H // GQA_GROUP
    kvw = QW - D
    kvb = D // kvw
    nb = S // WINDOW
    scale = HEAD_DIM ** -0.5
    n_ag = len(ag.items) if ag else 0
    schedule = (0, (9 * nb) // 20, (7 * nb) // 10, nb - 1)
    N_IN = 9
    bd, ones = _head_sum_matrices()

    def body(*refs):
        q_ref, kc_ref, kp_ref, b_ref, qg_ref, kg_ref, s_ref, bd_ref, ones_ref = refs[:N_IN]
        o_ref, l_ref = refs[N_IN + n_ag:N_IN + 2 + n_ag]
        n = pl.program_id(0)
        if ag:
            ag.run_at(n, schedule, refs[N_IN:N_IN + n_ag], refs[N_IN + 2 + n_ag:N_IN + 2 + 2 * n_ag],
                      *refs[N_IN + 2 + 2 * n_ag:])
        valid = _band_valid(n)
        q = q_ref[...]
        kvc = kc_ref[...]
        kvp = kp_ref[...]
        rq = lax.rsqrt(_head_sums(q * q, bd_ref) * (1.0 / HEAD_DIM) + EPS)
        qn_all = (q * rq * qg_ref[...]).astype(BF16)
        lane = lax.broadcasted_iota(jnp.int32, (WINDOW, H), 1)
        lse_all = jnp.zeros((WINDOW, H), F32)
        scores, vbs = [], []
        for kh in range(KV):
            ks = slice(kh * HEAD_DIM, (kh + 1) * HEAD_DIM)
            vs = slice((KV + kh) * HEAD_DIM, (KV + kh + 1) * HEAD_DIM)
            kb = jnp.concatenate([kvp[:, ks], kvc[:, ks]], axis=0)
            vb = jnp.concatenate([kvp[:, vs], kvc[:, vs]], axis=0).astype(BF16)
            kn = _head_norm(kb, kg_ref[...])[2].astype(BF16)
            vbs.append(vb)
            scores.append(_band_scores(_stack_heads(qn_all, kh), kn, b_ref, kh, valid, scale))
        sinks_ = [_sink_rows(s_ref, kh) for kh in range(KV)]
        ms = [jnp.maximum(jnp.max(s, axis=-1, keepdims=True), sk) for s, sk in zip(scores, sinks_)]
        ps = [jnp.exp(s - m).astype(BF16) for s, m in zip(scores, ms)]
        dens = [jnp.dot(p, ones_ref[...], preferred_element_type=F32)[:, :HEAD_DIM] + jnp.exp(sk - m)
                for p, sk, m in zip(ps, sinks_, ms)]
        outs = [_unstack_heads(jnp.dot(p, vb, preferred_element_type=F32) * (1.0 / den))
                for p, vb, den in zip(ps, vbs, dens)]
        for kh in range(KV):
            lse = ms[kh] + jnp.log(dens[kh][:, 0:1])
            for g in range(GQA_GROUP):
                lse_all = jnp.where(lane == kh * GQA_GROUP + g, lse[g * WINDOW:(g + 1) * WINDOW, :], lse_all)
        o_ref[...] = jnp.concatenate(outs, axis=1).astype(BF16)
        l_ref[...] = lse_all

    const2 = lambda shape: pl.BlockSpec(shape, lambda n: (0, 0))
    in_specs = [pl.BlockSpec((WINDOW, D), lambda n: (n, 0)),
                pl.BlockSpec((WINDOW, kvw), lambda n: (n, kvb)),
                pl.BlockSpec((WINDOW, kvw), lambda n: (jnp.maximum(n - 1, 0), kvb)),
                pl.BlockSpec(bias.shape, lambda n: (0, 0, 0)),
                const2((1, D)), const2((1, HEAD_DIM)), const2((H, 1)), const2(bd.shape), const2(ones.shape)]
    out_specs = [pl.BlockSpec((WINDOW, D), lambda n: (n, 0)), pl.BlockSpec((WINDOW, H), lambda n: (n, 0))]
    out_shape = [jax.ShapeDtypeStruct((S, D), BF16), jax.ShapeDtypeStruct((S, H), F32)]
    args = (qkv, qkv, qkv, bias, jnp.tile(qg, (1, H)), kg, sinks, bd, ones)
    scratch = []
    if ag:
        in_specs += ag.in_specs()
        out_specs += ag.out_specs()
        out_shape += ag.out_structs()
        scratch += ag.scratch()
        args += tuple(ag.srcs())
    return pl.pallas_call(
        body, name=name, grid=(nb,), in_specs=in_specs, out_specs=out_specs, out_shape=out_shape,
        scratch_shapes=scratch, compiler_params=_cparams("arbitrary" if ag else "parallel"),
    )(*args)


def _attn_bwd(name, qkv, do, lse, bias, qg, kg, sinks, comm=None):
    S, QW = qkv.shape
    H = bias.shape[0]
    D = H * HEAD_DIM
    KV = H // GQA_GROUP
    kvw = QW - D
    kvb = D // kvw
    nb = S // WINDOW
    scale = HEAD_DIM ** -0.5
    n_ci = len(comm.srcs) if comm else 0
    n_co = len(comm.out_structs) if comm else 0
    N_IN = 10
    GROUPS_TOGETHER = 2
    bd, _ = _head_sum_matrices()

    def body(*refs):
        q_ref, kc_ref, kp_ref, do_ref, l_ref, b_ref, qg_ref, kg_ref, s_ref, bd_ref = refs[:N_IN]
        dq_ref, dkv_ref, db_ref, ds_ref, dqg_ref, dkg_ref = refs[N_IN + n_ci:N_IN + 6 + n_ci]
        carry = refs[N_IN + 6 + n_ci + n_co]
        n = pl.program_id(0)
        if comm:
            comm.run(refs[N_IN:N_IN + n_ci], refs[N_IN + 6 + n_ci:N_IN + 6 + n_ci + n_co], refs[-2], refs[-1],
                     n == 0, n == nb)

        @pl.when(n == 0)
        def _():
            db_ref[...] = jnp.zeros(db_ref.shape, F32)
            ds_ref[...] = jnp.zeros(ds_ref.shape, F32)
            dqg_ref[...] = jnp.zeros(dqg_ref.shape, F32)
            dkg_ref[...] = jnp.zeros(dkg_ref.shape, F32)
            carry[...] = jnp.zeros(carry.shape, F32)

        @pl.when(n == nb)
        def _():
            dkv_ref[...] = carry[...].astype(BF16)

        @pl.when(n < nb)
        def _():
            valid = _band_valid(n)
            q = q_ref[...]
            kvc = kc_ref[...]
            kvp = kp_ref[...]
            do_all = do_ref[...]
            lse = l_ref[...]
            qgain = qg_ref[...]
            kgain = kg_ref[...]
            rq = lax.rsqrt(_head_sums(q * q, bd_ref) * (1.0 / HEAD_DIM) + EPS)
            qhat = q * rq
            qn_all = (qhat * qgain).astype(BF16)
            def run_groups(groups):
                idx = range(len(groups))
                heads = [slice(kh * GQA_GROUP, (kh + 1) * GQA_GROUP) for kh in groups]
                kbs = [jnp.concatenate([kvp[:, kh * HEAD_DIM:(kh + 1) * HEAD_DIM],
                                        kvc[:, kh * HEAD_DIM:(kh + 1) * HEAD_DIM]], axis=0) for kh in groups]
                vbs = [jnp.concatenate([kvp[:, (KV + kh) * HEAD_DIM:(KV + kh + 1) * HEAD_DIM],
                                        kvc[:, (KV + kh) * HEAD_DIM:(KV + kh + 1) * HEAD_DIM]], axis=0).astype(BF16)
                       for kh in groups]
                knorm = [_head_norm(kb, kgain) for kb in kbs]
                kns = [t[2].astype(BF16) for t in knorm]
                qns = [_stack_heads(qn_all, kh) for kh in groups]
                ss = [_band_scores(qns[i], kns[i], b_ref, groups[i], valid, scale) for i in idx]
                lse8 = [jnp.concatenate([lse[:, kh * GQA_GROUP + g:kh * GQA_GROUP + g + 1]
                                         for g in range(GQA_GROUP)], axis=0) for kh in groups]
                ps = [jnp.exp(s - l) for s, l in zip(ss, lse8)]
                do8 = [_stack_heads(do_all, kh) for kh in groups]
                dps = [lax.dot_general(d, vb, NT, preferred_element_type=F32) for d, vb in zip(do8, vbs)]
                deltas = [jnp.sum(p * dp, axis=-1, keepdims=True) for p, dp in zip(ps, dps)]
                dss = [p * (dp - dl) for p, dp, dl in zip(ps, dps, deltas)]
                for i in idx:
                    db_ref[heads[i]] += dss[i].reshape(GQA_GROUP, WINDOW, 2 * WINDOW)
                    psink = jnp.exp(_sink_rows(s_ref, groups[i]) - lse8[i])
                    ds_ref[heads[i], :] += -jnp.sum((psink * deltas[i]).reshape(GQA_GROUP, WINDOW, 1), axis=1)
                dsbs = [(ds * scale).astype(BF16) for ds in dss]
                dqn_p = [_unstack_heads(jnp.dot(dsb, kn, preferred_element_type=F32)) for dsb, kn in zip(dsbs, kns)]
                dkns = [lax.dot_general(dsb, qn, TN, preferred_element_type=F32) for dsb, qn in zip(dsbs, qns)]
                dv_p = [lax.dot_general(p.astype(BF16), d, TN, preferred_element_type=F32) for p, d in zip(ps, do8)]
                dkg_p = jnp.zeros((1, HEAD_DIM), F32)
                dk_p = []
                for i in idx:
                    rk, khat, _ = knorm[i]
                    dkg_p = dkg_p + jnp.sum(dkns[i] * khat, axis=0, keepdims=True)
                    dkh = dkns[i] * kgain
                    dk_p.append(rk * (dkh - khat * jnp.mean(dkh * khat, axis=-1, keepdims=True)))
                return dqn_p, dk_p, dv_p, dkg_p

            dqn_parts, dk_parts, dv_parts = [], [], []
            dkg = jnp.zeros((1, HEAD_DIM), F32)
            for g0 in range(0, KV, GROUPS_TOGETHER):
                dqn_p, dk_p, dv_p, dkg_p = run_groups(list(range(g0, min(g0 + GROUPS_TOGETHER, KV))))
                dqn_parts += dqn_p
                dk_parts += dk_p
                dv_parts += dv_p
                dkg = dkg + dkg_p
            dqn = jnp.concatenate(dqn_parts, axis=1)
            dqh = dqn * qgain
            dq = rq * (dqh - qhat * (_head_sums(dqh * qhat, bd_ref) * (1.0 / HEAD_DIM)))
            dq_ref[...] = dq.astype(BF16)
            dqg_ref[...] += jnp.sum(dqn * qhat, axis=0, keepdims=True)
            dkg_ref[...] += dkg
            dkv = jnp.concatenate(dk_parts + dv_parts, axis=1)
            dkv_ref[...] = (carry[...] + dkv[0:WINDOW, :]).astype(BF16)
            carry[...] = dkv[WINDOW:2 * WINDOW, :]

    cur = lambda n: jnp.minimum(n, nb - 1)
    const2 = lambda shape: pl.BlockSpec(shape, lambda n: (0, 0))
    in_specs = [pl.BlockSpec((WINDOW, D), lambda n: (cur(n), 0)),
                pl.BlockSpec((WINDOW, kvw), lambda n: (cur(n), kvb)),
                pl.BlockSpec((WINDOW, kvw), lambda n: (jnp.maximum(cur(n) - 1, 0), kvb)),
                pl.BlockSpec((WINDOW, D), lambda n: (cur(n), 0)),
                pl.BlockSpec((WINDOW, H), lambda n: (cur(n), 0)),
                pl.BlockSpec(bias.shape, lambda n: (0, 0, 0)),
                const2((1, D)), const2((1, HEAD_DIM)), const2((H, 1)), const2(bd.shape)]
    out_specs = [pl.BlockSpec((WINDOW, D), lambda n: (cur(n), 0)),
                 pl.BlockSpec((WINDOW, kvw), lambda n: (jnp.maximum(n - 1, 0), 0)),
                 pl.BlockSpec(bias.shape, lambda n: (0, 0, 0)),
                 const2((H, 1)), const2((1, D)), const2((1, HEAD_DIM))]
    out_shape = [jax.ShapeDtypeStruct((S, D), BF16), jax.ShapeDtypeStruct((S, kvw), BF16),
                 jax.ShapeDtypeStruct(bias.shape, F32), jax.ShapeDtypeStruct((H, 1), F32),
                 jax.ShapeDtypeStruct((1, D), F32), jax.ShapeDtypeStruct((1, HEAD_DIM), F32)]
    scratch = [pltpu.VMEM((WINDOW, kvw), F32)]
    args = (qkv, qkv, qkv, do, lse, bias, jnp.tile(qg, (1, H)), kg, sinks, bd)
    if comm:
        in_specs += comm.in_specs()
        out_specs += comm.out_specs()
        out_shape += comm.out_structs
        scratch += comm.scratch()
        args += tuple(comm.srcs)
    return pl.pallas_call(
        body, name=name, grid=(nb + 1,), in_specs=in_specs, out_specs=out_specs, out_shape=out_shape,
        scratch_shapes=scratch, compiler_params=_cparams("arbitrary"),
    )(*args)


def _window_sums(src, bufs, lo, n_rows, step_sign, col_groups):
    out = []
    for g, cols in enumerate(col_groups):
        prev = src
        for level in range(g + 1):
            k = step_sign * (1 << level)
            cur = bufs[level]
            cur[pl.ds(lo, n_rows), cols] = prev[pl.ds(lo, n_rows), cols] + prev[pl.ds(lo + k, n_rows), cols]
            prev = cur
        out.append(prev)
    return out


def _pool_fwd(name, x, gain, wp, scale):
    S, D = x.shape
    G, C = wp.shape[0], wp.shape[1]
    tm = _tile(S, 256, POOL_HALO)
    hb = tm // POOL_HALO
    HL = POOL_HALO
    groups = [slice(g * C, (g + 1) * C) for g in range(G)]

    def body(x_ref, xh_ref, g_ref, w_ref, sc_ref, o_ref, d_ref, ext, p2, p4, p8, p16):
        i = pl.program_id(0)
        gain_v = g_ref[...]
        xt = x_ref[...]
        h = _head_norm(xt, gain_v)[2]
        hh = _head_norm(xh_ref[...], gain_v)[2]
        ext[pl.ds(0, HL), :] = jnp.where(i > 0, hh, 0.0)
        ext[pl.ds(HL, tm), :] = h
        bufs = (p2, p4, p8, p16)
        for b in bufs:
            b[pl.ds(0, 8), :] = jnp.zeros((8, D), F32)
        sums = _window_sums(ext, bufs, 8, tm + HL - 8, -1, groups)
        t = i * tm + lax.broadcasted_iota(jnp.int32, (tm, 1), 0)
        for g, cols in enumerate(groups):
            cnt = jnp.minimum(t + 1, POOL_WINDOWS[g]).astype(F32)
            d = sums[g][pl.ds(HL, tm), cols] / cnt - h[:, cols]
            db = d.astype(BF16)
            d_ref[:, cols] = db
            y = jnp.dot(db, w_ref[g], preferred_element_type=F32)
            o_ref[:, cols] = xt[:, cols] + y * sc_ref[:, cols]

    row = pl.BlockSpec((tm, D), lambda i: (i, 0))
    vec = pl.BlockSpec((1, D), lambda i: (0, 0))
    return pl.pallas_call(
        body, name=name, grid=(S // tm,),
        in_specs=[row, pl.BlockSpec((HL, D), lambda i: (jnp.maximum(i * hb - 1, 0), 0)), vec,
                  pl.BlockSpec(wp.shape, lambda i: (0, 0, 0)), vec],
        out_specs=[row, row],
        out_shape=[jax.ShapeDtypeStruct((S, D), F32), jax.ShapeDtypeStruct((S, D), BF16)],
        scratch_shapes=[pltpu.VMEM((tm + HL, D), F32) for _ in range(5)],
        compiler_params=_cparams("parallel"),
    )(x, x, gain, wp, scale)


def _pool_bwd(name, dx1, x, gain, wp, scale, dsave):
    S, D = x.shape
    G, C = wp.shape[0], wp.shape[1]
    tm = _tile(S, 256, POOL_HALO)
    hb = tm // POOL_HALO
    HL = POOL_HALO
    nI = S // tm
    groups = [slice(g * C, (g + 1) * C) for g in range(G)]

    def body(dx_ref, dxh_ref, x_ref, g_ref, w_ref, sc_ref, ds_ref, o_ref, ob_ref, dw_ref, dsc_ref, dg_ref,
             ext, p2, p4, p8, p16):
        i = pl.program_id(0)

        @pl.when(i == 0)
        def _():
            dw_ref[...] = jnp.zeros(dw_ref.shape, F32)
            dsc_ref[...] = jnp.zeros(dsc_ref.shape, F32)
            dg_ref[...] = jnp.zeros(dg_ref.shape, F32)

        dx1t = dx_ref[...]
        sc = sc_ref[...]
        dys = (dx1t * sc).astype(BF16)
        dys_h = (dxh_ref[...] * sc).astype(BF16)
        t = i * tm + lax.broadcasted_iota(jnp.int32, (tm, 1), 0)
        th = (i + 1) * tm + lax.broadcasted_iota(jnp.int32, (HL, 1), 0)
        dds = []
        for g, cols in enumerate(groups):
            dsv = ds_ref[:, cols]
            y = jnp.dot(dsv, w_ref[g], preferred_element_type=F32)
            dsc_ref[:, cols] += jnp.sum(dx1t[:, cols] * y, axis=0, keepdims=True)
            dw_ref[g] += lax.dot_general(dsv, dys[:, cols], TN, preferred_element_type=F32)
            dd = lax.dot_general(dys[:, cols], w_ref[g], NT, preferred_element_type=F32)
            dd_h = lax.dot_general(dys_h[:, cols], w_ref[g], NT, preferred_element_type=F32)
            dds.append(dd)
            w = POOL_WINDOWS[g]
            ext[pl.ds(0, tm), cols] = dd / jnp.minimum(t + 1, w).astype(F32)
            e_h = dd_h / jnp.minimum(th + 1, w).astype(F32)
            ext[pl.ds(tm, HL), cols] = jnp.where(i < nI - 1, e_h, 0.0)
        bufs = (p2, p4, p8, p16)
        for b in bufs:
            b[pl.ds(tm + HL - 8, 8), :] = jnp.zeros((8, D), F32)
        sums = _window_sums(ext, bufs, 0, tm + HL - 8, 1, groups)
        dh = jnp.concatenate([sums[g][pl.ds(0, tm), cols] - dds[g] for g, cols in enumerate(groups)], axis=1)
        dx, dg = _rms_bwd_math(dh, x_ref[...], g_ref[...])
        dx = dx1t + dx
        o_ref[...] = dx
        ob_ref[...] = dx.astype(BF16)
        dg_ref[...] += dg

    row = pl.BlockSpec((tm, D), lambda i: (i, 0))
    vec = pl.BlockSpec((1, D), lambda i: (0, 0))
    last_h = S // HL - 1
    return pl.pallas_call(
        body, name=name, grid=(nI,),
        in_specs=[row, pl.BlockSpec((HL, D), lambda i: (jnp.minimum((i + 1) * hb, last_h), 0)), row, vec,
                  pl.BlockSpec(wp.shape, lambda i: (0, 0, 0)), vec, row],
        out_specs=[row, row, pl.BlockSpec(wp.shape, lambda i: (0, 0, 0)), vec, vec],
        out_shape=[jax.ShapeDtypeStruct((S, D), F32), jax.ShapeDtypeStruct((S, D), BF16),
                   jax.ShapeDtypeStruct(wp.shape, F32),
                   jax.ShapeDtypeStruct((1, D), F32), jax.ShapeDtypeStruct((1, D), F32)],
        scratch_shapes=[pltpu.VMEM((tm + HL, D), F32) for _ in range(5)],
        compiler_params=_cparams("arbitrary"),
    )(dx1, dx1, x, gain, wp, scale, dsave)


HBM_SPEC = pl.BlockSpec(memory_space=pltpu.HBM)


def _coords():
    return lax.axis_index("x"), lax.axis_index("y"), lax.axis_index("c")


class _AgItem:
    def __init__(self, src, out_struct, slot, half):
        self.src, self.out_struct, self.slot, self.half = src, out_struct, slot, half


def _rows_item(src, n_rows_total):
    r, ncol = src.shape
    return _AgItem(src, jax.ShapeDtypeStruct((n_rows_total, ncol), src.dtype),
                   lambda out, d: out.at[pl.ds(pl.multiple_of(d * r, 16), r), :],
                   lambda ref, h: ref.at[pl.ds(h * (r // 2), r // 2), :])


def _cols_item(src, n_cols_total):
    nrow, cb = src.shape
    return _AgItem(src, jax.ShapeDtypeStruct((nrow, n_cols_total), src.dtype),
                   lambda out, d: out.at[:, pl.ds(pl.multiple_of(d * cb, LANES), cb)],
                   lambda ref, h: ref.at[pl.ds(h * (nrow // 2), nrow // 2), :])


def _lead_item(src):
    return _AgItem(src, jax.ShapeDtypeStruct((NDEV,) + src.shape, src.dtype),
                   lambda out, d: out.at[d],
                   lambda ref, h: ref.at[pl.ds(h * (src.shape[0] // 2), src.shape[0] // 2)])


def _pool_item(src, c_total):
    g, pc, c = src.shape
    return _AgItem(src, jax.ShapeDtypeStruct((g, c_total, c), src.dtype),
                   lambda out, d: out.at[:, pl.ds(pl.multiple_of(d * pc, 16), pc), :],
                   lambda ref, h: ref.at[pl.ds(h * (g // 2), g // 2)])


class _AllGather:
    N_PHASES = 4

    def __init__(self, items):
        self.items = list(items)

    def srcs(self):
        return [it.src for it in self.items]

    def out_structs(self):
        return [it.out_struct for it in self.items]

    def in_specs(self):
        return [HBM_SPEC] * len(self.items)

    def out_specs(self):
        return [HBM_SPEC] * len(self.items)

    def scratch(self):
        n = len(self.items)
        return [pltpu.SemaphoreType.DMA((8 * n,)), pltpu.SemaphoreType.DMA((8 * n,)), pltpu.SemaphoreType.DMA((n,))]

    def phase(self, ph, ins, outs, send_sems, recv_sems, local_sems):
        x, y, c = _coords()
        me, xn = 4 * x + 2 * y + c, 4 * (1 - x) + 2 * y + c
        yn, dg = 4 * x + 2 * (1 - y) + c, 4 * (1 - x) + 2 * (1 - y) + c
        XN, YN, SB = (1 - x, y, c), (x, 1 - y, c), (x, y, 1 - c)
        sib = lambda blk: blk + 1 - 2 * c
        for o, it in enumerate(self.items):
            slot = lambda d, it=it, o=o: it.slot(outs[o], d)
            half = it.half
            table = [
                (ins[o], slot(me), XN, slot(xn)),
                (ins[o], slot(me), YN, slot(yn)),
                (half(slot(xn), 0), half(slot(xn), 0), YN, half(slot(dg), 0)),
                (half(slot(yn), 1), half(slot(yn), 1), XN, half(slot(dg), 1)),
                (ins[o], slot(me), SB, slot(sib(me))),
                (slot(xn), slot(xn), SB, slot(sib(xn))),
                (slot(yn), slot(yn), SB, slot(sib(yn))),
                (slot(dg), slot(dg), SB, slot(sib(dg))),
            ]

            def send(k, table=table, o=o):
                src, dst, peer, _ = table[k]
                return pltpu.make_async_remote_copy(src_ref=src, dst_ref=dst, send_sem=send_sems.at[8 * o + k],
                                                    recv_sem=recv_sems.at[8 * o + k], device_id=peer, device_id_type=MESH)

            def arrived(k, table=table, o=o):
                land = table[k][3]
                pltpu.make_async_remote_copy(src_ref=land, dst_ref=land, send_sem=send_sems.at[8 * o + k],
                                             recv_sem=recv_sems.at[8 * o + k], device_id=table[k][2],
                                             device_id_type=MESH).wait_recv()

            local = pltpu.make_async_copy(ins[o], slot(me), local_sems.at[o])
            if ph == 0:
                local.start()
                for k in (0, 1, 4):
                    send(k).start()
            elif ph == 1:
                arrived(0)
                send(2).start()
                send(5).start()
                arrived(1)
                send(3).start()
                send(6).start()
            elif ph == 2:
                arrived(2)
                arrived(3)
                send(7).start()
            else:
                for k in (4, 5, 6, 7):
                    arrived(k)
                for k in range(8):
                    send(k).wait_send()
                local.wait()

    def run_at(self, step, schedule, ins, outs, send_sems, recv_sems, local_sems):
        for ph in range(self.N_PHASES):
            @pl.when(step == schedule[ph])
            def _(ph=ph):
                self.phase(ph, ins, outs, send_sems, recv_sems, local_sems)


def _allgather_now(name, ag):
    n = len(ag.items)

    def body(*refs):
        for ph in range(ag.N_PHASES):
            ag.phase(ph, refs[:n], refs[n:2 * n], *refs[2 * n:])

    return pl.pallas_call(body, name=name, in_specs=ag.in_specs(), out_specs=ag.out_specs(),
                          out_shape=ag.out_structs(), scratch_shapes=ag.scratch())(*ag.srcs())


def _allgather_small(name, block):
    m_per, ncol = block.shape

    def body(x_ref, out_ref, send_sems, recv_sems, local_sem):
        x, y, c = _coords()
        me, sibling = (x, y, c), (x, y, 1 - c)
        chips = [(1 - x, y), (x, 1 - y), (1 - x, 1 - y)]

        def rows(px, py, pc):
            return out_ref.at[pl.ds((4 * px + 2 * py + pc) * m_per, m_per), :]

        def copy(k, block_of, to, src=None):
            return pltpu.make_async_remote_copy(
                src_ref=rows(*block_of) if src is None else src, dst_ref=rows(*block_of),
                send_sem=send_sems.at[k], recv_sem=recv_sems.at[k], device_id=to, device_id_type=MESH)

        mine = pltpu.make_async_copy(x_ref, rows(*me), local_sem)
        mine.start()
        first = [copy(0, me, sibling, src=x_ref)]
        first += [copy(1 + j, me, (*chip, c), src=x_ref) for j, chip in enumerate(chips)]
        for cp in first:
            cp.start()
        passed = [copy(4 + j, (*chip, c), sibling) for j, chip in enumerate(chips)]
        for j, chip in enumerate(chips):
            copy(1 + j, (*chip, c), me).wait_recv()
            passed[j].start()
        copy(0, sibling, me).wait_recv()
        for j, chip in enumerate(chips):
            copy(4 + j, (*chip, 1 - c), me).wait_recv()
        for cp in first + passed:
            cp.wait_send()
        mine.wait()

    return pl.pallas_call(
        body, name=name, out_shape=jax.ShapeDtypeStruct((NDEV * m_per, ncol), block.dtype),
        in_specs=[pl.BlockSpec(memory_space=pltpu.VMEM)], out_specs=pl.BlockSpec(memory_space=pltpu.VMEM),
        scratch_shapes=[pltpu.SemaphoreType.DMA((7,)), pltpu.SemaphoreType.DMA((7,)), pltpu.SemaphoreType.DMA],
        compiler_params=pltpu.CompilerParams(vmem_limit_bytes=VMEM_LIMIT),
    )(block)


def _exchange_now(name, ex):
    n_in, n_out = len(ex.srcs), len(ex.out_structs)

    def body(*refs):
        ex.run(refs[:n_in], refs[n_in:n_in + n_out], refs[n_in + n_out], refs[n_in + n_out + 1], True, True)

    return pl.pallas_call(body, name=name, in_specs=ex.in_specs(), out_specs=ex.out_specs(),
                          out_shape=ex.out_structs, scratch_shapes=ex.scratch())(*ex.srcs)


def _rs_stage_c(gs):
    def plan(x, y, c, ins, outs):
        sib = (x, y, 1 - c)
        return [(g.at[q, 1 - c], r.at[q], sib) for g, r in zip(ins, outs) for q in range(4)]

    outs = [jax.ShapeDtypeStruct((4,) + g.shape[2:], g.dtype) for g in gs]
    return _Exchange(gs, outs, plan, 4 * len(gs))


def _rs_stage_ici(sends, first):
    def plan(x, y, c, ins, outs):
        XN, YN = (1 - x, y, c), (x, 1 - y, c)
        peers = (YN, XN) if first else (XN, YN)
        return [(s.at[h], r.at[h], peers[h]) for s, r in zip(ins, outs) for h in range(2)]

    outs = [jax.ShapeDtypeStruct(s.shape, s.dtype) for s in sends]
    return _Exchange(sends, outs, plan, 2 * len(sends))


def _coord_vec():
    x, y, c = _coords()
    return jnp.stack([x, y, c]).astype(jnp.int32)


def _rs_add1(name, g, r1, coords):
    R, L = g.shape[3], g.shape[4]
    tr = _tile(R, 512, 16)

    def qk(h, idx, cr):
        return jnp.where(h == 0, 2 * idx + cr[1], 2 * cr[0] + idx)

    def qs(h, idx, cr):
        return jnp.where(h == 0, 2 * idx + 1 - cr[1], 2 * (1 - cr[0]) + idx)

    def body(cr, gk, rk, gsd, rsd, keep, send):
        keep[...] = gk[...] + rk[...]
        send[...] = (gsd[...] + rsd[...]).astype(BF16)

    gspec = lambda qf: pl.BlockSpec((None, None, None, tr, L), lambda h, idx, r, cr: (qf(h, idx, cr), cr[2], h, r, 0))
    rspec = lambda qf: pl.BlockSpec((None, None, tr, L), lambda h, idx, r, cr: (qf(h, idx, cr), h, r, 0))
    ospec = pl.BlockSpec((None, None, tr, L), lambda h, idx, r, cr: (h, idx, r, 0))
    return pl.pallas_call(
        body, name=name,
        grid_spec=pltpu.PrefetchScalarGridSpec(
            num_scalar_prefetch=1, grid=(2, 2, R // tr),
            in_specs=[gspec(qk), rspec(qk), gspec(qs), rspec(qs)], out_specs=[ospec, ospec]),
        out_shape=[jax.ShapeDtypeStruct((2, 2, R, L), F32), jax.ShapeDtypeStruct((2, 2, R, L), BF16)],
        compiler_params=_cparams("parallel", "parallel", "parallel"),
    )(coords, g, r1, g, r1)


def _rs_add2(name, keep2, recv2, coords):
    R, L = keep2.shape[2], keep2.shape[3]
    tr = _tile(R, 512, 16)

    def mine(h, cr):
        return jnp.where(h == 0, cr[0], cr[1])

    def body(cr, kk, rk, ks, rs, keep, send):
        keep[...] = kk[...] + rk[...].astype(F32)
        send[...] = (ks[...] + rs[...].astype(F32)).astype(BF16)

    sel = lambda f: pl.BlockSpec((None, None, tr, L), lambda h, r, cr: (h, f(h, cr), r, 0))
    ospec = pl.BlockSpec((None, tr, L), lambda h, r, cr: (h, r, 0))
    other = lambda h, cr: 1 - mine(h, cr)
    return pl.pallas_call(
        body, name=name,
        grid_spec=pltpu.PrefetchScalarGridSpec(
            num_scalar_prefetch=1, grid=(2, R // tr),
            in_specs=[sel(mine), sel(mine), sel(other), sel(other)], out_specs=[ospec, ospec]),
        out_shape=[jax.ShapeDtypeStruct((2, R, L), F32), jax.ShapeDtypeStruct((2, R, L), BF16)],
        compiler_params=_cparams("parallel", "parallel"),
    )(coords, keep2, recv2, keep2, recv2)


def _rs_add3(name, keep3, recv3):
    R, L = keep3.shape[1], keep3.shape[2]
    tr = _tile(R, 512, 16)

    def body(k, r, o):
        o[...] = k[...] + r[...].astype(F32)

    spec = pl.BlockSpec((None, tr, L), lambda h, r: (h, r, 0))
    return pl.pallas_call(body, name=name, grid=(2, R // tr), in_specs=[spec, spec], out_specs=spec,
                          out_shape=jax.ShapeDtypeStruct((2, R, L), F32),
                          compiler_params=_cparams("parallel", "parallel"))(keep3, recv3)


class _ReduceScatter:
    N_STAGES = 3

    def __init__(self, tag, gs, coords):
        self.tag, self.coords, self.stage, self.result = tag, coords, 0, None
        self.full = []
        for g in gs:
            per = int(np.prod(g.shape[1:]))
            L = g.shape[-1]
            self.full.append(g.reshape(4, 2, 2, per // (2 * L), L))
        self.keep, self.send = None, None

    def exchange(self):
        if self.stage == 0:
            return _rs_stage_c(self.full)
        return _rs_stage_ici(list(self.send), self.stage == 1)

    def absorb(self, recv):
        names = [f"rs_add{self.stage + 1}_{self.tag}_{k}" for k in range(len(self.full))]
        if self.stage == 0:
            pairs = [_rs_add1(nm, g, r, self.coords) for nm, g, r in zip(names, self.full, recv)]
            self.keep, self.send = zip(*pairs)
        elif self.stage == 1:
            pairs = [_rs_add2(nm, kp, r, self.coords) for nm, kp, r in zip(names, self.keep, recv)]
            self.keep, self.send = zip(*pairs)
        else:
            self.result = [_rs_add3(nm, kp, r) for nm, kp, r in zip(names, self.keep, recv)]
        self.stage += 1

    def finish_now(self):
        while self.stage < self.N_STAGES:
            self.absorb(_exchange_now(f"rs_x{self.stage}_{self.tag}", self.exchange()))
        return self.result


def _adamw(name, w, g, m, v):
    R, L = w.shape
    tr = _tile(R, 256, 8)

    def body(w_ref, g_ref, m_ref, v_ref, d_ref, nm_ref, nv_ref):
        gv = g_ref[...]
        nm = ADAM_B1 * m_ref[...] + (1.0 - ADAM_B1) * gv
        nv = ADAM_B2 * v_ref[...] + (1.0 - ADAM_B2) * (gv * gv)
        m_hat = nm / (1.0 - ADAM_B1 ** ADAM_STEP)
        v_hat = nv / (1.0 - ADAM_B2 ** ADAM_STEP)
        d_ref[...] = -ADAM_LR * (m_hat / (jnp.sqrt(v_hat) + ADAM_EPS) + ADAM_WD * w_ref[...])
        nm_ref[...] = nm
        nv_ref[...] = nv

    spec = pl.BlockSpec((tr, L), lambda i: (i, 0))
    out = jax.ShapeDtypeStruct((R, L), F32)
    return pl.pallas_call(body, name=name, grid=(R // tr,), in_specs=[spec] * 4, out_specs=[spec] * 3,
                          out_shape=[out, out, out], compiler_params=_cparams("parallel"))(w, g, m, v)


def _sum_devices(name, gathered):
    _, R, L = gathered.shape
    tr = _tile(R, 512, 8)

    def body(g_ref, o_ref):
        acc = g_ref[0]
        for d in range(1, NDEV):
            acc = acc + g_ref[d]
        o_ref[...] = acc

    return pl.pallas_call(body, name=name, grid=(R // tr,),
                          in_specs=[pl.BlockSpec((NDEV, tr, L), lambda i: (0, i, 0))],
                          out_specs=pl.BlockSpec((tr, L), lambda i: (i, 0)),
                          out_shape=jax.ShapeDtypeStruct((R, L), F32),
                          compiler_params=_cparams("parallel"))(gathered)


def _pack(parts):
    flat, offs, pos = [], [], 0
    for p in parts:
        n = int(np.prod(p.shape))
        padded = -(-n // PACK_ALIGN) * PACK_ALIGN
        flat.append(jnp.pad(p.reshape(-1).astype(F32), (0, padded - n)))
        offs.append((pos, n, p.shape))
        pos += padded
    return jnp.concatenate(flat).reshape(-1, LANES), offs


def _unpack(packed, offs):
    flat = packed.reshape(-1)
    return [flat[pos:pos + n].reshape(shape) for pos, n, shape in offs]


def kernel(x, norm_mix, norm_ffn, rel_bias, attn_w_qkv, attn_q_gain, attn_k_gain, attn_sinks, attn_w_o, pool_w, pool_scale, ffn_w_up, ffn_conv_w, ffn_conv_b, ffn_w_down, loss_target, m_norm_mix, m_norm_ffn, m_rel_bias, m_attn_w_qkv, m_attn_q_gain, m_attn_k_gain, m_attn_sinks, m_attn_w_o, m_pool_w, m_pool_scale, m_ffn_w_up, m_ffn_conv_w, m_ffn_conv_b, m_ffn_w_down, v_norm_mix, v_norm_ffn, v_rel_bias, v_attn_w_qkv, v_attn_q_gain, v_attn_k_gain, v_attn_sinks, v_attn_w_o, v_pool_w, v_pool_scale, v_ffn_w_up, v_ffn_conv_w, v_ffn_conv_b, v_ffn_w_down):
    xs = x[0]
    target = loss_target[0]
    S, D = xs.shape
    depth = norm_mix.shape[0]
    H = D // HEAD_DIM
    n_attn, n_pool = attn_w_qkv.shape[0], pool_w.shape[0]
    QS = attn_w_qkv.shape[2]
    CB = ffn_w_up.shape[2]
    FB = ffn_w_down.shape[1]
    FF = FB * NDEV
    G, PC, C = pool_w.shape[1], pool_w.shape[2], pool_w.shape[3]
    xi, yi, ci = _coords()
    me = 4 * xi + 2 * yi + ci
    coords = _coord_vec()

    def ffn_allgather(i):
        return _AllGather([_cols_item(ffn_w_up[i].astype(BF16), NDEV * CB),
                           _rows_item(ffn_w_down[i].astype(BF16), FF)])

    def mixer_allgather(i):
        j = i // 2
        if i % 2 == 0:
            return _AllGather([_lead_item(attn_w_qkv[j].astype(BF16)), _rows_item(attn_w_o[j].astype(BF16), D)])
        return _AllGather([_pool_item(pool_w[j].astype(BF16), C)])

    def layer_allgather(i):
        return _AllGather(ffn_allgather(i).items + mixer_allgather(i).items)

    def unpack_ffn(outs):
        return {"up": outs[0], "down": outs[1]}

    def unpack_mixer(i, outs):
        if i % 2 == 0:
            return {"qkv": outs[0].transpose(1, 0, 2).reshape(D, NDEV * QS), "o": outs[1]}
        return {"pool": outs[0]}

    weights_of = [None] * depth
    weights_of[0] = unpack_mixer(0, _allgather_now("allgather_mixer_0", mixer_allgather(0)))

    small_in, small_in_offs = _pack([ffn_conv_w, pool_scale])
    gathered_in = _allgather_small("allgather_small_params", small_in).reshape(NDEV, -1)
    per_dev = [_unpack(gathered_in[d], small_in_offs) for d in range(NDEV)]
    conv_w_full = jnp.concatenate([p[0] for p in per_dev], axis=2)
    pool_scale_full = jnp.concatenate([p[1] for p in per_dev], axis=1)

    onehot = jnp.asarray(_t5_onehot())
    bias = _bias_band(rel_bias, onehot).reshape(H, WINDOW, 2 * WINDOW)

    saved = []
    cur = xs
    for i in range(depth):
        j = i // 2
        w = weights_of[i]
        st = {"x0": cur}
        if i % 2 == 0:
            h = _rmsnorm_fwd(f"norm_mix_{i}", cur, norm_mix[i:i + 1])
            qkv = _mm_rows(f"qkv_{i}", h, w["qkv"], NN, F32)
            qg, kg, sk = attn_q_gain[j:j + 1], attn_k_gain[j:j + 1], attn_sinks[j].reshape(H, 1)
            ag0 = _AllGather(ffn_allgather(0).items[:1]) if i == 0 else None
            o, lse, *gathered_up = _attn_fwd(f"attn_fwd_{i}", qkv, bias, qg, kg, sk, ag0)
            if ag0:
                w["up"] = gathered_up[0]
            x1 = _mm_rows(f"attn_out_{i}", o, w["o"], NN, F32, res=cur)
            st.update(h=h, qkv=qkv, o=o, lse=lse)
        else:
            x1, dsave = _pool_fwd(f"pool_fwd_{i}", cur, norm_mix[i:i + 1], w["pool"], pool_scale_full[j:j + 1])
            st.update(dsave=dsave)
        h2 = _rmsnorm_fwd(f"norm_ffn_{i}", x1, norm_ffn[i:i + 1])
        own_down = ffn_allgather(i).items[1:] if "down" not in w else []
        ag = _AllGather(own_down + (layer_allgather(i + 1).items if i + 1 < depth else []))
        ag = ag if ag.items else None
        u3, ab3, a, *gathered_next = _ffn_up_fwd(f"ffn_up_{i}", h2, w["up"], conv_w_full[i], ffn_conv_b[i:i + 1], ag)
        if own_down:
            w["down"] = gathered_next.pop(0)
        if i + 1 < depth:
            weights_of[i + 1] = {**unpack_ffn(gathered_next[:2]), **unpack_mixer(i + 1, gathered_next[2:])}
        cur = _ffn_down_fwd(f"ffn_down_{i}", a, w["down"], x1)
        st.update(x1=x1, h2=h2, u3=u3, ab3=ab3, a=a)
        saved.append(st)

    loss_tile, dcur, dcur_b = _loss_fwd_bwd(cur, target)

    g_up_l, g_down_l = [None] * depth, [None] * depth
    g_qkv_l, g_o_l, g_pool_l = [None] * n_attn, [None] * n_attn, [None] * n_pool
    d_norm_mix, d_norm_ffn = [None] * depth, [None] * depth
    d_conv_w, d_conv_b = [None] * depth, [None] * depth
    d_qg, d_kg, d_sinks, d_pscale = [None] * n_attn, [None] * n_attn, [None] * n_attn, [None] * n_pool
    d_band = None

    def store_ffn(i, red):
        g_up_l[i] = red[0].reshape(D, CB)
        g_down_l[i] = red[1].reshape(FB, D)

    def store_mixer(i, red):
        if i % 2 == 0:
            g_qkv_l[i // 2] = red[0].reshape(D, QS)
            g_o_l[i // 2] = red[1].reshape(D // NDEV, D)
        else:
            g_pool_l[i // 2] = red[0].reshape(G, PC, C)

    def carrying(rs, call):
        if rs is None or rs.stage >= rs.N_STAGES:
            return call(None)
        out, *recv = call(rs.exchange())
        rs.absorb(recv)
        return out

    pending = None
    for i in reversed(range(depth)):
        j = i // 2
        st = saved[i]
        w = weights_of[i]
        du3, dc = _ffn_da_bwd(f"ffn_da_{i}", dcur_b, w["down"], st["u3"], st["ab3"], conv_w_full[i])
        d_conv_w[i] = jnp.concatenate([dc[0, 0:3], dc[1, 0:3]], axis=1)
        d_conv_b[i] = jnp.concatenate([dc[0, 3], dc[1, 3]], axis=0)
        dwdown = carrying(pending, lambda ex: _mm_tn(f"ffn_dwdown_{i}", st["a"], dcur_b,
                                                     tm_pref=CB if CB % LANES == 0 else 1024, comm=ex))
        dwdown = dwdown.reshape(NDEV, FB, D)
        dwup = carrying(pending, lambda ex: _ffn_dwup_bwd(f"ffn_dwup_{i}", st["h2"], du3, comm=ex))
        dh2 = carrying(pending, lambda ex: _ffn_dh_bwd(f"ffn_dh_{i}", du3, w["up"], comm=ex))
        if pending is not None:
            red = pending.finish_now()
            store_ffn(i + 1, red[:2])
            store_mixer(i + 1, red[2:])
        dx1, dx1_b, dg = _rmsnorm_bwd(f"norm_ffn_bwd_{i}", dh2, st["x1"], norm_ffn[i:i + 1], dcur)
        d_norm_ffn[i] = dg[0]
        own = _ReduceScatter(f"l{i}f", [dwup, dwdown], coords) if i == 0 else None
        if i % 2 == 0:
            do = _mm_rows(f"attn_do_{i}", dx1_b, w["o"], NT, BF16)
            dwo = carrying(own, lambda ex: _mm_tn(f"attn_dwo_{i}", st["o"], dx1_b, comm=ex)).reshape(NDEV, D // NDEV, D)
            qg, kg, sk = attn_q_gain[j:j + 1], attn_k_gain[j:j + 1], attn_sinks[j].reshape(H, 1)
            ex = own.exchange() if own is not None and own.stage < own.N_STAGES else None
            dq, dkv, db, dsk, dqg, dkg, *recv = _attn_bwd(f"attn_bwd_{i}", st["qkv"], do, st["lse"], bias, qg, kg, sk,
                                                          comm=ex)
            if ex is not None:
                own.absorb(recv)
            d_band = db if d_band is None else d_band + db
            d_sinks[j], d_qg[j], d_kg[j] = dsk[:, 0], dqg.reshape(H, HEAD_DIM).sum(axis=0), dkg[0]
            dqkv = jnp.concatenate([dq, dkv], axis=1)
            dwqkv = carrying(own, lambda ex: _mm_tn(f"attn_dwqkv_{i}", st["h"], dqkv, tn_pref=1280, comm=ex))
            dwqkv = dwqkv.reshape(D, NDEV, QS).transpose(1, 0, 2)
            dh = _mm_rows(f"attn_dh_{i}", dqkv, w["qkv"], NT, F32)
            dcur, dcur_b, dg = _rmsnorm_bwd(f"norm_mix_bwd_{i}", dh, st["x0"], norm_mix[i:i + 1], dx1)
            mixer_grads = [dwqkv, dwo]
        else:
            dcur, dcur_b, dwp, dps, dg = _pool_bwd(f"pool_bwd_{i}", dx1, st["x0"], norm_mix[i:i + 1], w["pool"],
                                                   pool_scale_full[j:j + 1], st["dsave"])
            d_pscale[j] = dps[0]
            mixer_grads = [dwp.reshape(G, NDEV, PC, C).transpose(1, 0, 2, 3)]
        d_norm_mix[i] = dg[0]
        if own is not None:
            store_ffn(i, own.finish_now())
            store_mixer(i, _ReduceScatter(f"l{i}m", mixer_grads, coords).finish_now())
        else:
            pending = _ReduceScatter(f"l{i}", [dwup, dwdown] + mixer_grads, coords)

    d_rel = _bias_band_bwd(d_band.reshape(H, -1), onehot)

    small_parts = [loss_tile, jnp.stack(d_norm_mix), jnp.stack(d_norm_ffn), d_rel, jnp.stack(d_qg),
                   jnp.stack(d_kg), jnp.stack(d_sinks), jnp.stack(d_pscale), jnp.stack(d_conv_w),
                   jnp.stack(d_conv_b)]
    small, small_offs = _pack(small_parts)
    gathered = _allgather_small("allgather_small_grads", small).reshape(NDEV, -1, LANES)
    summed = _unpack(_sum_devices("sum_small_grads", gathered), small_offs)
    loss = summed[0][0, 0]
    (g_norm_mix, g_norm_ffn, g_rel, g_qg, g_kg, g_sinks, g_pscale_full, g_conv_w_full, g_conv_b) = summed[1:]
    g_pscale = lax.dynamic_slice_in_dim(g_pscale_full, me * (D // NDEV), D // NDEV, axis=1)
    g_conv_w = lax.dynamic_slice_in_dim(g_conv_w_full, me * CB, CB, axis=2)

    grads = {
        "norm_mix": g_norm_mix, "norm_ffn": g_norm_ffn, "rel_bias": g_rel, "attn_w_qkv": jnp.stack(g_qkv_l),
        "attn_q_gain": g_qg, "attn_k_gain": g_kg, "attn_sinks": g_sinks, "attn_w_o": jnp.stack(g_o_l),
        "pool_w": jnp.stack(g_pool_l), "pool_scale": g_pscale, "ffn_w_up": jnp.stack(g_up_l),
        "ffn_conv_w": g_conv_w, "ffn_conv_b": g_conv_b, "ffn_w_down": jnp.stack(g_down_l),
    }
    weights = {
        "norm_mix": (norm_mix, m_norm_mix, v_norm_mix), "norm_ffn": (norm_ffn, m_norm_ffn, v_norm_ffn),
        "rel_bias": (rel_bias, m_rel_bias, v_rel_bias), "attn_w_qkv": (attn_w_qkv, m_attn_w_qkv, v_attn_w_qkv),
        "attn_q_gain": (attn_q_gain, m_attn_q_gain, v_attn_q_gain),
        "attn_k_gain": (attn_k_gain, m_attn_k_gain, v_attn_k_gain),
        "attn_sinks": (attn_sinks, m_attn_sinks, v_attn_sinks), "attn_w_o": (attn_w_o, m_attn_w_o, v_attn_w_o),
        "pool_w": (pool_w, m_pool_w, v_pool_w), "pool_scale": (pool_scale, m_pool_scale, v_pool_scale),
        "ffn_w_up": (ffn_w_up, m_ffn_w_up, v_ffn_w_up), "ffn_conv_w": (ffn_conv_w, m_ffn_conv_w, v_ffn_conv_w),
        "ffn_conv_b": (ffn_conv_b, m_ffn_conv_b, v_ffn_conv_b), "ffn_w_down": (ffn_w_down, m_ffn_w_down, v_ffn_w_down),
    }
    names = list(weights)
    big = ("attn_w_qkv", "attn_w_o", "pool_w", "ffn_w_up", "ffn_w_down")
    upd = {}
    for nm in big:
        w, m, v = weights[nm]
        two_d = lambda t: t.reshape(-1, w.shape[-1])
        d_, m_, v_ = _adamw(f"adamw_{nm}", two_d(w), two_d(grads[nm]), two_d(m), two_d(v))
        upd[nm] = (d_.reshape(w.shape), m_.reshape(w.shape), v_.reshape(w.shape))
    small_names = [nm for nm in names if nm not in big]
    pw, offs = _pack([weights[nm][0] for nm in small_names])
    pg, _ = _pack([grads[nm] for nm in small_names])
    pm, _ = _pack([weights[nm][1] for nm in small_names])
    pv, _ = _pack([weights[nm][2] for nm in small_names])
    d_, m_, v_ = _adamw("adamw_small", pw, pg, pm, pv)
    for nm, dd, mm, vv in zip(small_names, _unpack(d_, offs), _unpack(m_, offs), _unpack(v_, offs)):
        upd[nm] = (dd, mm, vv)

    grad_x = dcur[None]
    return (loss, grad_x, *[grads[nm].reshape(weights[nm][0].shape) for nm in names],
            *[upd[nm][0] for nm in names], *[upd[nm][1] for nm in names], *[upd[nm][2] for nm in names])
```

```python
import numpy as np
import jax
import jax.numpy as jnp
from jax import lax
from jax.experimental import pallas as pl
from jax.experimental.pallas import tpu as pltpu

F32 = jnp.float32
BF16 = jnp.bfloat16
MESH = pl.DeviceIdType.MESH

NDEV = 8
HEAD_DIM = 64
GQA_GROUP = 8
WINDOW = 128
N_BUCKETS = 32
MAX_DISTANCE = 128
POOL_WINDOWS = (2, 4, 8, 16)
POOL_HALO = 32
EPS = 1e-6
NEG_INF = -1e30
ADAM_LR = 0.001
ADAM_B1 = 0.9
ADAM_B2 = 0.999
ADAM_EPS = 1e-08
ADAM_WD = 0.01
ADAM_STEP = 10

V7X_VMEM_BYTES = 64 * 1024 * 1024
VMEM_LIMIT = V7X_VMEM_BYTES - 8 * 1024 * 1024
LANES = 128
SUBLANES = 8
PACK_ALIGN = SUBLANES * LANES

NN = (((1,), (0,)), ((), ()))
NT = (((1,), (1,)), ((), ()))
TN = (((0,), (0,)), ((), ()))


def _tile(dim, pref, align):
    t = min(pref, dim)
    t -= t % align
    while t >= align:
        if dim % t == 0:
            return t
        t -= align
    return dim


def _cparams(*sem):
    return pltpu.CompilerParams(dimension_semantics=sem, vmem_limit_bytes=VMEM_LIMIT)


def _bf(v):
    return v if v.dtype == BF16 else v.astype(BF16)


class _Exchange:
    def __init__(self, srcs, out_structs, plan, n_copies):
        self.srcs, self.out_structs, self.plan, self.n_copies = list(srcs), list(out_structs), plan, n_copies

    def in_specs(self):
        return [HBM_SPEC] * len(self.srcs)

    def out_specs(self):
        return [HBM_SPEC] * len(self.out_structs)

    def scratch(self):
        return [pltpu.SemaphoreType.DMA((self.n_copies,)), pltpu.SemaphoreType.DMA((self.n_copies,))]

    def run(self, in_refs, out_refs, send_sems, recv_sems, is_first, is_last):
        def copies():
            x, y, c = _coords()
            return [pltpu.make_async_remote_copy(src_ref=src, dst_ref=dst, send_sem=send_sems.at[k],
                                                 recv_sem=recv_sems.at[k], device_id=peer, device_id_type=MESH)
                    for k, (src, dst, peer) in enumerate(self.plan(x, y, c, in_refs, out_refs))]

        def start():
            for cp in copies():
                cp.start()

        def finish():
            cps = copies()
            for cp in cps:
                cp.wait_recv()
            for cp in cps:
                cp.wait_send()

        if is_first is True and is_last is True:
            start()
            finish()
        else:
            pl.when(is_first)(start)
            pl.when(is_last)(finish)


def _grid_edges(grid):
    first, last = True, True
    for ax, n in enumerate(grid):
        first = jnp.logical_and(first, pl.program_id(ax) == 0)
        last = jnp.logical_and(last, pl.program_id(ax) == n - 1)
    return first, last


def _mm(name, a, b, *, grid, a_spec, b_spec, o_spec, out_shape, contract, acc_shape, res=None, comm=None):
    nk = grid[2]
    n_main = 3 if res is not None else 2
    n_ci = len(comm.srcs) if comm else 0
    n_co = len(comm.out_structs) if comm else 0

    def body(*refs):
        a_ref, b_ref = refs[:2]
        r_ref = refs[2] if res is not None else None
        o_ref = refs[n_main + n_ci]
        scr = refs[n_main + n_ci + 1 + n_co:]
        if comm:
            first, last = _grid_edges(grid)
            comm.run(refs[n_main:n_main + n_ci], refs[n_main + n_ci + 1:n_main + n_ci + 1 + n_co],
                     scr[-2], scr[-1], first, last)
        part = lax.dot_general(_bf(a_ref[...]), _bf(b_ref[...]), contract, preferred_element_type=F32)

        def finish(acc):
            if r_ref is not None:
                acc = acc + r_ref[...]
            o_ref[...] = acc.astype(o_ref.dtype)

        if nk == 1:
            finish(part)
        else:
            acc_ref = scr[0]
            k = pl.program_id(2)

            @pl.when(k == 0)
            def _():
                acc_ref[...] = part

            @pl.when(k > 0)
            def _():
                acc_ref[...] += part

            @pl.when(k == nk - 1)
            def _():
                finish(acc_ref[...])

    in_specs = [a_spec, b_spec] + ([o_spec] if res is not None else [])
    args = (a, b) + ((res,) if res is not None else ())
    out_specs, out_shapes = o_spec, out_shape
    scratch = [pltpu.VMEM(acc_shape, F32)] if nk > 1 else []
    if comm:
        in_specs += comm.in_specs()
        args += tuple(comm.srcs)
        out_specs = [o_spec] + comm.out_specs()
        out_shapes = [out_shape] + comm.out_structs
        scratch += comm.scratch()
    return pl.pallas_call(
        body, name=name, grid=grid, in_specs=in_specs, out_specs=out_specs, out_shape=out_shapes,
        scratch_shapes=scratch,
        compiler_params=_cparams(*(("arbitrary",) * 3 if comm else ("parallel", "parallel", "arbitrary"))),
    )(*args)


def _mm_rows(name, a, b, contract, out_dtype, res=None, tm_pref=512, comm=None):
    S, K = a.shape
    N = b.shape[1] if contract == NN else b.shape[0]
    tm = _tile(S, tm_pref, 16)
    return _mm(name, a, b, grid=(1, S // tm, 1),
               a_spec=pl.BlockSpec((tm, K), lambda p, q, k: (q, 0)),
               b_spec=pl.BlockSpec(b.shape, lambda p, q, k: (0, 0)),
               o_spec=pl.BlockSpec((tm, N), lambda p, q, k: (q, 0)),
               out_shape=jax.ShapeDtypeStruct((S, N), out_dtype), contract=contract,
               acc_shape=(tm, N), res=res, comm=comm)


def _mm_tn(name, a, b, tm_pref=1024, tn_pref=1024, tk_pref=2048, comm=None):
    S, M = a.shape
    N = b.shape[1]
    tm, tn, tk = _tile(M, tm_pref, LANES), _tile(N, tn_pref, LANES), _tile(S, tk_pref, 16)
    return _mm(name, a, b, grid=(M // tm, N // tn, S // tk),
               a_spec=pl.BlockSpec((tk, tm), lambda p, q, k: (k, p)),
               b_spec=pl.BlockSpec((tk, tn), lambda p, q, k: (k, q)),
               o_spec=pl.BlockSpec((tm, tn), lambda p, q, k: (p, q)),
               out_shape=jax.ShapeDtypeStruct((M, N), F32), contract=TN, acc_shape=(tm, tn), comm=comm)


def _rmsnorm_fwd(name, x, gain):
    S, D = x.shape
    tm = _tile(S, 512, 16)

    def body(x_ref, g_ref, o_ref):
        xf = x_ref[...]
        r = lax.rsqrt(jnp.mean(xf * xf, axis=-1, keepdims=True) + EPS)
        o_ref[...] = (xf * r * g_ref[...]).astype(o_ref.dtype)

    return pl.pallas_call(
        body, name=name, grid=(S // tm,),
        in_specs=[pl.BlockSpec((tm, D), lambda i: (i, 0)), pl.BlockSpec((1, D), lambda i: (0, 0))],
        out_specs=pl.BlockSpec((tm, D), lambda i: (i, 0)),
        out_shape=jax.ShapeDtypeStruct((S, D), BF16), compiler_params=_cparams("parallel"),
    )(x, gain)


def _rms_bwd_math(dh, xf, gain):
    r = lax.rsqrt(jnp.mean(xf * xf, axis=-1, keepdims=True) + EPS)
    xhat = xf * r
    dxh = dh * gain
    dx = r * (dxh - xhat * jnp.mean(dxh * xhat, axis=-1, keepdims=True))
    return dx, jnp.sum(dh * xhat, axis=0, keepdims=True)


def _rmsnorm_bwd(name, dh, x, gain, dres):
    S, D = x.shape
    tm = _tile(S, 512, 16)

    def body(dh_ref, x_ref, g_ref, dr_ref, dx_ref, dxb_ref, dg_ref):
        dx, dg = _rms_bwd_math(dh_ref[...], x_ref[...], g_ref[...])
        dx = dr_ref[...] + dx
        dx_ref[...] = dx
        dxb_ref[...] = dx.astype(BF16)

        @pl.when(pl.program_id(0) == 0)
        def _():
            dg_ref[...] = dg

        @pl.when(pl.program_id(0) > 0)
        def _():
            dg_ref[...] += dg

    row = pl.BlockSpec((tm, D), lambda i: (i, 0))
    vec = pl.BlockSpec((1, D), lambda i: (0, 0))
    return pl.pallas_call(
        body, name=name, grid=(S // tm,), in_specs=[row, row, vec, row], out_specs=[row, row, vec],
        out_shape=[jax.ShapeDtypeStruct((S, D), F32), jax.ShapeDtypeStruct((S, D), BF16),
                   jax.ShapeDtypeStruct((1, D), F32)],
        compiler_params=_cparams("arbitrary"),
    )(dh, x, gain, dres)


def _loss_fwd_bwd(y, target):
    S, D = y.shape
    tm = _tile(S, 512, 16)

    def body(y_ref, t_ref, l_ref, dy_ref, dyb_ref):
        e = y_ref[...] - t_ref[...]
        dy = e * (1.0 / D)
        dy_ref[...] = dy
        dyb_ref[...] = dy.astype(BF16)
        part = 0.5 * jnp.sum(jnp.mean(e * e, axis=-1, keepdims=True), axis=0, keepdims=True)
        part = jnp.broadcast_to(part, (SUBLANES, LANES))

        @pl.when(pl.program_id(0) == 0)
        def _():
            l_ref[...] = part

        @pl.when(pl.program_id(0) > 0)
        def _():
            l_ref[...] += part

    row = pl.BlockSpec((tm, D), lambda i: (i, 0))
    return pl.pallas_call(
        body, name="loss", grid=(S // tm,), in_specs=[row, row],
        out_specs=[pl.BlockSpec((SUBLANES, LANES), lambda i: (0, 0)), row, row],
        out_shape=[jax.ShapeDtypeStruct((SUBLANES, LANES), F32), jax.ShapeDtypeStruct((S, D), F32),
                   jax.ShapeDtypeStruct((S, D), BF16)],
        compiler_params=_cparams("arbitrary"),
    )(y, target)


def _sigmoid(v):
    return 1.0 / (1.0 + jnp.exp(-v))


MXU_COLS = 256


def _col_chunks(n):
    return [slice(c, min(c + MXU_COLS, n)) for c in range(0, n, MXU_COLS)]


def _shift_rows(v, k, edge8, down):
    tm = v.shape[0]
    sub = lax.broadcasted_iota(jnp.int32, edge8.shape, 0)
    if down:
        r = pltpu.roll(v, k, axis=0)
        head = jnp.where(sub < k, pltpu.roll(edge8, k, axis=0), r[0:8, :])
        return jnp.concatenate([head, r[8:, :]], axis=0)
    r = pltpu.roll(v, tm - k, axis=0)
    tail = jnp.where(sub >= 8 - k, pltpu.roll(edge8, 8 - k, axis=0), r[tm - 8:tm, :])
    return jnp.concatenate([r[:tm - 8, :], tail], axis=0)


def _ffn_up_fwd(name, h2, wup, cw, cb, ag=None):
    S, D = h2.shape
    FF = wup.shape[1] // 2
    CB = _tile(FF, 1408, LANES)
    NJ = FF // CB
    tm = _tile(S, 512, 16)
    nI = S // tm
    n_steps = NJ * nI
    n_ag = len(ag.items) if ag else 0
    schedule = (0, (9 * n_steps) // 20, (7 * n_steps) // 10, n_steps - 1)

    def body(*refs):
        h_ref, wg_ref, wv_ref, cwg_ref, cwv_ref, cbg_ref, cbv_ref = refs[:7]
        u_ref, ab_ref, a_ref = refs[7 + n_ag:10 + n_ag]
        edge_g, edge_v = refs[10 + 2 * n_ag:12 + 2 * n_ag]
        if ag:
            ag.run_at(pl.program_id(0) * nI + pl.program_id(1), schedule, refs[7:7 + n_ag],
                      refs[10 + n_ag:10 + 2 * n_ag], *refs[12 + 2 * n_ag:])

        @pl.when(pl.program_id(1) == 0)
        def _():
            edge_g[...] = jnp.zeros((8, CB), F32)
            edge_v[...] = jnp.zeros((8, CB), F32)

        h = h_ref[...]

        def conv(w_ref, cw_ref, cb_ref, edge, slot, cs):
            u = jnp.dot(h, w_ref[:, cs], preferred_element_type=F32)
            u_ref[slot, :, cs] = u.astype(BF16)
            prev8 = edge[:, cs]
            uc = (cw_ref[0:1, cs] * _shift_rows(u, 2, prev8, True) + cw_ref[1:2, cs] * _shift_rows(u, 1, prev8, True)
                  + cw_ref[2:3, cs] * u + cb_ref[:, cs])
            edge[:, cs] = u[tm - 8:tm, :]
            return uc

        cs = slice(0, CB)
        gc = conv(wg_ref, cwg_ref, cbg_ref, edge_g, 0, cs)
        vc = conv(wv_ref, cwv_ref, cbv_ref, edge_v, 1, cs)
        sig = _sigmoid(gc)
        silu = gc * sig
        a_ref[...] = (silu * vc).astype(BF16)
        ab_ref[0] = (vc * (sig * (1.0 + gc * (1.0 - sig)))).astype(BF16)
        ab_ref[1] = silu.astype(BF16)

    def wspec(off):
        return pl.BlockSpec((D, CB), lambda j, i: (0, j + off))

    def cspec(rows, off):
        return pl.BlockSpec((rows, CB), lambda j, i: (0, j + off))

    pair = pl.BlockSpec((2, tm, CB), lambda j, i: (0, i, j))
    in_specs = [pl.BlockSpec((tm, D), lambda j, i: (i, 0)), wspec(0), wspec(NJ),
                cspec(3, 0), cspec(3, NJ), cspec(1, 0), cspec(1, NJ)]
    out_specs = [pair, pair, pl.BlockSpec((tm, CB), lambda j, i: (i, j))]
    out_shape = [jax.ShapeDtypeStruct((2, S, FF), BF16), jax.ShapeDtypeStruct((2, S, FF), BF16),
                 jax.ShapeDtypeStruct((S, FF), BF16)]
    scratch = [pltpu.VMEM((8, CB), F32), pltpu.VMEM((8, CB), F32)]
    args = (h2, wup, wup, cw, cw, cb, cb)
    if ag:
        in_specs += ag.in_specs()
        out_specs += ag.out_specs()
        out_shape += ag.out_structs()
        scratch += ag.scratch()
        args += tuple(ag.srcs())
    return pl.pallas_call(
        body, name=name, grid=(NJ, nI), in_specs=in_specs, out_specs=out_specs, out_shape=out_shape,
        scratch_shapes=scratch, compiler_params=_cparams("arbitrary", "arbitrary"),
    )(*args)


def _ffn_da_bwd(name, dyb, wd, u3, ab3, cw):
    S, D = dyb.shape
    FF = wd.shape[0]
    CB = _tile(FF, 1408, LANES)
    NJ = FF // CB
    tm = _tile(S, 512, 16)
    nI = S // tm

    def body(dy_ref, wd_ref, u_ref, ab_ref, cwg_ref, cwv_ref, du_ref, dc_ref, edge_g, edge_v):
        i = pl.program_id(1)

        @pl.when(i == 0)
        def _():
            edge_g[...] = jnp.zeros((8, CB), F32)
            edge_v[...] = jnp.zeros((8, CB), F32)
            dc_ref[...] = jnp.zeros(dc_ref.shape, F32)

        dy = dy_ref[...]

        def back(slot, d_uc, edge, cw_ref, cs):
            u = u_ref[slot, :, cs].astype(F32)
            next8 = edge[:, cs]
            dp1 = _shift_rows(d_uc, 1, next8, False)
            dp2 = _shift_rows(d_uc, 2, next8, False)
            du = cw_ref[2:3, cs] * d_uc + cw_ref[1:2, cs] * dp1 + cw_ref[0:1, cs] * dp2
            edge[:, cs] = d_uc[0:8, :]
            du_ref[slot, :, cs] = du.astype(BF16)
            dc_ref[slot, 0:1, cs] += jnp.sum(dp2 * u, axis=0, keepdims=True)
            dc_ref[slot, 1:2, cs] += jnp.sum(dp1 * u, axis=0, keepdims=True)
            dc_ref[slot, 2:3, cs] += jnp.sum(d_uc * u, axis=0, keepdims=True)
            dc_ref[slot, 3:4, cs] += jnp.sum(d_uc, axis=0, keepdims=True)

        for cs in _col_chunks(CB):
            da = lax.dot_general(dy, wd_ref[cs, :], NT, preferred_element_type=F32)
            back(0, da * ab_ref[0, :, cs].astype(F32), edge_g, cwg_ref, cs)
            back(1, da * ab_ref[1, :, cs].astype(F32), edge_v, cwv_ref, cs)

    def rev(i):
        return nI - 1 - i

    pair = pl.BlockSpec((2, tm, CB), lambda j, i: (0, rev(i), j))
    return pl.pallas_call(
        body, name=name, grid=(NJ, nI),
        in_specs=[pl.BlockSpec((tm, D), lambda j, i: (rev(i), 0)),
                  pl.BlockSpec((CB, D), lambda j, i: (j, 0)), pair, pair,
                  pl.BlockSpec((3, CB), lambda j, i: (0, j)), pl.BlockSpec((3, CB), lambda j, i: (0, j + NJ))],
        out_specs=[pair, pl.BlockSpec((2, 8, CB), lambda j, i: (0, 0, j))],
        out_shape=[jax.ShapeDtypeStruct((2, S, FF), BF16), jax.ShapeDtypeStruct((2, 8, FF), F32)],
        scratch_shapes=[pltpu.VMEM((8, CB), F32) for _ in range(2)],
        compiler_params=_cparams("parallel", "arbitrary"),
    )(dyb, wd, u3, ab3, cw, cw)


def _ffn_down_fwd(name, a, wd, res):
    S, FF = a.shape
    D = wd.shape[1]
    tm = _tile(S, 512, 16)
    tk = _tile(FF, 2816, LANES)
    return _mm(name, a, wd, grid=(1, S // tm, FF // tk),
               a_spec=pl.BlockSpec((tm, tk), lambda p, q, k: (q, k)),
               b_spec=pl.BlockSpec((tk, D), lambda p, q, k: (k, 0)),
               o_spec=pl.BlockSpec((tm, D), lambda p, q, k: (q, 0)),
               out_shape=jax.ShapeDtypeStruct((S, D), F32), contract=NN, acc_shape=(tm, D), res=res)


def _ffn_dh_bwd(name, du3, wup, comm=None):
    _, S, FF = du3.shape
    D = wup.shape[0]
    tm = _tile(S, 512, 16)
    tk = _tile(FF, 2816, LANES)
    nh = FF // tk
    return _mm(name, du3, wup, grid=(1, S // tm, 2 * nh),
               a_spec=pl.BlockSpec((None, tm, tk), lambda p, q, k: (k // nh, q, k % nh)),
               b_spec=pl.BlockSpec((D, tk), lambda p, q, k: (0, k)),
               o_spec=pl.BlockSpec((tm, D), lambda p, q, k: (q, 0)),
               out_shape=jax.ShapeDtypeStruct((S, D), F32), contract=NT, acc_shape=(tm, D), comm=comm)


def _ffn_dwup_bwd(name, h2, du3, comm=None):
    S, D = h2.shape
    FF = du3.shape[2]
    NJ = NDEV // 2
    CB = FF // NJ
    tm, tk = _tile(D, 1024, LANES), _tile(S, 2048, 16)
    return _mm(name, h2, du3, grid=(NDEV, D // tm, S // tk),
               a_spec=pl.BlockSpec((tk, tm), lambda p, q, k: (k, q)),
               b_spec=pl.BlockSpec((None, tk, CB), lambda p, q, k: (p // NJ, k, p % NJ)),
               o_spec=pl.BlockSpec((None, tm, CB), lambda p, q, k: (p, q, 0)),
               out_shape=jax.ShapeDtypeStruct((NDEV, D, CB), F32), contract=TN, acc_shape=(tm, CB), comm=comm)


def _t5_onehot():
    i = np.arange(WINDOW)[:, None]
    j = np.arange(2 * WINDOW)[None, :]
    n = np.maximum(WINDOW + i - j, 0)
    max_exact = N_BUCKETS // 2
    nf = np.maximum(n, 1).astype(np.float32)
    large = max_exact + (np.log(nf / max_exact) / np.log(MAX_DISTANCE / max_exact)
                         * (N_BUCKETS - max_exact)).astype(np.int32)
    large = np.minimum(large, N_BUCKETS - 1)
    bucket = np.where(n < max_exact, n, large).astype(np.int32).reshape(-1)
    return (np.arange(N_BUCKETS)[:, None] == bucket[None, :]).astype(np.float32)


def _bias_band(rel_bias, onehot):
    H = rel_bias.shape[0]
    n = onehot.shape[1]

    def body(r_ref, oh_ref, o_ref):
        o_ref[...] = jnp.dot(r_ref[...], oh_ref[...], preferred_element_type=F32,
                             precision=lax.Precision.HIGHEST)

    return pl.pallas_call(body, name="bias_band", out_shape=jax.ShapeDtypeStruct((H, n), F32),
                          compiler_params=pltpu.CompilerParams(vmem_limit_bytes=VMEM_LIMIT))(rel_bias, onehot)


def _bias_band_bwd(dband, onehot):
    H = dband.shape[0]

    def body(d_ref, oh_ref, o_ref):
        o_ref[...] = lax.dot_general(d_ref[...], oh_ref[...], NT, preferred_element_type=F32,
                                     precision=lax.Precision.HIGHEST)

    return pl.pallas_call(body, name="bias_band_bwd", out_shape=jax.ShapeDtypeStruct((H, N_BUCKETS), F32),
                          compiler_params=pltpu.CompilerParams(vmem_limit_bytes=VMEM_LIMIT))(dband, onehot)


def _band_valid(n):
    i = lax.broadcasted_iota(jnp.int32, (WINDOW, 2 * WINDOW), 0)
    j = lax.broadcasted_iota(jnp.int32, (WINDOW, 2 * WINDOW), 1)
    return (j > i) & (j <= i + WINDOW) & ((n > 0) | (j >= WINDOW))


def _head_norm(v, gain):
    r = lax.rsqrt(jnp.mean(v * v, axis=-1, keepdims=True) + EPS)
    vhat = v * r
    return r, vhat, vhat * gain


HEAD_SUM_COLS = 256


def _head_sum_matrices():
    blk = np.arange(HEAD_SUM_COLS) // HEAD_DIM
    bd = (blk[:, None] == blk[None, :]).astype(np.float32)
    return jnp.asarray(bd, BF16), jnp.ones((2 * WINDOW, LANES), BF16)


def _head_sums(v, bd_ref):
    hi = v.astype(BF16)
    lo = (v - hi.astype(F32)).astype(BF16)
    bd = bd_ref[...]
    parts = []
    for c in range(0, v.shape[1], HEAD_SUM_COLS):
        cs = slice(c, c + HEAD_SUM_COLS)
        parts.append(jnp.dot(hi[:, cs], bd, preferred_element_type=F32) + jnp.dot(lo[:, cs], bd, preferred_element_type=F32))
    return jnp.concatenate(parts, axis=1)


def _stack_heads(t, kh):
    return jnp.concatenate([t[:, (kh * GQA_GROUP + g) * HEAD_DIM:(kh * GQA_GROUP + g + 1) * HEAD_DIM]
                            for g in range(GQA_GROUP)], axis=0)


def _unstack_heads(t8):
    return jnp.concatenate([t8[g * WINDOW:(g + 1) * WINDOW, :] for g in range(GQA_GROUP)], axis=1)


def _band_scores(qn, kn, b_ref, kh, valid, scale):
    s = lax.dot_general(qn, kn, NT, preferred_element_type=F32) * scale
    s = s.reshape(GQA_GROUP, WINDOW, 2 * WINDOW) + b_ref[kh * GQA_GROUP:(kh + 1) * GQA_GROUP]
    s = jnp.where(valid[None], s, NEG_INF)
    return s.reshape(GQA_GROUP * WINDOW, 2 * WINDOW)


def _sink_rows(s_ref, kh):
    return jnp.concatenate([jnp.broadcast_to(s_ref[kh * GQA_GROUP + g:kh * GQA_GROUP + g + 1, :], (WINDOW, 1))
                            for g in range(GQA_GROUP)], axis=0)


def _attn_fwd(name, qkv, bias, qg, kg, sinks, ag=None):
    S, QW = qkv.shape
    H = bias.shape[0]
    D = H * HEAD_DIM
    KV = H // GQA_GROUP
    kvw = QW - D
    kvb = D // kvw
    nb = S // WINDOW
    scale = HEAD_DIM ** -0.5
    n_ag = len(ag.items) if ag else 0
    schedule = (0, (9 * nb) // 20, (7 * nb) // 10, nb - 1)
    N_IN = 9
    bd, ones = _head_sum_matrices()

    def body(*refs):
        q_ref, kc_ref, kp_ref, b_ref, qg_ref, kg_ref, s_ref, bd_ref, ones_ref = refs[:N_IN]
        o_ref, l_ref = refs[N_IN + n_ag:N_IN + 2 + n_ag]
        n = pl.program_id(0)
        if ag:
            ag.run_at(n, schedule, refs[N_IN:N_IN + n_ag], refs[N_IN + 2 + n_ag:N_IN + 2 + 2 * n_ag],
                      *refs[N_IN + 2 + 2 * n_ag:])
        valid = _band_valid(n)
        q = q_ref[...]
        kvc = kc_ref[...]
        kvp = kp_ref[...]
        rq = lax.rsqrt(_head_sums(q * q, bd_ref) * (1.0 / HEAD_DIM) + EPS)
        qn_all = (q * rq * qg_ref[...]).astype(BF16)
        lane = lax.broadcasted_iota(jnp.int32, (WINDOW, H), 1)
        lse_all = jnp.zeros((WINDOW, H), F32)
        scores, vbs = [], []
        for kh in range(KV):
            ks = slice(kh * HEAD_DIM, (kh + 1) * HEAD_DIM)
            vs = slice((KV + kh) * HEAD_DIM, (KV + kh + 1) * HEAD_DIM)
            kb = jnp.concatenate([kvp[:, ks], kvc[:, ks]], axis=0)
            vb = jnp.concatenate([kvp[:, vs], kvc[:, vs]], axis=0).astype(BF16)
            kn = _head_norm(kb, kg_ref[...])[2].astype(BF16)
            vbs.append(vb)
            scores.append(_band_scores(_stack_heads(qn_all, kh), kn, b_ref, kh, valid, scale))
        sinks_ = [_sink_rows(s_ref, kh) for kh in range(KV)]
        ms = [jnp.maximum(jnp.max(s, axis=-1, keepdims=True), sk) for s, sk in zip(scores, sinks_)]
        ps = [jnp.exp(s - m).astype(BF16) for s, m in zip(scores, ms)]
        dens = [jnp.dot(p, ones_ref[...], preferred_element_type=F32)[:, :HEAD_DIM] + jnp.exp(sk - m)
                for p, sk, m in zip(ps, sinks_, ms)]
        outs = [_unstack_heads(jnp.dot(p, vb, preferred_element_type=F32) * (1.0 / den))
                for p, vb, den in zip(ps, vbs, dens)]
        for kh in range(KV):
            lse = ms[kh] + jnp.log(dens[kh][:, 0:1])
            for g in range(GQA_GROUP):
                lse_all = jnp.where(lane == kh * GQA_GROUP + g, lse[g * WINDOW:(g + 1) * WINDOW, :], lse_all)
        o_ref[...] = jnp.concatenate(outs, axis=1).astype(BF16)
        l_ref[...] = lse_all

    const2 = lambda shape: pl.BlockSpec(shape, lambda n: (0, 0))
    in_specs = [pl.BlockSpec((WINDOW, D), lambda n: (n, 0)),
                pl.BlockSpec((WINDOW, kvw), lambda n: (n, kvb)),
                pl.BlockSpec((WINDOW, kvw), lambda n: (jnp.maximum(n - 1, 0), kvb)),
                pl.BlockSpec(bias.shape, lambda n: (0, 0, 0)),
                const2((1, D)), const2((1, HEAD_DIM)), const2((H, 1)), const2(bd.shape), const2(ones.shape)]
    out_specs = [pl.BlockSpec((WINDOW, D), lambda n: (n, 0)), pl.BlockSpec((WINDOW, H), lambda n: (n, 0))]
    out_shape = [jax.ShapeDtypeStruct((S, D), BF16), jax.ShapeDtypeStruct((S, H), F32)]
    args = (qkv, qkv, qkv, bias, jnp.tile(qg, (1, H)), kg, sinks, bd, ones)
    scratch = []
    if ag:
        in_specs += ag.in_specs()
        out_specs += ag.out_specs()
        out_shape += ag.out_structs()
        scratch += ag.scratch()
        args += tuple(ag.srcs())
    return pl.pallas_call(
        body, name=name, grid=(nb,), in_specs=in_specs, out_specs=out_specs, out_shape=out_shape,
        scratch_shapes=scratch, compiler_params=_cparams("arbitrary" if ag else "parallel"),
    )(*args)


def _attn_bwd(name, qkv, do, lse, bias, qg, kg, sinks, comm=None):
    S, QW = qkv.shape
    H = bias.shape[0]
    D = H * HEAD_DIM
    KV = H // GQA_GROUP
    kvw = QW - D
    kvb = D // kvw
    nb = S // WINDOW
    scale = HEAD_DIM ** -0.5
    n_ci = len(comm.srcs) if comm else 0
    n_co = len(comm.out_structs) if comm else 0
    N_IN = 10
    GROUPS_TOGETHER = 2
    bd, _ = _head_sum_matrices()

    def body(*refs):
        q_ref, kc_ref, kp_ref, do_ref, l_ref, b_ref, qg_ref, kg_ref, s_ref, bd_ref = refs[:N_IN]
        dq_ref, dkv_ref, db_ref, ds_ref, dqg_ref, dkg_ref = refs[N_IN + n_ci:N_IN + 6 + n_ci]
        carry = refs[N_IN + 6 + n_ci + n_co]
        n = pl.program_id(0)
        if comm:
            comm.run(refs[N_IN:N_IN + n_ci], refs[N_IN + 6 + n_ci:N_IN + 6 + n_ci + n_co], refs[-2], refs[-1],
                     n == 0, n == nb)

        @pl.when(n == 0)
        def _():
            db_ref[...] = jnp.zeros(db_ref.shape, F32)
            ds_ref[...] = jnp.zeros(ds_ref.shape, F32)
            dqg_ref[...] = jnp.zeros(dqg_ref.shape, F32)
            dkg_ref[...] = jnp.zeros(dkg_ref.shape, F32)
            carry[...] = jnp.zeros(carry.shape, F32)

        @pl.when(n == nb)
        def _():
            dkv_ref[...] = carry[...].astype(BF16)

        @pl.when(n < nb)
        def _():
            valid = _band_valid(n)
            q = q_ref[...]
            kvc = kc_ref[...]
            kvp = kp_ref[...]
            do_all = do_ref[...]
            lse = l_ref[...]
            qgain = qg_ref[...]
            kgain = kg_ref[...]
            rq = lax.rsqrt(_head_sums(q * q, bd_ref) * (1.0 / HEAD_DIM) + EPS)
            qhat = q * rq
            qn_all = (qhat * qgain).astype(BF16)
            def run_groups(groups):
                idx = range(len(groups))
                heads = [slice(kh * GQA_GROUP, (kh + 1) * GQA_GROUP) for kh in groups]
                kbs = [jnp.concatenate([kvp[:, kh * HEAD_DIM:(kh + 1) * HEAD_DIM],
                                        kvc[:, kh * HEAD_DIM:(kh + 1) * HEAD_DIM]], axis=0) for kh in groups]
                vbs = [jnp.concatenate([kvp[:, (KV + kh) * HEAD_DIM:(KV + kh + 1) * HEAD_DIM],
                                        kvc[:, (KV + kh) * HEAD_DIM:(KV + kh + 1) * HEAD_DIM]], axis=0).astype(BF16)
                       for kh in groups]
                knorm = [_head_norm(kb, kgain) for kb in kbs]
                kns = [t[2].astype(BF16) for t in knorm]
                qns = [_stack_heads(qn_all, kh) for kh in groups]
                ss = [_band_scores(qns[i], kns[i], b_ref, groups[i], valid, scale) for i in idx]
                lse8 = [jnp.concatenate([lse[:, kh * GQA_GROUP + g:kh * GQA_GROUP + g + 1]
                                         for g in range(GQA_GROUP)], axis=0) for kh in groups]
                ps = [jnp.exp(s - l) for s, l in zip(ss, lse8)]
                do8 = [_stack_heads(do_all, kh) for kh in groups]
                dps = [lax.dot_general(d, vb, NT, preferred_element_type=F32) for d, vb in zip(do8, vbs)]
                deltas = [jnp.sum(p * dp, axis=-1, keepdims=True) for p, dp in zip(ps, dps)]
                dss = [p * (dp - dl) for p, dp, dl in zip(ps, dps, deltas)]
                for i in idx:
                    db_ref[heads[i]] += dss[i].reshape(GQA_GROUP, WINDOW, 2 * WINDOW)
                    psink = jnp.exp(_sink_rows(s_ref, groups[i]) - lse8[i])
                    ds_ref[heads[i], :] += -jnp.sum((psink * deltas[i]).reshape(GQA_GROUP, WINDOW, 1), axis=1)
                dsbs = [(ds * scale).astype(BF16) for ds in dss]
                dqn_p = [_unstack_heads(jnp.dot(dsb, kn, preferred_element_type=F32)) for dsb, kn in zip(dsbs, kns)]
                dkns = [lax.dot_general(dsb, qn, TN, preferred_element_type=F32) for dsb, qn in zip(dsbs, qns)]
                dv_p = [lax.dot_general(p.astype(BF16), d, TN, preferred_element_type=F32) for p, d in zip(ps, do8)]
                dkg_p = jnp.zeros((1, HEAD_DIM), F32)
                dk_p = []
                for i in idx:
                    rk, khat, _ = knorm[i]
                    dkg_p = dkg_p + jnp.sum(dkns[i] * khat, axis=0, keepdims=True)
                    dkh = dkns[i] * kgain
                    dk_p.append(rk * (dkh - khat * jnp.mean(dkh * khat, axis=-1, keepdims=True)))
                return dqn_p, dk_p, dv_p, dkg_p

            dqn_parts, dk_parts, dv_parts = [], [], []
            dkg = jnp.zeros((1, HEAD_DIM), F32)
            for g0 in range(0, KV, GROUPS_TOGETHER):
                dqn_p, dk_p, dv_p, dkg_p = run_groups(list(range(g0, min(g0 + GROUPS_TOGETHER, KV))))
                dqn_parts += dqn_p
                dk_parts += dk_p
                dv_parts += dv_p
                dkg = dkg + dkg_p
            dqn = jnp.concatenate(dqn_parts, axis=1)
            dqh = dqn * qgain
            dq = rq * (dqh - qhat * (_head_sums(dqh * qhat, bd_ref) * (1.0 / HEAD_DIM)))
            dq_ref[...] = dq.astype(BF16)
            dqg_ref[...] += jnp.sum(dqn * qhat, axis=0, keepdims=True)
            dkg_ref[...] += dkg
            dkv = jnp.concatenate(dk_parts + dv_parts, axis=1)
            dkv_ref[...] = (carry[...] + dkv[0:WINDOW, :]).astype(BF16)
            carry[...] = dkv[WINDOW:2 * WINDOW, :]

    cur = lambda n: jnp.minimum(n, nb - 1)
    const2 = lambda shape: pl.BlockSpec(shape, lambda n: (0, 0))
    in_specs = [pl.BlockSpec((WINDOW, D), lambda n: (cur(n), 0)),
                pl.BlockSpec((WINDOW, kvw), lambda n: (cur(n), kvb)),
                pl.BlockSpec((WINDOW, kvw), lambda n: (jnp.maximum(cur(n) - 1, 0), kvb)),
                pl.BlockSpec((WINDOW, D), lambda n: (cur(n), 0)),
                pl.BlockSpec((WINDOW, H), lambda n: (cur(n), 0)),
                pl.BlockSpec(bias.shape, lambda n: (0, 0, 0)),
                const2((1, D)), const2((1, HEAD_DIM)), const2((H, 1)), const2(bd.shape)]
    out_specs = [pl.BlockSpec((WINDOW, D), lambda n: (cur(n), 0)),
                 pl.BlockSpec((WINDOW, kvw), lambda n: (jnp.maximum(n - 1, 0), 0)),
                 pl.BlockSpec(bias.shape, lambda n: (0, 0, 0)),
                 const2((H, 1)), const2((1, D)), const2((1, HEAD_DIM))]
    out_shape = [jax.ShapeDtypeStruct((S, D), BF16), jax.ShapeDtypeStruct((S, kvw), BF16),
                 jax.ShapeDtypeStruct(bias.shape, F32), jax.ShapeDtypeStruct((H, 1), F32),
                 jax.ShapeDtypeStruct((1, D), F32), jax.ShapeDtypeStruct((1, HEAD_DIM), F32)]
    scratch = [pltpu.VMEM((WINDOW, kvw), F32)]
    args = (qkv, qkv, qkv, do, lse, bias, jnp.tile(qg, (1, H)), kg, sinks, bd)
    if comm:
        in_specs += comm.in_specs()
        out_specs += comm.out_specs()
        out_shape += comm.out_structs
        scratch += comm.scratch()
        args += tuple(comm.srcs)
    return pl.pallas_call(
        body, name=name, grid=(nb + 1,), in_specs=in_specs, out_specs=out_specs, out_shape=out_shape,
        scratch_shapes=scratch, compiler_params=_cparams("arbitrary"),
    )(*args)


def _window_sums(src, bufs, lo, n_rows, step_sign, col_groups):
    out = []
    for g, cols in enumerate(col_groups):
        prev = src
        for level in range(g + 1):
            k = step_sign * (1 << level)
            cur = bufs[level]
            cur[pl.ds(lo, n_rows), cols] = prev[pl.ds(lo, n_rows), cols] + prev[pl.ds(lo + k, n_rows), cols]
            prev = cur
        out.append(prev)
    return out


def _pool_fwd(name, x, gain, wp, scale):
    S, D = x.shape
    G, C = wp.shape[0], wp.shape[1]
    tm = _tile(S, 256, POOL_HALO)
    hb = tm // POOL_HALO
    HL = POOL_HALO
    groups = [slice(g * C, (g + 1) * C) for g in range(G)]

    def body(x_ref, xh_ref, g_ref, w_ref, sc_ref, o_ref, d_ref, ext, p2, p4, p8, p16):
        i = pl.program_id(0)
        gain_v = g_ref[...]
        xt = x_ref[...]
        h = _head_norm(xt, gain_v)[2]
        hh = _head_norm(xh_ref[...], gain_v)[2]
        ext[pl.ds(0, HL), :] = jnp.where(i > 0, hh, 0.0)
        ext[pl.ds(HL, tm), :] = h
        bufs = (p2, p4, p8, p16)
        for b in bufs:
            b[pl.ds(0, 8), :] = jnp.zeros((8, D), F32)
        sums = _window_sums(ext, bufs, 8, tm + HL - 8, -1, groups)
        t = i * tm + lax.broadcasted_iota(jnp.int32, (tm, 1), 0)
        for g, cols in enumerate(groups):
            cnt = jnp.minimum(t + 1, POOL_WINDOWS[g]).astype(F32)
            d = sums[g][pl.ds(HL, tm), cols] / cnt - h[:, cols]
            db = d.astype(BF16)
            d_ref[:, cols] = db
            y = jnp.dot(db, w_ref[g], preferred_element_type=F32)
            o_ref[:, cols] = xt[:, cols] + y * sc_ref[:, cols]

    row = pl.BlockSpec((tm, D), lambda i: (i, 0))
    vec = pl.BlockSpec((1, D), lambda i: (0, 0))
    return pl.pallas_call(
        body, name=name, grid=(S // tm,),
        in_specs=[row, pl.BlockSpec((HL, D), lambda i: (jnp.maximum(i * hb - 1, 0), 0)), vec,
                  pl.BlockSpec(wp.shape, lambda i: (0, 0, 0)), vec],
        out_specs=[row, row],
        out_shape=[jax.ShapeDtypeStruct((S, D), F32), jax.ShapeDtypeStruct((S, D), BF16)],
        scratch_shapes=[pltpu.VMEM((tm + HL, D), F32) for _ in range(5)],
        compiler_params=_cparams("parallel"),
    )(x, x, gain, wp, scale)


def _pool_bwd(name, dx1, x, gain, wp, scale, dsave):
    S, D = x.shape
    G, C = wp.shape[0], wp.shape[1]
    tm = _tile(S, 256, POOL_HALO)
    hb = tm // POOL_HALO
    HL = POOL_HALO
    nI = S // tm
    groups = [slice(g * C, (g + 1) * C) for g in range(G)]

    def body(dx_ref, dxh_ref, x_ref, g_ref, w_ref, sc_ref, ds_ref, o_ref, ob_ref, dw_ref, dsc_ref, dg_ref,
             ext, p2, p4, p8, p16):
        i = pl.program_id(0)

        @pl.when(i == 0)
        def _():
            dw_ref[...] = jnp.zeros(dw_ref.shape, F32)
            dsc_ref[...] = jnp.zeros(dsc_ref.shape, F32)
            dg_ref[...] = jnp.zeros(dg_ref.shape, F32)

        dx1t = dx_ref[...]
        sc = sc_ref[...]
        dys = (dx1t * sc).astype(BF16)
        dys_h = (dxh_ref[...] * sc).astype(BF16)
        t = i * tm + lax.broadcasted_iota(jnp.int32, (tm, 1), 0)
        th = (i + 1) * tm + lax.broadcasted_iota(jnp.int32, (HL, 1), 0)
        dds = []
        for g, cols in enumerate(groups):
            dsv = ds_ref[:, cols]
            y = jnp.dot(dsv, w_ref[g], preferred_element_type=F32)
            dsc_ref[:, cols] += jnp.sum(dx1t[:, cols] * y, axis=0, keepdims=True)
            dw_ref[g] += lax.dot_general(dsv, dys[:, cols], TN, preferred_element_type=F32)
            dd = lax.dot_general(dys[:, cols], w_ref[g], NT, preferred_element_type=F32)
            dd_h = lax.dot_general(dys_h[:, cols], w_ref[g], NT, preferred_element_type=F32)
            dds.append(dd)
            w = POOL_WINDOWS[g]
            ext[pl.ds(0, tm), cols] = dd / jnp.minimum(t + 1, w).astype(F32)
            e_h = dd_h / jnp.minimum(th + 1, w).astype(F32)
            ext[pl.ds(tm, HL), cols] = jnp.where(i < nI - 1, e_h, 0.0)
        bufs = (p2, p4, p8, p16)
        for b in bufs:
            b[pl.ds(tm + HL - 8, 8), :] = jnp.zeros((8, D), F32)
        sums = _window_sums(ext, bufs, 0, tm + HL - 8, 1, groups)
        dh = jnp.concatenate([sums[g][pl.ds(0, tm), cols] - dds[g] for g, cols in enumerate(groups)], axis=1)
        dx, dg = _rms_bwd_math(dh, x_ref[...], g_ref[...])
        dx = dx1t + dx
        o_ref[...] = dx
        ob_ref[...] = dx.astype(BF16)
        dg_ref[...] += dg

    row = pl.BlockSpec((tm, D), lambda i: (i, 0))
    vec = pl.BlockSpec((1, D), lambda i: (0, 0))
    last_h = S // HL - 1
    return pl.pallas_call(
        body, name=name, grid=(nI,),
        in_specs=[row, pl.BlockSpec((HL, D), lambda i: (jnp.minimum((i + 1) * hb, last_h), 0)), row, vec,
                  pl.BlockSpec(wp.shape, lambda i: (0, 0, 0)), vec, row],
        out_specs=[row, row, pl.BlockSpec(wp.shape, lambda i: (0, 0, 0)), vec, vec],
        out_shape=[jax.ShapeDtypeStruct((S, D), F32), jax.ShapeDtypeStruct((S, D), BF16),
                   jax.ShapeDtypeStruct(wp.shape, F32),
                   jax.ShapeDtypeStruct((1, D), F32), jax.ShapeDtypeStruct((1, D), F32)],
        scratch_shapes=[pltpu.VMEM((tm + HL, D), F32) for _ in range(5)],
        compiler_params=_cparams("arbitrary"),
    )(dx1, dx1, x, gain, wp, scale, dsave)


HBM_SPEC = pl.BlockSpec(memory_space=pltpu.HBM)


def _coords():
    return lax.axis_index("x"), lax.axis_index("y"), lax.axis_index("c")


class _AgItem:
    def __init__(self, src, out_struct, slot, half):
        self.src, self.out_struct, self.slot, self.half = src, out_struct, slot, half


def _rows_item(src, n_rows_total):
    r, ncol = src.shape
    return _AgItem(src, jax.ShapeDtypeStruct((n_rows_total, ncol), src.dtype),
                   lambda out, d: out.at[pl.ds(pl.multiple_of(d * r, 16), r), :],
                   lambda ref, h: ref.at[pl.ds(h * (r // 2), r // 2), :])


def _cols_item(src, n_cols_total):
    nrow, cb = src.shape
    return _AgItem(src, jax.ShapeDtypeStruct((nrow, n_cols_total), src.dtype),
                   lambda out, d: out.at[:, pl.ds(pl.multiple_of(d * cb, LANES), cb)],
                   lambda ref, h: ref.at[pl.ds(h * (nrow // 2), nrow // 2), :])


def _lead_item(src):
    return _AgItem(src, jax.ShapeDtypeStruct((NDEV,) + src.shape, src.dtype),
                   lambda out, d: out.at[d],
                   lambda ref, h: ref.at[pl.ds(h * (src.shape[0] // 2), src.shape[0] // 2)])


def _pool_item(src, c_total):
    g, pc, c = src.shape
    return _AgItem(src, jax.ShapeDtypeStruct((g, c_total, c), src.dtype),
                   lambda out, d: out.at[:, pl.ds(pl.multiple_of(d * pc, 16), pc), :],
                   lambda ref, h: ref.at[pl.ds(h * (g // 2), g // 2)])


class _AllGather:
    N_PHASES = 4

    def __init__(self, items):
        self.items = list(items)

    def srcs(self):
        return [it.src for it in self.items]

    def out_structs(self):
        return [it.out_struct for it in self.items]

    def in_specs(self):
        return [HBM_SPEC] * len(self.items)

    def out_specs(self):
        return [HBM_SPEC] * len(self.items)

    def scratch(self):
        n = len(self.items)
        return [pltpu.SemaphoreType.DMA((8 * n,)), pltpu.SemaphoreType.DMA((8 * n,)), pltpu.SemaphoreType.DMA((n,))]

    def phase(self, ph, ins, outs, send_sems, recv_sems, local_sems):
        x, y, c = _coords()
        me, xn = 4 * x + 2 * y + c, 4 * (1 - x) + 2 * y + c
        yn, dg = 4 * x + 2 * (1 - y) + c, 4 * (1 - x) + 2 * (1 - y) + c
        XN, YN, SB = (1 - x, y, c), (x, 1 - y, c), (x, y, 1 - c)
        sib = lambda blk: blk + 1 - 2 * c
        for o, it in enumerate(self.items):
            slot = lambda d, it=it, o=o: it.slot(outs[o], d)
            half = it.half
            table = [
                (ins[o], slot(me), XN, slot(xn)),
                (ins[o], slot(me), YN, slot(yn)),
                (half(slot(xn), 0), half(slot(xn), 0), YN, half(slot(dg), 0)),
                (half(slot(yn), 1), half(slot(yn), 1), XN, half(slot(dg), 1)),
                (ins[o], slot(me), SB, slot(sib(me))),
                (slot(xn), slot(xn), SB, slot(sib(xn))),
                (slot(yn), slot(yn), SB, slot(sib(yn))),
                (slot(dg), slot(dg), SB, slot(sib(dg))),
            ]

            def send(k, table=table, o=o):
                src, dst, peer, _ = table[k]
                return pltpu.make_async_remote_copy(src_ref=src, dst_ref=dst, send_sem=send_sems.at[8 * o + k],
                                                    recv_sem=recv_sems.at[8 * o + k], device_id=peer, device_id_type=MESH)

            def arrived(k, table=table, o=o):
                land = table[k][3]
                pltpu.make_async_remote_copy(src_ref=land, dst_ref=land, send_sem=send_sems.at[8 * o + k],
                                             recv_sem=recv_sems.at[8 * o + k], device_id=table[k][2],
                                             device_id_type=MESH).wait_recv()

            local = pltpu.make_async_copy(ins[o], slot(me), local_sems.at[o])
            if ph == 0:
                local.start()
                for k in (0, 1, 4):
                    send(k).start()
            elif ph == 1:
                arrived(0)
                send(2).start()
                send(5).start()
                arrived(1)
                send(3).start()
                send(6).start()
            elif ph == 2:
                arrived(2)
                arrived(3)
                send(7).start()
            else:
                for k in (4, 5, 6, 7):
                    arrived(k)
                for k in range(8):
                    send(k).wait_send()
                local.wait()

    def run_at(self, step, schedule, ins, outs, send_sems, recv_sems, local_sems):
        for ph in range(self.N_PHASES):
            @pl.when(step == schedule[ph])
            def _(ph=ph):
                self.phase(ph, ins, outs, send_sems, recv_sems, local_sems)


def _allgather_now(name, ag):
    n = len(ag.items)

    def body(*refs):
        for ph in range(ag.N_PHASES):
            ag.phase(ph, refs[:n], refs[n:2 * n], *refs[2 * n:])

    return pl.pallas_call(body, name=name, in_specs=ag.in_specs(), out_specs=ag.out_specs(),
                          out_shape=ag.out_structs(), scratch_shapes=ag.scratch())(*ag.srcs())


def _allgather_small(name, block):
    m_per, ncol = block.shape

    def body(x_ref, out_ref, send_sems, recv_sems, local_sem):
        x, y, c = _coords()
        me, sibling = (x, y, c), (x, y, 1 - c)
        chips = [(1 - x, y), (x, 1 - y), (1 - x, 1 - y)]

        def rows(px, py, pc):
            return out_ref.at[pl.ds((4 * px + 2 * py + pc) * m_per, m_per), :]

        def copy(k, block_of, to, src=None):
            return pltpu.make_async_remote_copy(
                src_ref=rows(*block_of) if src is None else src, dst_ref=rows(*block_of),
                send_sem=send_sems.at[k], recv_sem=recv_sems.at[k], device_id=to, device_id_type=MESH)

        mine = pltpu.make_async_copy(x_ref, rows(*me), local_sem)
        mine.start()
        first = [copy(0, me, sibling, src=x_ref)]
        first += [copy(1 + j, me, (*chip, c), src=x_ref) for j, chip in enumerate(chips)]
        for cp in first:
            cp.start()
        passed = [copy(4 + j, (*chip, c), sibling) for j, chip in enumerate(chips)]
        for j, chip in enumerate(chips):
            copy(1 + j, (*chip, c), me).wait_recv()
            passed[j].start()
        copy(0, sibling, me).wait_recv()
        for j, chip in enumerate(chips):
            copy(4 + j, (*chip, 1 - c), me).wait_recv()
        for cp in first + passed:
            cp.wait_send()
        mine.wait()

    return pl.pallas_call(
        body, name=name, out_shape=jax.ShapeDtypeStruct((NDEV * m_per, ncol), block.dtype),
        in_specs=[pl.BlockSpec(memory_space=pltpu.VMEM)], out_specs=pl.BlockSpec(memory_space=pltpu.VMEM),
        scratch_shapes=[pltpu.SemaphoreType.DMA((7,)), pltpu.SemaphoreType.DMA((7,)), pltpu.SemaphoreType.DMA],
        compiler_params=pltpu.CompilerParams(vmem_limit_bytes=VMEM_LIMIT),
    )(block)


def _exchange_now(name, ex):
    n_in, n_out = len(ex.srcs), len(ex.out_structs)

    def body(*refs):
        ex.run(refs[:n_in], refs[n_in:n_in + n_out], refs[n_in + n_out], refs[n_in + n_out + 1], True, True)

    return pl.pallas_call(body, name=name, in_specs=ex.in_specs(), out_specs=ex.out_specs(),
                          out_shape=ex.out_structs, scratch_shapes=ex.scratch())(*ex.srcs)


def _rs_stage_c(gs):
    def plan(x, y, c, ins, outs):
        sib = (x, y, 1 - c)
        return [(g.at[q, 1 - c], r.at[q], sib) for g, r in zip(ins, outs) for q in range(4)]

    outs = [jax.ShapeDtypeStruct((4,) + g.shape[2:], g.dtype) for g in gs]
    return _Exchange(gs, outs, plan, 4 * len(gs))


def _rs_stage_ici(sends, first):
    def plan(x, y, c, ins, outs):
        XN, YN = (1 - x, y, c), (x, 1 - y, c)
        peers = (YN, XN) if first else (XN, YN)
        return [(s.at[h], r.at[h], peers[h]) for s, r in zip(ins, outs) for h in range(2)]

    outs = [jax.ShapeDtypeStruct(s.shape, s.dtype) for s in sends]
    return _Exchange(sends, outs, plan, 2 * len(sends))


def _coord_vec():
    x, y, c = _coords()
    return jnp.stack([x, y, c]).astype(jnp.int32)


def _rs_add1(name, g, r1, coords):
    R, L = g.shape[3], g.shape[4]
    tr = _tile(R, 512, 16)

    def qk(h, idx, cr):
        return jnp.where(h == 0, 2 * idx + cr[1], 2 * cr[0] + idx)

    def qs(h, idx, cr):
        return jnp.where(h == 0, 2 * idx + 1 - cr[1], 2 * (1 - cr[0]) + idx)

    def body(cr, gk, rk, gsd, rsd, keep, send):
        keep[...] = gk[...] + rk[...]
        send[...] = (gsd[...] + rsd[...]).astype(BF16)

    gspec = lambda qf: pl.BlockSpec((None, None, None, tr, L), lambda h, idx, r, cr: (qf(h, idx, cr), cr[2], h, r, 0))
    rspec = lambda qf: pl.BlockSpec((None, None, tr, L), lambda h, idx, r, cr: (qf(h, idx, cr), h, r, 0))
    ospec = pl.BlockSpec((None, None, tr, L), lambda h, idx, r, cr: (h, idx, r, 0))
    return pl.pallas_call(
        body, name=name,
        grid_spec=pltpu.PrefetchScalarGridSpec(
            num_scalar_prefetch=1, grid=(2, 2, R // tr),
            in_specs=[gspec(qk), rspec(qk), gspec(qs), rspec(qs)], out_specs=[ospec, ospec]),
        out_shape=[jax.ShapeDtypeStruct((2, 2, R, L), F32), jax.ShapeDtypeStruct((2, 2, R, L), BF16)],
        compiler_params=_cparams("parallel", "parallel", "parallel"),
    )(coords, g, r1, g, r1)


def _rs_add2(name, keep2, recv2, coords):
    R, L = keep2.shape[2], keep2.shape[3]
    tr = _tile(R, 512, 16)

    def mine(h, cr):
        return jnp.where(h == 0, cr[0], cr[1])

    def body(cr, kk, rk, ks, rs, keep, send):
        keep[...] = kk[...] + rk[...].astype(F32)
        send[...] = (ks[...] + rs[...].astype(F32)).astype(BF16)

    sel = lambda f: pl.BlockSpec((None, None, tr, L), lambda h, r, cr: (h, f(h, cr), r, 0))
    ospec = pl.BlockSpec((None, tr, L), lambda h, r, cr: (h, r, 0))
    other = lambda h, cr: 1 - mine(h, cr)
    return pl.pallas_call(
        body, name=name,
        grid_spec=pltpu.PrefetchScalarGridSpec(
            num_scalar_prefetch=1, grid=(2, R // tr),
            in_specs=[sel(mine), sel(mine), sel(other), sel(other)], out_specs=[ospec, ospec]),
        out_shape=[jax.ShapeDtypeStruct((2, R, L), F32), jax.ShapeDtypeStruct((2, R, L), BF16)],
        compiler_params=_cparams("parallel", "parallel"),
    )(coords, keep2, recv2, keep2, recv2)


def _rs_add3(name, keep3, recv3):
    R, L = keep3.shape[1], keep3.shape[2]
    tr = _tile(R, 512, 16)

    def body(k, r, o):
        o[...] = k[...] + r[...].astype(F32)

    spec = pl.BlockSpec((None, tr, L), lambda h, r: (h, r, 0))
    return pl.pallas_call(body, name=name, grid=(2, R // tr), in_specs=[spec, spec], out_specs=spec,
                          out_shape=jax.ShapeDtypeStruct((2, R, L), F32),
                          compiler_params=_cparams("parallel", "parallel"))(keep3, recv3)


class _ReduceScatter:
    N_STAGES = 3

    def __init__(self, tag, gs, coords):
        self.tag, self.coords, self.stage, self.result = tag, coords, 0, None
        self.full = []
        for g in gs:
            per = int(np.prod(g.shape[1:]))
            L = g.shape[-1]
            self.full.append(g.reshape(4, 2, 2, per // (2 * L), L))
        self.keep, self.send = None, None

    def exchange(self):
        if self.stage == 0:
            return _rs_stage_c(self.full)
        return _rs_stage_ici(list(self.send), self.stage == 1)

    def absorb(self, recv):
        names = [f"rs_add{self.stage + 1}_{self.tag}_{k}" for k in range(len(self.full))]
        if self.stage == 0:
            pairs = [_rs_add1(nm, g, r, self.coords) for nm, g, r in zip(names, self.full, recv)]
            self.keep, self.send = zip(*pairs)
        elif self.stage == 1:
            pairs = [_rs_add2(nm, kp, r, self.coords) for nm, kp, r in zip(names, self.keep, recv)]
            self.keep, self.send = zip(*pairs)
        else:
            self.result = [_rs_add3(nm, kp, r) for nm, kp, r in zip(names, self.keep, recv)]
        self.stage += 1

    def finish_now(self):
        while self.stage < self.N_STAGES:
            self.absorb(_exchange_now(f"rs_x{self.stage}_{self.tag}", self.exchange()))
        return self.result


def _adamw(name, w, g, m, v):
    R, L = w.shape
    tr = _tile(R, 256, 8)

    def body(w_ref, g_ref, m_ref, v_ref, d_ref, nm_ref, nv_ref):
        gv = g_ref[...]
        nm = ADAM_B1 * m_ref[...] + (1.0 - ADAM_B1) * gv
        nv = ADAM_B2 * v_ref[...] + (1.0 - ADAM_B2) * (gv * gv)
        m_hat = nm / (1.0 - ADAM_B1 ** ADAM_STEP)
        v_hat = nv / (1.0 - ADAM_B2 ** ADAM_STEP)
        d_ref[...] = -ADAM_LR * (m_hat / (jnp.sqrt(v_hat) + ADAM_EPS) + ADAM_WD * w_ref[...])
        nm_ref[...] = nm
        nv_ref[...] = nv

    spec = pl.BlockSpec((tr, L), lambda i: (i, 0))
    out = jax.ShapeDtypeStruct((R, L), F32)
    return pl.pallas_call(body, name=name, grid=(R // tr,), in_specs=[spec] * 4, out_specs=[spec] * 3,
                          out_shape=[out, out, out], compiler_params=_cparams("parallel"))(w, g, m, v)


def _sum_devices(name, gathered):
    _, R, L = gathered.shape
    tr = _tile(R, 512, 8)

    def body(g_ref, o_ref):
        acc = g_ref[0]
        for d in range(1, NDEV):
            acc = acc + g_ref[d]
        o_ref[...] = acc

    return pl.pallas_call(body, name=name, grid=(R // tr,),
                          in_specs=[pl.BlockSpec((NDEV, tr, L), lambda i: (0, i, 0))],
                          out_specs=pl.BlockSpec((tr, L), lambda i: (i, 0)),
                          out_shape=jax.ShapeDtypeStruct((R, L), F32),
                          compiler_params=_cparams("parallel"))(gathered)


def _pack(parts):
    flat, offs, pos = [], [], 0
    for p in parts:
        n = int(np.prod(p.shape))
        padded = -(-n // PACK_ALIGN) * PACK_ALIGN
        flat.append(jnp.pad(p.reshape(-1).astype(F32), (0, padded - n)))
        offs.append((pos, n, p.shape))
        pos += padded
    return jnp.concatenate(flat).reshape(-1, LANES), offs


def _unpack(packed, offs):
    flat = packed.reshape(-1)
    return [flat[pos:pos + n].reshape(shape) for pos, n, shape in offs]


def kernel(x, norm_mix, norm_ffn, rel_bias, attn_w_qkv, attn_q_gain, attn_k_gain, attn_sinks, attn_w_o, pool_w, pool_scale, ffn_w_up, ffn_conv_w, ffn_conv_b, ffn_w_down, loss_target, m_norm_mix, m_norm_ffn, m_rel_bias, m_attn_w_qkv, m_attn_q_gain, m_attn_k_gain, m_attn_sinks, m_attn_w_o, m_pool_w, m_pool_scale, m_ffn_w_up, m_ffn_conv_w, m_ffn_conv_b, m_ffn_w_down, v_norm_mix, v_norm_ffn, v_rel_bias, v_attn_w_qkv, v_attn_q_gain, v_attn_k_gain, v_attn_sinks, v_attn_w_o, v_pool_w, v_pool_scale, v_ffn_w_up, v_ffn_conv_w, v_ffn_conv_b, v_ffn_w_down):
    xs = x[0]
    target = loss_target[0]
    S, D = xs.shape
    depth = norm_mix.shape[0]
    H = D // HEAD_DIM
    n_attn, n_pool = attn_w_qkv.shape[0], pool_w.shape[0]
    QS = attn_w_qkv.shape[2]
    CB = ffn_w_up.shape[2]
    FB = ffn_w_down.shape[1]
    FF = FB * NDEV
    G, PC, C = pool_w.shape[1], pool_w.shape[2], pool_w.shape[3]
    xi, yi, ci = _coords()
    me = 4 * xi + 2 * yi + ci
    coords = _coord_vec()

    def ffn_allgather(i):
        return _AllGather([_cols_item(ffn_w_up[i].astype(BF16), NDEV * CB),
                           _rows_item(ffn_w_down[i].astype(BF16), FF)])

    def mixer_allgather(i):
        j = i // 2
        if i % 2 == 0:
            return _AllGather([_lead_item(attn_w_qkv[j].astype(BF16)), _rows_item(attn_w_o[j].astype(BF16), D)])
        return _AllGather([_pool_item(pool_w[j].astype(BF16), C)])

    def layer_allgather(i):
        return _AllGather(ffn_allgather(i).items + mixer_allgather(i).items)

    def unpack_ffn(outs):
        return {"up": outs[0], "down": outs[1]}

    def unpack_mixer(i, outs):
        if i % 2 == 0:
            return {"qkv": outs[0].transpose(1, 0, 2).reshape(D, NDEV * QS), "o": outs[1]}
        return {"pool": outs[0]}

    weights_of = [None] * depth
    weights_of[0] = unpack_mixer(0, _allgather_now("allgather_mixer_0", mixer_allgather(0)))

    small_in, small_in_offs = _pack([ffn_conv_w, pool_scale])
    gathered_in = _allgather_small("allgather_small_params", small_in).reshape(NDEV, -1)
    per_dev = [_unpack(gathered_in[d], small_in_offs) for d in range(NDEV)]
    conv_w_full = jnp.concatenate([p[0] for p in per_dev], axis=2)
    pool_scale_full = jnp.concatenate([p[1] for p in per_dev], axis=1)

    onehot = jnp.asarray(_t5_onehot())
    bias = _bias_band(rel_bias, onehot).reshape(H, WINDOW, 2 * WINDOW)

    saved = []
    cur = xs
    for i in range(depth):
        j = i // 2
        w = weights_of[i]
        st = {"x0": cur}
        if i % 2 == 0:
            h = _rmsnorm_fwd(f"norm_mix_{i}", cur, norm_mix[i:i + 1])
            qkv = _mm_rows(f"qkv_{i}", h, w["qkv"], NN, F32)
            qg, kg, sk = attn_q_gain[j:j + 1], attn_k_gain[j:j + 1], attn_sinks[j].reshape(H, 1)
            ag0 = _AllGather(ffn_allgather(0).items[:1]) if i == 0 else None
            o, lse, *gathered_up = _attn_fwd(f"attn_fwd_{i}", qkv, bias, qg, kg, sk, ag0)
            if ag0:
                w["up"] = gathered_up[0]
            x1 = _mm_rows(f"attn_out_{i}", o, w["o"], NN, F32, res=cur)
            st.update(h=h, qkv=qkv, o=o, lse=lse)
        else:
            x1, dsave = _pool_fwd(f"pool_fwd_{i}", cur, norm_mix[i:i + 1], w["pool"], pool_scale_full[j:j + 1])
            st.update(dsave=dsave)
        h2 = _rmsnorm_fwd(f"norm_ffn_{i}", x1, norm_ffn[i:i + 1])
        own_down = ffn_allgather(i).items[1:] if "down" not in w else []
        ag = _AllGather(own_down + (layer_allgather(i + 1).items if i + 1 < depth else []))
        ag = ag if ag.items else None
        u3, ab3, a, *gathered_next = _ffn_up_fwd(f"ffn_up_{i}", h2, w["up"], conv_w_full[i], ffn_conv_b[i:i + 1], ag)
        if own_down:
            w["down"] = gathered_next.pop(0)
        if i + 1 < depth:
            weights_of[i + 1] = {**unpack_ffn(gathered_next[:2]), **unpack_mixer(i + 1, gathered_next[2:])}
        cur = _ffn_down_fwd(f"ffn_down_{i}", a, w["down"], x1)
        st.update(x1=x1, h2=h2, u3=u3, ab3=ab3, a=a)
        saved.append(st)

    loss_tile, dcur, dcur_b = _loss_fwd_bwd(cur, target)

    g_up_l, g_down_l = [None] * depth, [None] * depth
    g_qkv_l, g_o_l, g_pool_l = [None] * n_attn, [None] * n_attn, [None] * n_pool
    d_norm_mix, d_norm_ffn = [None] * depth, [None] * depth
    d_conv_w, d_conv_b = [None] * depth, [None] * depth
    d_qg, d_kg, d_sinks, d_pscale = [None] * n_attn, [None] * n_attn, [None] * n_attn, [None] * n_pool
    d_band = None

    def store_ffn(i, red):
        g_up_l[i] = red[0].reshape(D, CB)
        g_down_l[i] = red[1].reshape(FB, D)

    def store_mixer(i, red):
        if i % 2 == 0:
            g_qkv_l[i // 2] = red[0].reshape(D, QS)
            g_o_l[i // 2] = red[1].reshape(D // NDEV, D)
        else:
            g_pool_l[i // 2] = red[0].reshape(G, PC, C)

    def carrying(rs, call):
        if rs is None or rs.stage >= rs.N_STAGES:
            return call(None)
        out, *recv = call(rs.exchange())
        rs.absorb(recv)
        return out

    pending = None
    for i in reversed(range(depth)):
        j = i // 2
        st = saved[i]
        w = weights_of[i]
        du3, dc = _ffn_da_bwd(f"ffn_da_{i}", dcur_b, w["down"], st["u3"], st["ab3"], conv_w_full[i])
        d_conv_w[i] = jnp.concatenate([dc[0, 0:3], dc[1, 0:3]], axis=1)
        d_conv_b[i] = jnp.concatenate([dc[0, 3], dc[1, 3]], axis=0)
        dwdown = carrying(pending, lambda ex: _mm_tn(f"ffn_dwdown_{i}", st["a"], dcur_b,
                                                     tm_pref=CB if CB % LANES == 0 else 1024, comm=ex))
        dwdown = dwdown.reshape(NDEV, FB, D)
        dwup = carrying(pending, lambda ex: _ffn_dwup_bwd(f"ffn_dwup_{i}", st["h2"], du3, comm=ex))
        dh2 = carrying(pending, lambda ex: _ffn_dh_bwd(f"ffn_dh_{i}", du3, w["up"], comm=ex))
        if pending is not None:
            red = pending.finish_now()
            store_ffn(i + 1, red[:2])
            store_mixer(i + 1, red[2:])
        dx1, dx1_b, dg = _rmsnorm_bwd(f"norm_ffn_bwd_{i}", dh2, st["x1"], norm_ffn[i:i + 1], dcur)
        d_norm_ffn[i] = dg[0]
        own = _ReduceScatter(f"l{i}f", [dwup, dwdown], coords) if i == 0 else None
        if i % 2 == 0:
            do = _mm_rows(f"attn_do_{i}", dx1_b, w["o"], NT, BF16)
            dwo = carrying(own, lambda ex: _mm_tn(f"attn_dwo_{i}", st["o"], dx1_b, comm=ex)).reshape(NDEV, D // NDEV, D)
            qg, kg, sk = attn_q_gain[j:j + 1], attn_k_gain[j:j + 1], attn_sinks[j].reshape(H, 1)
            ex = own.exchange() if own is not None and own.stage < own.N_STAGES else None
            dq, dkv, db, dsk, dqg, dkg, *recv = _attn_bwd(f"attn_bwd_{i}", st["qkv"], do, st["lse"], bias, qg, kg, sk,
                                                          comm=ex)
            if ex is not None:
                own.absorb(recv)
            d_band = db if d_band is None else d_band + db
            d_sinks[j], d_qg[j], d_kg[j] = dsk[:, 0], dqg.reshape(H, HEAD_DIM).sum(axis=0), dkg[0]
            dqkv = jnp.concatenate([dq, dkv], axis=1)
            dwqkv = carrying(own, lambda ex: _mm_tn(f"attn_dwqkv_{i}", st["h"], dqkv, tn_pref=1280, comm=ex))
            dwqkv = dwqkv.reshape(D, NDEV, QS).transpose(1, 0, 2)
            dh = _mm_rows(f"attn_dh_{i}", dqkv, w["qkv"], NT, F32)
            dcur, dcur_b, dg = _rmsnorm_bwd(f"norm_mix_bwd_{i}", dh, st["x0"], norm_mix[i:i + 1], dx1)
            mixer_grads = [dwqkv, dwo]
        else:
            dcur, dcur_b, dwp, dps, dg = _pool_bwd(f"pool_bwd_{i}", dx1, st["x0"], norm_mix[i:i + 1], w["pool"],
                                                   pool_scale_full[j:j + 1], st["dsave"])
            d_pscale[j] = dps[0]
            mixer_grads = [dwp.reshape(G, NDEV, PC, C).transpose(1, 0, 2, 3)]
        d_norm_mix[i] = dg[0]
        if own is not None:
            store_ffn(i, own.finish_now())
            store_mixer(i, _ReduceScatter(f"l{i}m", mixer_grads, coords).finish_now())
        else:
            pending = _ReduceScatter(f"l{i}", [dwup, dwdown] + mixer_grads, coords)

    d_rel = _bias_band_bwd(d_band.reshape(H, -1), onehot)

    small_parts = [loss_tile, jnp.stack(d_norm_mix), jnp.stack(d_norm_ffn), d_rel, jnp.stack(d_qg),
                   jnp.stack(d_kg), jnp.stack(d_sinks), jnp.stack(d_pscale), jnp.stack(d_conv_w),
                   jnp.stack(d_conv_b)]
    small, small_offs = _pack(small_parts)
    gathered = _allgather_small("allgather_small_grads", small).reshape(NDEV, -1, LANES)
    summed = _unpack(_sum_devices("sum_small_grads", gathered), small_offs)
    loss = summed[0][0, 0]
    (g_norm_mix, g_norm_ffn, g_rel, g_qg, g_kg, g_sinks, g_pscale_full, g_conv_w_full, g_conv_b) = summed[1:]
    g_pscale = lax.dynamic_slice_in_dim(g_pscale_full, me * (D // NDEV), D // NDEV, axis=1)
    g_conv_w = lax.dynamic_slice_in_dim(g_conv_w_full, me * CB, CB, axis=2)

    grads = {
        "norm_mix": g_norm_mix, "norm_ffn": g_norm_ffn, "rel_bias": g_rel, "attn_w_qkv": jnp.stack(g_qkv_l),
        "attn_q_gain": g_qg, "attn_k_gain": g_kg, "attn_sinks": g_sinks, "attn_w_o": jnp.stack(g_o_l),
        "pool_w": jnp.stack(g_pool_l), "pool_scale": g_pscale, "ffn_w_up": jnp.stack(g_up_l),
        "ffn_conv_w": g_conv_w, "ffn_conv_b": g_conv_b, "ffn_w_down": jnp.stack(g_down_l),
    }
    weights = {
        "norm_mix": (norm_mix, m_norm_mix, v_norm_mix), "norm_ffn": (norm_ffn, m_norm_ffn, v_norm_ffn),
        "rel_bias": (rel_bias, m_rel_bias, v_rel_bias), "attn_w_qkv": (attn_w_qkv, m_attn_w_qkv, v_attn_w_qkv),
        "attn_q_gain": (attn_q_gain, m_attn_q_gain, v_attn_q_gain),
        "attn_k_gain": (attn_k_gain, m_attn_k_gain, v_attn_k_gain),
        "attn_sinks": (attn_sinks, m_attn_sinks, v_attn_sinks), "attn_w_o": (attn_w_o, m_attn_w_o, v_attn_w_o),
        "pool_w": (pool_w, m_pool_w, v_pool_w), "pool_scale": (pool_scale, m_pool_scale, v_pool_scale),
        "ffn_w_up": (ffn_w_up, m_ffn_w_up, v_ffn_w_up), "ffn_conv_w": (ffn_conv_w, m_ffn_conv_w, v_ffn_conv_w),
        "ffn_conv_b": (ffn_conv_b, m_ffn_conv_b, v_ffn_conv_b), "ffn_w_down": (ffn_w_down, m_ffn_w_down, v_ffn_w_down),
    }
    names = list(weights)
    big = ("attn_w_qkv", "attn_w_o", "pool_w", "ffn_w_up", "ffn_w_down")
    upd = {}
    for nm in big:
        w, m, v = weights[nm]
        two_d = lambda t: t.reshape(-1, w.shape[-1])
        d_, m_, v_ = _adamw(f"adamw_{nm}", two_d(w), two_d(grads[nm]), two_d(m), two_d(v))
        upd[nm] = (d_.reshape(w.shape), m_.reshape(w.shape), v_.reshape(w.shape))
    small_names = [nm for nm in names if nm not in big]
    pw, offs = _pack([weights[nm][0] for nm in small_names])
    pg, _ = _pack([grads[nm] for nm in small_names])
    pm, _ = _pack([weights[nm][1] for nm in small_names])
    pv, _ = _pack([weights[nm][2] for nm in small_names])
    d_, m_, v_ = _adamw("adamw_small", pw, pg, pm, pv)
    for nm, dd, mm, vv in zip(small_names, _unpack(d_, offs), _unpack(m_, offs), _unpack(v_, offs)):
        upd[nm] = (dd, mm, vv)

    grad_x = dcur[None]
    return (loss, grad_x, *[grads[nm].reshape(weights[nm][0].shape) for nm in names],
            *[upd[nm][0] for nm in names], *[upd[nm][1] for nm in names], *[upd[nm][2] for nm in names])
```
